```python
import math
import jax
import jax.numpy as jnp
from jax import lax
import numpy as np

D_MODEL = 1024
BATCH = 4
SEQ = 4096
DEPTH = 4
DEC_BATCH = 128
DEC_SEQ = 4
PAST_LEN = 2048
PAGE_SIZE = 128

N_EVEN = (DEPTH + 1) // 2
N_ODD = DEPTH // 2
NORM_EPS = 1e-6
N_PAGES = PAST_LEN // PAGE_SIZE
N_POOL = (DEC_BATCH * N_PAGES * 5) // 4

D_A = D_MODEL
HD_A = 64
H_A = D_A // HD_A
W_LORA = 64
A_LORA = 64
A_SHIFT = 3 * D_A + W_LORA + A_LORA
RWKV_LN_EPS = 64e-5
D_B = D_MODEL
P_B = 64
H_B = D_B // P_B
N_B = 128
G_B = 4
HPG_B = H_B // G_B
CONV_W = 4
CONV_DIM = D_B + 2 * G_B * N_B
SSD_CHUNK = 128
D_C = D_MODEL // 2
CH_C = 16
G_C = D_C // CH_C
P_C = 64
H_D = 8
DH_D = 128
KVH_D = 2
QPK_D = H_D // KVH_D
D_D = H_D * DH_D
HI_D = 8
DI_D = 64
IDX_SCALE = (DI_D ** -0.5) * (HI_D ** -0.5)
TOPK_MAX = 256
QBLK = 128
ROPE_THETA = 500000.0
ROPE_FRAC = 4

IN_E = A_SHIFT + D_A + D_B + CONV_DIM + H_B
OUT_E = D_A + D_B
IN_O = 2 * D_C + D_D + 2 * KVH_D * DH_D + HI_D * DI_D + DI_D + HI_D + D_D
OUT_O = D_C + D_D

kernel_name = 'hybrid_rwkv7_ssd_s5_dsa_step'


def _split(x, sizes):
    return jnp.split(x, np.cumsum(sizes)[:-1].tolist(), axis=-1)


def rmsnorm(x, g):
    xf = x.astype(jnp.float32)
    xf = xf * lax.rsqrt(jnp.mean(xf * xf, axis=-1, keepdims=True) + NORM_EPS)
    return (xf * g.astype(jnp.float32)).astype(x.dtype)


def partial_rotary(x, pos):
    rd = x.shape[-1] // ROPE_FRAC
    half = rd // 2
    inv_freq = ROPE_THETA ** (-jnp.arange(half, dtype=jnp.float32) / half)
    ang = pos.astype(jnp.float32)[:, None] * inv_freq[None, :]
    cos = jnp.cos(ang)[:, None, :]
    sin = jnp.sin(ang)[:, None, :]
    xf = x.astype(jnp.float32)
    x1, x2 = xf[..., :half], xf[..., half:rd]
    rot = jnp.concatenate([x1 * cos - x2 * sin, x1 * sin + x2 * cos, xf[..., rd:]], axis=-1)
    return rot.astype(x.dtype)


def wkv_step(S, inp):
    r, w, k, v, kk, a = inp
    sa = jnp.einsum('bhvk,bhk->bhv', S, -kk)
    S = S * w[:, :, None, :] + sa[..., None] * (kk * a)[:, :, None, :] + v[..., None] * k[:, :, None, :]
    return S, jnp.einsum('bhvk,bhk->bhv', S, r)


def rwkv7_branch(pa, gate, shift0, wkv0, mu, w0, w2, a0, a2, k_k, k_a, r_k, ln_g, ln_b):
    f32 = jnp.float32
    b, l, _ = pa.shape
    paf = pa.astype(f32)
    prev = jnp.concatenate([shift0.astype(f32)[:, None], paf[:, :-1]], axis=1)
    xs = paf + (prev - paf) * mu.astype(f32)
    r, k, v, wl, al = _split(xs, [D_A, D_A, D_A, W_LORA, A_LORA])
    w_log = -jax.nn.softplus(-(w0.astype(f32) + jnp.tanh(wl) @ w2.astype(f32))) - 0.5
    decay = jnp.exp(-jnp.exp(w_log))
    a = jax.nn.sigmoid(a0.astype(f32) + al @ a2.astype(f32))
    kk = (k * k_k.astype(f32)).reshape(b, l, H_A, HD_A)
    kk = kk / jnp.maximum(jnp.sqrt(jnp.sum(kk * kk, axis=-1, keepdims=True)), 1e-12)
    k = k * (1.0 + (a - 1.0) * k_a.astype(f32))
    r, decay, k, v, a = [t.reshape(b, l, H_A, HD_A) for t in (r, decay, k, v, a)]
    seq = tuple(jnp.moveaxis(t, 1, 0) for t in (r, decay, k, v, kk, a))
    S_last, out = lax.scan(wkv_step, wkv0.astype(f32), seq)
    out = jnp.moveaxis(out, 0, 1)
    mean = jnp.mean(out, axis=-1, keepdims=True)
    var = jnp.mean(jnp.square(out - mean), axis=-1, keepdims=True)
    out = ((out - mean) * lax.rsqrt(var + RWKV_LN_EPS)).reshape(b, l, D_A)
    out = out * ln_g.astype(f32) + ln_b.astype(f32)
    bonus = jnp.sum(r * k * r_k.astype(f32), axis=-1, keepdims=True) * v
    out = (out + bonus.reshape(b, l, D_A)) * jax.nn.silu(gate.astype(f32))
    return out.astype(pa.dtype), S_last, pa[:, -1]


def segsum(x):
    T = x.shape[-1]
    xr = jnp.broadcast_to(x[..., None], x.shape + (T,))
    xr = jnp.where(jnp.tril(jnp.ones((T, T), bool), -1), xr, 0.0)
    cs = jnp.cumsum(xr, axis=-2)
    return jnp.where(jnp.tril(jnp.ones((T, T), bool), 0), cs, -jnp.inf)


def ssd_chunked(x, a, bm, cm, h0):
    b, l = x.shape[:2]
    cl = math.gcd(l, SSD_CHUNK)
    nc = l // cl
    x = x.reshape(b, nc, cl, G_B, HPG_B, P_B)
    a = a.reshape(b, nc, cl, G_B, HPG_B).transpose(0, 3, 4, 1, 2)
    bm = bm.reshape(b, nc, cl, G_B, N_B)
    cm = cm.reshape(b, nc, cl, G_B, N_B)
    a_cum = jnp.cumsum(a, axis=-1)
    lmat = jnp.exp(segsum(a))
    cb = jnp.einsum('bclgn,bcsgn->bcgls', cm, bm)
    y_diag = jnp.einsum('bcgls,bgrcls,bcsgrp->bclgrp', cb, lmat, x)
    decay_states = jnp.exp(a_cum[..., -1:] - a_cum)
    states = jnp.einsum('bcsgn,bgrcs,bcsgrp->bcgrpn', bm, decay_states, x)
    states = jnp.concatenate([h0.reshape(b, 1, G_B, HPG_B, P_B, N_B), states], axis=1)
    chunk_tot = jnp.pad(a_cum[..., -1], ((0, 0), (0, 0), (0, 0), (1, 0)))
    decay_chunk = jnp.exp(segsum(chunk_tot))
    new_states = jnp.einsum('bgrzc,bcgrpn->bzgrpn', decay_chunk, states)
    states, h_last = new_states[:, :-1], new_states[:, -1]
    y_off = jnp.einsum('bclgn,bcgrpn,bgrcl->bclgrp', cm, states, jnp.exp(a_cum))
    y = (y_diag + y_off).reshape(b, l, H_B, P_B)
    return y, h_last.reshape(b, H_B, P_B, N_B)


def ssd_branch(z, xbc, dt_raw, conv0, ssm0, conv_w, conv_b, dt_bias, a_log, d_skip, gnorm_g):
    f32 = jnp.float32
    b, l, _ = xbc.shape
    ext = jnp.concatenate([conv0.astype(xbc.dtype), xbc], axis=1)
    conv = conv_b.astype(f32) + sum(ext[:, i:i + l].astype(f32) * conv_w[i].astype(f32) for i in range(CONV_W))
    conv_new = ext[:, l:]
    xs, bm, cm = _split(jax.nn.silu(conv), [D_B, G_B * N_B, G_B * N_B])
    xs = xs.reshape(b, l, H_B, P_B)
    bm = bm.reshape(b, l, G_B, N_B)
    cm = cm.reshape(b, l, G_B, N_B)
    dt = jax.nn.softplus(dt_raw.astype(f32) + dt_bias.astype(f32))
    a_cont = -jnp.exp(a_log.astype(f32))
    y, ssm_new = ssd_chunked(xs * dt[..., None], dt * a_cont, bm, cm, ssm0.astype(f32))
    y = y + xs * d_skip.astype(f32)[:, None]
    y = y.reshape(b, l, D_B) * jax.nn.silu(z.astype(f32))
    yg = y.reshape(b, l, G_B, D_B // G_B)
    yg = yg * lax.rsqrt(jnp.mean(yg * yg, axis=-1, keepdims=True) + NORM_EPS)
    out = yg.reshape(b, l, D_B) * gnorm_g.astype(f32)
    return out.astype(z.dtype), ssm_new, conv_new


def even_mixer(h, shift0, wkv0, conv0, ssm0, w_in, w_out, mu, w0, w2, a0, a2, k_k, k_a, r_k,
               ln_g, ln_b, conv_w, conv_b, dt_bias, a_log, d_skip, gnorm_g):
    proj = h @ w_in
    pa, ga, z, xbc, dt_raw = _split(proj, [A_SHIFT, D_A, D_B, CONV_DIM, H_B])
    out_a, wkv_new, shift_new = rwkv7_branch(pa, ga, shift0, wkv0, mu, w0, w2, a0, a2, k_k, k_a, r_k, ln_g, ln_b)
    out_b, ssm_new, conv_new = ssd_branch(z, xbc, dt_raw, conv0, ssm0, conv_w, conv_b, dt_bias, a_log, d_skip, gnorm_g)
    y = jnp.concatenate([out_a, out_b], axis=-1) @ w_out
    return y, (wkv_new, shift_new, ssm_new, conv_new)


def s5_combine(e1, e2):
    a1, b1 = e1
    a2, b2 = e2
    return a1 * a2, a2 * b1 + b2


def s5_branch(u, gate, h0_re, h0_im, lam_re, lam_im, log_dt, b_re, b_im, c_re, c_im, d_skip, glu_w, glu_b):
    f32 = jnp.float32
    b, l, _ = u.shape
    lam = lax.complex(lam_re.astype(f32), lam_im.astype(f32))
    delta = jnp.exp(log_dt.astype(f32))[:, None]
    lam_bar = jnp.exp(lam * delta)
    b_bar = ((lam_bar - 1.0) / lam)[..., None] * lax.complex(b_re.astype(f32), b_im.astype(f32))
    c_cplx = lax.complex(c_re.astype(f32), c_im.astype(f32))
    uf = u.astype(f32)
    bu = jnp.einsum('gpm,blgm->blgp', b_bar, uf.reshape(b, l, G_C, CH_C).astype(jnp.complex64))
    h0 = lax.complex(h0_re.astype(f32), h0_im.astype(f32))
    bu = bu.at[:, 0].add(lam_bar[None] * h0)
    _, hs = lax.associative_scan(s5_combine, (jnp.broadcast_to(lam_bar, bu.shape), bu), axis=1)
    y = jnp.real(jnp.einsum('gmp,blgp->blgm', c_cplx, hs)).reshape(b, l, D_C) + d_skip.astype(f32) * uf
    y = jax.nn.gelu(y)
    y = y * jax.nn.sigmoid(y @ glu_w.astype(f32) + glu_b.astype(f32))
    y = y * jax.nn.silu(gate.astype(f32))
    h_last = hs[:, -1]
    return y.astype(u.dtype), jnp.real(h_last), jnp.imag(h_last)


def dsa_select_attend(q, qi, wi, tpos, ki_all, gather_kv, n_sel):
    f32 = jnp.float32
    b, t = q.shape[:2]
    n_keys = ki_all.shape[1]
    dots = jnp.einsum('bthd,bsd->bths', qi.astype(f32), ki_all.astype(f32))
    score = jnp.einsum('bth,bths->bts', wi.astype(f32), jax.nn.relu(dots)) * IDX_SCALE
    admissible = jnp.arange(n_keys)[None, :] <= tpos[:, None]
    score = jnp.where(admissible[None], score, -jnp.inf)
    top_val, top_idx = lax.top_k(score, n_sel)
    valid = jnp.isfinite(top_val)
    k_sel, v_sel = gather_kv(top_idx)
    logits = jnp.einsum('btjgd,btnjd->btjgn', q.astype(f32), k_sel.astype(f32)) * (DH_D ** -0.5)
    logits = jnp.where(valid[:, :, None, None, :], logits, -jnp.inf)
    probs = jax.nn.softmax(logits, axis=-1)
    out = jnp.einsum('btjgn,btnjd->btjgd', probs, v_sel.astype(f32))
    return out.reshape(b, t, D_D)


def dsa_prompt(q, k, v, qi, ki, wi, pos):
    b, l = q.shape[:2]
    n_sel = min(TOPK_MAX, l // 4)
    qb = math.gcd(l, QBLK)
    nb = l // qb
    bidx = jnp.arange(b)[:, None, None]

    def gather_kv(idx):
        return k[bidx, idx], v[bidx, idx]

    def blk(t):
        return jnp.moveaxis(t.reshape((b, nb, qb) + t.shape[2:]), 1, 0)

    def one_block(args):
        qq, qqi, wwi, pp = args
        return dsa_select_attend(qq, qqi, wwi, pp, ki, gather_kv, n_sel)

    out = lax.map(one_block, (blk(q), blk(qi), blk(wi), pos.reshape(nb, qb)))
    return jnp.moveaxis(out, 0, 1).reshape(b, l, D_D)


def make_dsa_sample(cache_k, cache_v, cache_ki, page_table):
    past_len = page_table.shape[1] * PAGE_SIZE

    def attend(q, k, v, qi, ki, wi, pos):
        b, t = q.shape[:2]
        n_sel = min(TOPK_MAX, (past_len + t) // 4)
        ki_past = cache_ki[page_table].reshape(b, past_len, DI_D)
        ki_all = jnp.concatenate([ki_past.astype(ki.dtype), ki], axis=1)
        k_flat = cache_k.reshape(-1, KVH_D, DH_D)
        v_flat = cache_v.reshape(-1, KVH_D, DH_D)
        bidx = jnp.arange(b)[:, None, None]

        def gather_kv(idx):
            is_past = (idx < past_len)[..., None, None]
            ic = jnp.minimum(idx, past_len - 1)
            phys = page_table[bidx, ic // PAGE_SIZE] * PAGE_SIZE + ic % PAGE_SIZE
            inew = jnp.clip(idx - past_len, 0, t - 1)
            k_sel = jnp.where(is_past, k_flat[phys].astype(k.dtype), k[bidx, inew])
            v_sel = jnp.where(is_past, v_flat[phys].astype(v.dtype), v[bidx, inew])
            return k_sel, v_sel

        return dsa_select_attend(q, qi, wi, pos, ki_all, gather_kv, n_sel)

    return attend


def odd_mixer(h, pos, c_re0, c_im0, attend, w_in, w_out, lam_re, lam_im, log_dt, b_re, b_im,
              c_re, c_im, d_skip, glu_w, glu_b):
    b, l, _ = h.shape
    proj = h @ w_in
    u, gc, q, k, v, qi, ki, wi, gd = _split(
        proj, [D_C, D_C, D_D, KVH_D * DH_D, KVH_D * DH_D, HI_D * DI_D, DI_D, HI_D, D_D])
    out_c, re_last, im_last = s5_branch(u, gc, c_re0, c_im0, lam_re, lam_im, log_dt, b_re, b_im,
                                        c_re, c_im, d_skip, glu_w, glu_b)
    q = partial_rotary(q.reshape(b, l, H_D, DH_D), pos).reshape(b, l, KVH_D, QPK_D, DH_D)
    k = partial_rotary(k.reshape(b, l, KVH_D, DH_D), pos)
    v = v.reshape(b, l, KVH_D, DH_D)
    qi = partial_rotary(qi.reshape(b, l, HI_D, DI_D), pos)
    ki = partial_rotary(ki[:, :, None, :], pos)[:, :, 0]
    out_d = attend(q, k, v, qi, ki, wi, pos) * jax.nn.silu(gd.astype(jnp.float32))
    y = jnp.concatenate([out_c, out_d.astype(h.dtype)], axis=-1) @ w_out
    return y, (re_last, im_last, k, v, ki)


def setup_inputs(seed: int = 0) -> dict:
    key = jax.random.key(seed)
    ks = iter(jax.random.split(key, 64))
    f32 = jnp.float32

    def nrm(shape, scale):
        return jax.random.normal(next(ks), shape, f32) * scale

    def unif(shape, lo, hi):
        return jax.random.uniform(next(ks), shape, f32, lo, hi)

    x_prompt = nrm((BATCH, SEQ, D_MODEL), 1.0)
    x_sample = nrm((DEC_BATCH, DEC_SEQ, D_MODEL), 1.0)
    state_a_wkv = nrm((N_EVEN, DEC_BATCH, H_A, HD_A, HD_A), 0.3)
    state_a_shift = nrm((N_EVEN, DEC_BATCH, A_SHIFT), 1.0)
    state_b_ssm = nrm((N_EVEN, DEC_BATCH, H_B, P_B, N_B), 0.1)
    state_b_conv = nrm((N_EVEN, DEC_BATCH, CONV_W - 1, CONV_DIM), 1.0)
    state_c_re = nrm((N_ODD, DEC_BATCH, G_C, P_C), 0.1)
    state_c_im = nrm((N_ODD, DEC_BATCH, G_C, P_C), 0.1)
    cache_d_k = nrm((N_ODD, N_POOL, PAGE_SIZE, KVH_D, DH_D), 1.0)
    cache_d_v = nrm((N_ODD, N_POOL, PAGE_SIZE, KVH_D, DH_D), 1.0)
    cache_d_kidx = nrm((N_ODD, N_POOL, PAGE_SIZE, DI_D), 1.0)
    page_table = jax.random.permutation(next(ks), N_POOL)[: DEC_BATCH * N_PAGES].reshape(
        DEC_BATCH, N_PAGES).astype(jnp.int32)
    norm_g = 1.0 + nrm((DEPTH, D_MODEL), 0.02)
    final_norm_g = 1.0 + nrm((D_MODEL,), 0.02)
    w_in_e = nrm((N_EVEN, D_MODEL, IN_E), D_MODEL ** -0.5)
    w_out_e = nrm((N_EVEN, OUT_E, D_MODEL), 0.5 * OUT_E ** -0.5)
    rwkv_mu = unif((N_EVEN, A_SHIFT), 0.0, 1.0)
    rwkv_w0 = jnp.linspace(-6.0, 1.0, D_A, dtype=f32)[None, :] + nrm((N_EVEN, D_A), 0.1)
    rwkv_w2 = nrm((N_EVEN, W_LORA, D_A), 0.5 * W_LORA ** -0.5)
    rwkv_a0 = nrm((N_EVEN, D_A), 0.1)
    rwkv_a2 = nrm((N_EVEN, A_LORA, D_A), 0.5 * A_LORA ** -0.5)
    rwkv_kk = 0.85 + nrm((N_EVEN, D_A), 0.02)
    rwkv_ka = 1.0 + nrm((N_EVEN, D_A), 0.02)
    rwkv_rk = nrm((N_EVEN, H_A, HD_A), 0.1)
    rwkv_ln_g = 1.0 + nrm((N_EVEN, D_A), 0.02)
    rwkv_ln_b = nrm((N_EVEN, D_A), 0.02)
    ssd_conv_w = nrm((N_EVEN, CONV_W, CONV_DIM), CONV_W ** -0.5)
    ssd_conv_b = nrm((N_EVEN, CONV_DIM), 0.02)
    dt0 = jnp.exp(unif((N_EVEN, H_B), math.log(1e-3), math.log(1e-1)))
    ssd_dt_bias = dt0 + jnp.log(-jnp.expm1(-dt0))
    ssd_a_log = jnp.log(unif((N_EVEN, H_B), 1.0, 16.0))
    ssd_d = 1.0 + nrm((N_EVEN, H_B), 0.1)
    ssd_norm_g = 1.0 + nrm((N_EVEN, D_B), 0.02)
    w_in_o = nrm((N_ODD, D_MODEL, IN_O), D_MODEL ** -0.5)
    w_out_o = nrm((N_ODD, OUT_O, D_MODEL), 0.5 * OUT_O ** -0.5)
    s5_lam_re = -0.5 + nrm((N_ODD, G_C, P_C), 0.01)
    s5_lam_im = math.pi * jnp.arange(P_C, dtype=f32)[None, None, :] + nrm((N_ODD, G_C, P_C), 0.01)
    s5_log_dt = unif((N_ODD, G_C), math.log(1e-3), math.log(1e-1))
    s5_b_re = nrm((N_ODD, G_C, P_C, CH_C), (2 * CH_C) ** -0.5)
    s5_b_im = nrm((N_ODD, G_C, P_C, CH_C), (2 * CH_C) ** -0.5)
    s5_c_re = nrm((N_ODD, G_C, CH_C, P_C), P_C ** -0.5)
    s5_c_im = nrm((N_ODD, G_C, CH_C, P_C), P_C ** -0.5)
    s5_d = nrm((N_ODD, D_C), 1.0)
    s5_glu_w = nrm((N_ODD, D_C, D_C), D_C ** -0.5)
    s5_glu_b = nrm((N_ODD, D_C), 0.02)
    return {
        'x_prompt': x_prompt, 'x_sample': x_sample,
        'state_a_wkv': state_a_wkv, 'state_a_shift': state_a_shift,
        'state_b_ssm': state_b_ssm, 'state_b_conv': state_b_conv,
        'state_c_re': state_c_re, 'state_c_im': state_c_im,
        'cache_d_k': cache_d_k, 'cache_d_v': cache_d_v, 'cache_d_kidx': cache_d_kidx,
        'page_table': page_table,
        'norm_g': norm_g, 'final_norm_g': final_norm_g,
        'w_in_e': w_in_e, 'w_out_e': w_out_e,
        'rwkv_mu': rwkv_mu, 'rwkv_w0': rwkv_w0, 'rwkv_w2': rwkv_w2, 'rwkv_a0': rwkv_a0, 'rwkv_a2': rwkv_a2,
        'rwkv_kk': rwkv_kk, 'rwkv_ka': rwkv_ka, 'rwkv_rk': rwkv_rk, 'rwkv_ln_g': rwkv_ln_g, 'rwkv_ln_b': rwkv_ln_b,
        'ssd_conv_w': ssd_conv_w, 'ssd_conv_b': ssd_conv_b, 'ssd_dt_bias': ssd_dt_bias, 'ssd_a_log': ssd_a_log,
        'ssd_d': ssd_d, 'ssd_norm_g': ssd_norm_g,
        'w_in_o': w_in_o, 'w_out_o': w_out_o,
        's5_lam_re': s5_lam_re, 's5_lam_im': s5_lam_im, 's5_log_dt': s5_log_dt,
        's5_b_re': s5_b_re, 's5_b_im': s5_b_im, 's5_c_re': s5_c_re, 's5_c_im': s5_c_im,
        's5_d': s5_d, 's5_glu_w': s5_glu_w, 's5_glu_b': s5_glu_b,
    }


def reference(x_prompt, x_sample, state_a_wkv, state_a_shift, state_b_ssm, state_b_conv,
              state_c_re, state_c_im, cache_d_k, cache_d_v, cache_d_kidx, page_table,
              norm_g, final_norm_g, w_in_e, w_out_e, rwkv_mu, rwkv_w0, rwkv_w2, rwkv_a0, rwkv_a2,
              rwkv_kk, rwkv_ka, rwkv_rk, rwkv_ln_g, rwkv_ln_b, ssd_conv_w, ssd_conv_b, ssd_dt_bias,
              ssd_a_log, ssd_d, ssd_norm_g, w_in_o, w_out_o, s5_lam_re, s5_lam_im, s5_log_dt,
              s5_b_re, s5_b_im, s5_c_re, s5_c_im, s5_d, s5_glu_w, s5_glu_b):
    f32 = jnp.float32
    bp, lp = x_prompt.shape[:2]
    pos_p = jnp.arange(lp)
    pos_s = PAST_LEN + jnp.arange(x_sample.shape[1])
    xp, xs = x_prompt, x_sample
    even_p, even_s, odd_p, odd_s = [], [], [], []
    for i in range(DEPTH):
        j = i // 2
        hp = rmsnorm(xp, norm_g[i])
        hs = rmsnorm(xs, norm_g[i])
        if i % 2 == 0:
            pe = (w_in_e[j], w_out_e[j], rwkv_mu[j], rwkv_w0[j], rwkv_w2[j], rwkv_a0[j], rwkv_a2[j],
                  rwkv_kk[j], rwkv_ka[j], rwkv_rk[j], rwkv_ln_g[j], rwkv_ln_b[j], ssd_conv_w[j],
                  ssd_conv_b[j], ssd_dt_bias[j], ssd_a_log[j], ssd_d[j], ssd_norm_g[j])
            yp, st_p = even_mixer(hp, jnp.zeros((bp, A_SHIFT), xp.dtype), jnp.zeros((bp, H_A, HD_A, HD_A), f32),
                                  jnp.zeros((bp, CONV_W - 1, CONV_DIM), xp.dtype),
                                  jnp.zeros((bp, H_B, P_B, N_B), f32), *pe)
            ys, st_s = even_mixer(hs, state_a_shift[j], state_a_wkv[j], state_b_conv[j], state_b_ssm[j], *pe)
            even_p.append(st_p)
            even_s.append(st_s)
        else:
            po = (w_in_o[j], w_out_o[j], s5_lam_re[j], s5_lam_im[j], s5_log_dt[j], s5_b_re[j], s5_b_im[j],
                  s5_c_re[j], s5_c_im[j], s5_d[j], s5_glu_w[j], s5_glu_b[j])
            zc = jnp.zeros((bp, G_C, P_C), f32)
            yp, st_p = odd_mixer(hp, pos_p, zc, zc, dsa_prompt, *po)
            attend_s = make_dsa_sample(cache_d_k[j], cache_d_v[j], cache_d_kidx[j], page_table)
            ys, st_s = odd_mixer(hs, pos_s, state_c_re[j], state_c_im[j], attend_s, *po)
            odd_p.append(st_p)
            odd_s.append(st_s)
        xp = xp + yp
        xs = xs + ys
    y_prompt = rmsnorm(xp, final_norm_g)
    y_sample = rmsnorm(xs, final_norm_g)
    new_a_wkv_p, new_a_shift_p, new_b_ssm_p, new_b_conv_p = [jnp.stack(t) for t in zip(*even_p)]
    new_a_wkv_s, new_a_shift_s, new_b_ssm_s, new_b_conv_s = [jnp.stack(t) for t in zip(*even_s)]
    new_c_re_p, new_c_im_p, new_d_k_p, new_d_v_p, new_d_kidx_p = [jnp.stack(t) for t in zip(*odd_p)]
    new_c_re_s, new_c_im_s, new_d_k_s, new_d_v_s, new_d_kidx_s = [jnp.stack(t) for t in zip(*odd_s)]
    return (y_prompt, y_sample,
            new_a_wkv_p, new_a_shift_p, new_b_ssm_p, new_b_conv_p,
            new_c_re_p, new_c_im_p, new_d_k_p, new_d_v_p, new_d_kidx_p,
            new_a_wkv_s, new_a_shift_s, new_b_ssm_s, new_b_conv_s,
            new_c_re_s, new_c_im_s, new_d_k_s, new_d_v_s, new_d_kidx_s)
```

```python
import functools
import math

import jax
import jax.numpy as jnp
import numpy as np
from jax import lax
from jax.experimental import pallas as pl
from jax.experimental.pallas import tpu as pltpu

D_MODEL = 1024
DEPTH = 4
PAST_LEN = 2048
PAGE_SIZE = 128
NORM_EPS = 1e-6

D_A = D_MODEL
HD_A = 64
H_A = D_A // HD_A
W_LORA = 64
A_LORA = 64
A_SHIFT = 3 * D_A + W_LORA + A_LORA
RWKV_LN_EPS = 64e-5
D_B = D_MODEL
P_B = 64
H_B = D_B // P_B
N_B = 128
G_B = 4
HPG_B = H_B // G_B
CONV_W = 4
CONV_DIM = D_B + 2 * G_B * N_B
SSD_CHUNK = 128
D_C = D_MODEL // 2
CH_C = 16
G_C = D_C // CH_C
P_C = 64
H_D = 8
DH_D = 128
KVH_D = 2
QPK_D = H_D // KVH_D
D_D = H_D * DH_D
HI_D = 8
DI_D = 64
IDX_SCALE = (DI_D ** -0.5) * (HI_D ** -0.5)
TOPK_MAX = 256
QBLK = 128
ROPE_THETA = 500000.0
ROPE_FRAC = 4

IN_E = A_SHIFT + D_A + D_B + CONV_DIM + H_B
OUT_E = D_A + D_B
IN_O = 2 * D_C + D_D + 2 * KVH_D * DH_D + HI_D * DI_D + DI_D + HI_D + D_D
OUT_O = D_C + D_D

F32 = jnp.float32
BF16 = jnp.bfloat16

TILE_M = 512
TILE_N = 512


def _round_up(n, m):
    return (n + m - 1) // m * m


def _norm_matmul_kernel(x_ref, g_ref, w_ref, o_ref, h_ref):
    @pl.when(pl.program_id(1) == 0)
    def _():
        x = x_ref[...]
        ms = jnp.mean(x * x, axis=-1, keepdims=True)
        h_ref[...] = (x * lax.rsqrt(ms + NORM_EPS) * g_ref[...]).astype(BF16)

    o_ref[...] = jnp.dot(h_ref[...], w_ref[...], preferred_element_type=F32)


def norm_matmul(x2d, g, w_bf16):
    m, d = x2d.shape
    n = w_bf16.shape[1]
    tm = min(TILE_M, m)
    assert m % tm == 0 and n % TILE_N == 0
    return pl.pallas_call(
        _norm_matmul_kernel,
        grid=(m // tm, n // TILE_N),
        in_specs=[
            pl.BlockSpec((tm, d), lambda i, j: (i, 0)),
            pl.BlockSpec((1, d), lambda i, j: (0, 0)),
            pl.BlockSpec((d, TILE_N), lambda i, j: (0, j)),
        ],
        out_specs=pl.BlockSpec((tm, TILE_N), lambda i, j: (i, j)),
        out_shape=jax.ShapeDtypeStruct((m, n), F32),
        scratch_shapes=[pltpu.VMEM((tm, d), BF16)],
        compiler_params=pltpu.CompilerParams(dimension_semantics=("parallel", "arbitrary")),
        name="norm_matmul",
    )(x2d, g.reshape(1, d), w_bf16)


def _matmul_res_kernel(a_ref, w_ref, r_ref, o_ref):
    o_ref[...] = r_ref[...] + jnp.dot(a_ref[...].astype(BF16), w_ref[...], preferred_element_type=F32)


def matmul_residual(a2d, w_bf16, res2d):
    m, k = a2d.shape
    n = w_bf16.shape[1]
    tm = min(TILE_M, m)
    assert m % tm == 0
    return pl.pallas_call(
        _matmul_res_kernel,
        grid=(m // tm,),
        in_specs=[
            pl.BlockSpec((tm, k), lambda i: (i, 0)),
            pl.BlockSpec((k, n), lambda i: (0, 0)),
            pl.BlockSpec((tm, n), lambda i: (i, 0)),
        ],
        out_specs=pl.BlockSpec((tm, n), lambda i: (i, 0)),
        out_shape=jax.ShapeDtypeStruct((m, n), F32),
        compiler_params=pltpu.CompilerParams(dimension_semantics=("parallel",)),
        name="matmul_residual",
    )(a2d, w_bf16, res2d)


def _rmsnorm_kernel(x_ref, g_ref, o_ref):
    x = x_ref[...]
    ms = jnp.mean(x * x, axis=-1, keepdims=True)
    o_ref[...] = x * lax.rsqrt(ms + NORM_EPS) * g_ref[...]


def rmsnorm_rows(x2d, g):
    m, d = x2d.shape
    tm = min(TILE_M, m)
    return pl.pallas_call(
        _rmsnorm_kernel,
        grid=(m // tm,),
        in_specs=[pl.BlockSpec((tm, d), lambda i: (i, 0)), pl.BlockSpec((1, d), lambda i: (0, 0))],
        out_specs=pl.BlockSpec((tm, d), lambda i: (i, 0)),
        out_shape=jax.ShapeDtypeStruct((m, d), F32),
        compiler_params=pltpu.CompilerParams(dimension_semantics=("parallel",)),
        name="final_rmsnorm",
    )(x2d, g.reshape(1, d))


def _split(x, sizes):
    return jnp.split(x, np.cumsum(sizes)[:-1].tolist(), axis=-1)


def partial_rotary(x, pos):
    rd = x.shape[-1] // ROPE_FRAC
    half = rd // 2
    inv_freq = ROPE_THETA ** (-jnp.arange(half, dtype=jnp.float32) / half)
    ang = pos.astype(jnp.float32)[:, None] * inv_freq[None, :]
    cos = jnp.cos(ang)[:, None, :]
    sin = jnp.sin(ang)[:, None, :]
    xf = x.astype(jnp.float32)
    x1, x2 = xf[..., :half], xf[..., half:rd]
    rot = jnp.concatenate([x1 * cos - x2 * sin, x1 * sin + x2 * cos, xf[..., rd:]], axis=-1)
    return rot.astype(x.dtype)


def wkv_step(S, inp):
    r, w, k, v, kk, a = inp
    sa = jnp.einsum('bhvk,bhk->bhv', S, -kk)
    S = S * w[:, :, None, :] + sa[..., None] * (kk * a)[:, :, None, :] + v[..., None] * k[:, :, None, :]
    return S, jnp.einsum('bhvk,bhk->bhv', S, r)


def rwkv7_branch(pa, gate, shift0, wkv0, mu, w0, w2, a0, a2, k_k, k_a, r_k, ln_g, ln_b):
    f32 = jnp.float32
    b, l, _ = pa.shape
    paf = pa.astype(f32)
    prev = jnp.concatenate([shift0.astype(f32)[:, None], paf[:, :-1]], axis=1)
    xs = paf + (prev - paf) * mu.astype(f32)
    r, k, v, wl, al = _split(xs, [D_A, D_A, D_A, W_LORA, A_LORA])
    w_log = -jax.nn.softplus(-(w0.astype(f32) + jnp.tanh(wl) @ w2.astype(f32))) - 0.5
    decay = jnp.exp(-jnp.exp(w_log))
    a = jax.nn.sigmoid(a0.astype(f32) + al @ a2.astype(f32))
    kk = (k * k_k.astype(f32)).reshape(b, l, H_A, HD_A)
    kk = kk / jnp.maximum(jnp.sqrt(jnp.sum(kk * kk, axis=-1, keepdims=True)), 1e-12)
    k = k * (1.0 + (a - 1.0) * k_a.astype(f32))
    r, decay, k, v, a = [t.reshape(b, l, H_A, HD_A) for t in (r, decay, k, v, a)]
    seq = tuple(jnp.moveaxis(t, 1, 0) for t in (r, decay, k, v, kk, a))
    S_last, out = lax.scan(wkv_step, wkv0.astype(f32), seq)
    out = jnp.moveaxis(out, 0, 1)
    mean = jnp.mean(out, axis=-1, keepdims=True)
    var = jnp.mean(jnp.square(out - mean), axis=-1, keepdims=True)
    out = ((out - mean) * lax.rsqrt(var + RWKV_LN_EPS)).reshape(b, l, D_A)
    out = out * ln_g.astype(f32) + ln_b.astype(f32)
    bonus = jnp.sum(r * k * r_k.astype(f32), axis=-1, keepdims=True) * v
    out = (out + bonus.reshape(b, l, D_A)) * jax.nn.silu(gate.astype(f32))
    return out.astype(pa.dtype), S_last, pa[:, -1]


def segsum(x):
    T = x.shape[-1]
    xr = jnp.broadcast_to(x[..., None], x.shape + (T,))
    xr = jnp.where(jnp.tril(jnp.ones((T, T), bool), -1), xr, 0.0)
    cs = jnp.cumsum(xr, axis=-2)
    return jnp.where(jnp.tril(jnp.ones((T, T), bool), 0), cs, -jnp.inf)


def ssd_chunked(x, a, bm, cm, h0):
    b, l = x.shape[:2]
    cl = math.gcd(l, SSD_CHUNK)
    nc = l // cl
    x = x.reshape(b, nc, cl, G_B, HPG_B, P_B)
    a = a.reshape(b, nc, cl, G_B, HPG_B).transpose(0, 3, 4, 1, 2)
    bm = bm.reshape(b, nc, cl, G_B, N_B)
    cm = cm.reshape(b, nc, cl, G_B, N_B)
    a_cum = jnp.cumsum(a, axis=-1)
    lmat = jnp.exp(segsum(a))
    cb = jnp.einsum('bclgn,bcsgn->bcgls', cm, bm)
    y_diag = jnp.einsum('bcgls,bgrcls,bcsgrp->bclgrp', cb, lmat, x)
    decay_states = jnp.exp(a_cum[..., -1:] - a_cum)
    states = jnp.einsum('bcsgn,bgrcs,bcsgrp->bcgrpn', bm, decay_states, x)
    states = jnp.concatenate([h0.reshape(b, 1, G_B, HPG_B, P_B, N_B), states], axis=1)
    chunk_tot = jnp.pad(a_cum[..., -1], ((0, 0), (0, 0), (0, 0), (1, 0)))
    decay_chunk = jnp.exp(segsum(chunk_tot))
    new_states = jnp.einsum('bgrzc,bcgrpn->bzgrpn', decay_chunk, states)
    states, h_last = new_states[:, :-1], new_states[:, -1]
    y_off = jnp.einsum('bclgn,bcgrpn,bgrcl->bclgrp', cm, states, jnp.exp(a_cum))
    y = (y_diag + y_off).reshape(b, l, H_B, P_B)
    return y, h_last.reshape(b, H_B, P_B, N_B)


def ssd_branch(z, xbc, dt_raw, conv0, ssm0, conv_w, conv_b, dt_bias, a_log, d_skip, gnorm_g):
    f32 = jnp.float32
    b, l, _ = xbc.shape
    ext = jnp.concatenate([conv0.astype(xbc.dtype), xbc], axis=1)
    conv = conv_b.astype(f32) + sum(ext[:, i:i + l].astype(f32) * conv_w[i].astype(f32) for i in range(CONV_W))
    conv_new = ext[:, l:]
    xs, bm, cm = _split(jax.nn.silu(conv), [D_B, G_B * N_B, G_B * N_B])
    xs = xs.reshape(b, l, H_B, P_B)
    bm = bm.reshape(b, l, G_B, N_B)
    cm = cm.reshape(b, l, G_B, N_B)
    dt = jax.nn.softplus(dt_raw.astype(f32) + dt_bias.astype(f32))
    a_cont = -jnp.exp(a_log.astype(f32))
    y, ssm_new = ssd_chunked(xs * dt[..., None], dt * a_cont, bm, cm, ssm0.astype(f32))
    y = y + xs * d_skip.astype(f32)[:, None]
    y = y.reshape(b, l, D_B) * jax.nn.silu(z.astype(f32))
    yg = y.reshape(b, l, G_B, D_B // G_B)
    yg = yg * lax.rsqrt(jnp.mean(yg * yg, axis=-1, keepdims=True) + NORM_EPS)
    out = yg.reshape(b, l, D_B) * gnorm_g.astype(f32)
    return out.astype(z.dtype), ssm_new, conv_new


def even_mixer(x, g, shift0, wkv0, conv0, ssm0, w_in, w_out, mu, w0, w2, a0, a2, k_k, k_a, r_k,
               ln_g, ln_b, conv_w, conv_b, dt_bias, a_log, d_skip, gnorm_g):
    b, l, d = x.shape
    n_pad = _round_up(IN_E, TILE_N)
    w_in_p = jnp.pad(w_in, ((0, 0), (0, n_pad - IN_E))).astype(BF16)
    proj = norm_matmul(x.reshape(b * l, d), g, w_in_p)[:, :IN_E].reshape(b, l, IN_E)
    pa, ga, z, xbc, dt_raw = _split(proj, [A_SHIFT, D_A, D_B, CONV_DIM, H_B])
    out_a, wkv_new, shift_new = rwkv7_branch(pa, ga, shift0, wkv0, mu, w0, w2, a0, a2, k_k, k_a, r_k, ln_g, ln_b)
    out_b, ssm_new, conv_new = ssd_branch(z, xbc, dt_raw, conv0, ssm0, conv_w, conv_b, dt_bias, a_log, d_skip, gnorm_g)
    cat = jnp.concatenate([out_a, out_b], axis=-1).reshape(b * l, OUT_E)
    x_new = matmul_residual(cat, w_out.astype(BF16), x.reshape(b * l, d)).reshape(b, l, d)
    return x_new, (wkv_new, shift_new, ssm_new, conv_new)


def s5_combine(e1, e2):
    a1, b1 = e1
    a2, b2 = e2
    return a1 * a2, a2 * b1 + b2


def s5_branch(u, gate, h0_re, h0_im, lam_re, lam_im, log_dt, b_re, b_im, c_re, c_im, d_skip, glu_w, glu_b):
    f32 = jnp.float32
    b, l, _ = u.shape
    lam = lax.complex(lam_re.astype(f32), lam_im.astype(f32))
    delta = jnp.exp(log_dt.astype(f32))[:, None]
    lam_bar = jnp.exp(lam * delta)
    b_bar = ((lam_bar - 1.0) / lam)[..., None] * lax.complex(b_re.astype(f32), b_im.astype(f32))
    c_cplx = lax.complex(c_re.astype(f32), c_im.astype(f32))
    uf = u.astype(f32)
    bu = jnp.einsum('gpm,blgm->blgp', b_bar, uf.reshape(b, l, G_C, CH_C).astype(jnp.complex64))
    h0 = lax.complex(h0_re.astype(f32), h0_im.astype(f32))
    bu = bu.at[:, 0].add(lam_bar[None] * h0)
    _, hs = lax.associative_scan(s5_combine, (jnp.broadcast_to(lam_bar, bu.shape), bu), axis=1)
    y = jnp.real(jnp.einsum('gmp,blgp->blgm', c_cplx, hs)).reshape(b, l, D_C) + d_skip.astype(f32) * uf
    y = jax.nn.gelu(y)
    y = y * jax.nn.sigmoid(y @ glu_w.astype(f32) + glu_b.astype(f32))
    y = y * jax.nn.silu(gate.astype(f32))
    h_last = hs[:, -1]
    return y.astype(u.dtype), jnp.real(h_last), jnp.imag(h_last)


def dsa_select_attend(q, qi, wi, tpos, ki_all, gather_kv, n_sel):
    f32 = jnp.float32
    b, t = q.shape[:2]
    n_keys = ki_all.shape[1]
    dots = jnp.einsum('bthd,bsd->bths', qi.astype(f32), ki_all.astype(f32))
    score = jnp.einsum('bth,bths->bts', wi.astype(f32), jax.nn.relu(dots)) * IDX_SCALE
    admissible = jnp.arange(n_keys)[None, :] <= tpos[:, None]
    score = jnp.where(admissible[None], score, -jnp.inf)
    top_val, top_idx = lax.top_k(score, n_sel)
    valid = jnp.isfinite(top_val)
    k_sel, v_sel = gather_kv(top_idx)
    logits = jnp.einsum('btjgd,btnjd->btjgn', q.astype(f32), k_sel.astype(f32)) * (DH_D ** -0.5)
    logits = jnp.where(valid[:, :, None, None, :], logits, -jnp.inf)
    probs = jax.nn.softmax(logits, axis=-1)
    out = jnp.einsum('btjgn,btnjd->btjgd', probs, v_sel.astype(f32))
    return out.reshape(b, t, D_D)


def dsa_prompt(q, k, v, qi, ki, wi, pos):
    b, l = q.shape[:2]
    n_sel = min(TOPK_MAX, l // 4)
    qb = math.gcd(l, QBLK)
    nb = l // qb
    bidx = jnp.arange(b)[:, None, None]

    def gather_kv(idx):
        return k[bidx, idx], v[bidx, idx]

    def blk(t):
        return jnp.moveaxis(t.reshape((b, nb, qb) + t.shape[2:]), 1, 0)

    def one_block(args):
        qq, qqi, wwi, pp = args
        return dsa_select_attend(qq, qqi, wwi, pp, ki, gather_kv, n_sel)

    out = lax.map(one_block, (blk(q), blk(qi), blk(wi), pos.reshape(nb, qb)))
    return jnp.moveaxis(out, 0, 1).reshape(b, l, D_D)


def make_dsa_sample(cache_k, cache_v, cache_ki, page_table):
    past_len = page_table.shape[1] * PAGE_SIZE

    def attend(q, k, v, qi, ki, wi, pos):
        b, t = q.shape[:2]
        n_sel = min(TOPK_MAX, (past_len + t) // 4)
        ki_past = cache_ki[page_table].reshape(b, past_len, DI_D)
        ki_all = jnp.concatenate([ki_past.astype(ki.dtype), ki], axis=1)
        k_flat = cache_k.reshape(-1, KVH_D, DH_D)
        v_flat = cache_v.reshape(-1, KVH_D, DH_D)
        bidx = jnp.arange(b)[:, None, None]

        def gather_kv(idx):
            is_past = (idx < past_len)[..., None, None]
            ic = jnp.minimum(idx, past_len - 1)
            phys = page_table[bidx, ic // PAGE_SIZE] * PAGE_SIZE + ic % PAGE_SIZE
            inew = jnp.clip(idx - past_len, 0, t - 1)
            k_sel = jnp.where(is_past, k_flat[phys].astype(k.dtype), k[bidx, inew])
            v_sel = jnp.where(is_past, v_flat[phys].astype(v.dtype), v[bidx, inew])
            return k_sel, v_sel

        return dsa_select_attend(q, qi, wi, pos, ki_all, gather_kv, n_sel)

    return attend


def odd_mixer(x, g, pos, c_re0, c_im0, attend, w_in, w_out, lam_re, lam_im, log_dt, b_re, b_im,
              c_re, c_im, d_skip, glu_w, glu_b):
    b, l, d = x.shape
    n_pad = _round_up(IN_O, TILE_N)
    w_in_p = jnp.pad(w_in, ((0, 0), (0, n_pad - IN_O))).astype(BF16)
    proj = norm_matmul(x.reshape(b * l, d), g, w_in_p)[:, :IN_O].reshape(b, l, IN_O)
    u, gc, q, k, v, qi, ki, wi, gd = _split(
        proj, [D_C, D_C, D_D, KVH_D * DH_D, KVH_D * DH_D, HI_D * DI_D, DI_D, HI_D, D_D])
    out_c, re_last, im_last = s5_branch(u, gc, c_re0, c_im0, lam_re, lam_im, log_dt, b_re, b_im,
                                        c_re, c_im, d_skip, glu_w, glu_b)
    q = partial_rotary(q.reshape(b, l, H_D, DH_D), pos).reshape(b, l, KVH_D, QPK_D, DH_D)
    k = partial_rotary(k.reshape(b, l, KVH_D, DH_D), pos)
    v = v.reshape(b, l, KVH_D, DH_D)
    qi = partial_rotary(qi.reshape(b, l, HI_D, DI_D), pos)
    ki = partial_rotary(ki[:, :, None, :], pos)[:, :, 0]
    out_d = attend(q, k, v, qi, ki, wi, pos) * jax.nn.silu(gd.astype(jnp.float32))
    cat = jnp.concatenate([out_c, out_d], axis=-1).reshape(b * l, OUT_O)
    x_new = matmul_residual(cat, w_out.astype(BF16), x.reshape(b * l, d)).reshape(b, l, d)
    return x_new, (re_last, im_last, k, v, ki)


def kernel(x_prompt, x_sample, state_a_wkv, state_a_shift, state_b_ssm, state_b_conv, state_c_re, state_c_im, cache_d_k, cache_d_v, cache_d_kidx, page_table, norm_g, final_norm_g, w_in_e, w_out_e, rwkv_mu, rwkv_w0, rwkv_w2, rwkv_a0, rwkv_a2, rwkv_kk, rwkv_ka, rwkv_rk, rwkv_ln_g, rwkv_ln_b, ssd_conv_w, ssd_conv_b, ssd_dt_bias, ssd_a_log, ssd_d, ssd_norm_g, w_in_o, w_out_o, s5_lam_re, s5_lam_im, s5_log_dt, s5_b_re, s5_b_im, s5_c_re, s5_c_im, s5_d, s5_glu_w, s5_glu_b):
    f32 = jnp.float32
    bp, lp = x_prompt.shape[:2]
    bs, ls = x_sample.shape[:2]
    pos_p = jnp.arange(lp)
    pos_s = PAST_LEN + jnp.arange(ls)
    xp, xs = x_prompt, x_sample
    even_p, even_s, odd_p, odd_s = [], [], [], []
    for i in range(DEPTH):
        j = i // 2
        if i % 2 == 0:
            pe = (w_in_e[j], w_out_e[j], rwkv_mu[j], rwkv_w0[j], rwkv_w2[j], rwkv_a0[j], rwkv_a2[j],
                  rwkv_kk[j], rwkv_ka[j], rwkv_rk[j], rwkv_ln_g[j], rwkv_ln_b[j], ssd_conv_w[j],
                  ssd_conv_b[j], ssd_dt_bias[j], ssd_a_log[j], ssd_d[j], ssd_norm_g[j])
            xp, st_p = even_mixer(xp, norm_g[i], jnp.zeros((bp, A_SHIFT), f32), jnp.zeros((bp, H_A, HD_A, HD_A), f32),
                                  jnp.zeros((bp, CONV_W - 1, CONV_DIM), f32),
                                  jnp.zeros((bp, H_B, P_B, N_B), f32), *pe)
            xs, st_s = even_mixer(xs, norm_g[i], state_a_shift[j], state_a_wkv[j], state_b_conv[j], state_b_ssm[j], *pe)
            even_p.append(st_p)
            even_s.append(st_s)
        else:
            po = (w_in_o[j], w_out_o[j], s5_lam_re[j], s5_lam_im[j], s5_log_dt[j], s5_b_re[j], s5_b_im[j],
                  s5_c_re[j], s5_c_im[j], s5_d[j], s5_glu_w[j], s5_glu_b[j])
            zc = jnp.zeros((bp, G_C, P_C), f32)
            xp, st_p = odd_mixer(xp, norm_g[i], pos_p, zc, zc, dsa_prompt, *po)
            attend_s = make_dsa_sample(cache_d_k[j], cache_d_v[j], cache_d_kidx[j], page_table)
            xs, st_s = odd_mixer(xs, norm_g[i], pos_s, state_c_re[j], state_c_im[j], attend_s, *po)
            odd_p.append(st_p)
            odd_s.append(st_s)
    y_prompt = rmsnorm_rows(xp.reshape(bp * lp, D_MODEL), final_norm_g).reshape(bp, lp, D_MODEL)
    y_sample = rmsnorm_rows(xs.reshape(bs * ls, D_MODEL), final_norm_g).reshape(bs, ls, D_MODEL)
    new_a_wkv_p, new_a_shift_p, new_b_ssm_p, new_b_conv_p = [jnp.stack(t) for t in zip(*even_p)]
    new_a_wkv_s, new_a_shift_s, new_b_ssm_s, new_b_conv_s = [jnp.stack(t) for t in zip(*even_s)]
    new_c_re_p, new_c_im_p, new_d_k_p, new_d_v_p, new_d_kidx_p = [jnp.stack(t) for t in zip(*odd_p)]
    new_c_re_s, new_c_im_s, new_d_k_s, new_d_v_s, new_d_kidx_s = [jnp.stack(t) for t in zip(*odd_s)]
    return (y_prompt, y_sample,
            new_a_wkv_p, new_a_shift_p, new_b_ssm_p, new_b_conv_p,
            new_c_re_p, new_c_im_p, new_d_k_p, new_d_v_p, new_d_kidx_p,
            new_a_wkv_s, new_a_shift_s, new_b_ssm_s, new_b_conv_s,
            new_c_re_s, new_c_im_s, new_d_k_s, new_d_v_s, new_d_kidx_s)
```

```python
import functools
import math

import jax
import jax.numpy as jnp
import numpy as np
from jax import lax
from jax.experimental import pallas as pl
from jax.experimental.pallas import tpu as pltpu

D_MODEL = 1024
DEPTH = 4
PAST_LEN = 2048
PAGE_SIZE = 128
NORM_EPS = 1e-6

D_A = D_MODEL
HD_A = 64
H_A = D_A // HD_A
W_LORA = 64
A_LORA = 64
A_SHIFT = 3 * D_A + W_LORA + A_LORA
RWKV_LN_EPS = 64e-5
D_B = D_MODEL
P_B = 64
H_B = D_B // P_B
N_B = 128
G_B = 4
HPG_B = H_B // G_B
CONV_W = 4
CONV_DIM = D_B + 2 * G_B * N_B
SSD_CHUNK = 128
D_C = D_MODEL // 2
CH_C = 16
G_C = D_C // CH_C
P_C = 64
H_D = 8
DH_D = 128
KVH_D = 2
QPK_D = H_D // KVH_D
D_D = H_D * DH_D
HI_D = 8
DI_D = 64
IDX_SCALE = (DI_D ** -0.5) * (HI_D ** -0.5)
TOPK_MAX = 256
QBLK = 128
ROPE_THETA = 500000.0
ROPE_FRAC = 4

IN_E = A_SHIFT + D_A + D_B + CONV_DIM + H_B
OUT_E = D_A + D_B
IN_O = 2 * D_C + D_D + 2 * KVH_D * DH_D + HI_D * DI_D + DI_D + HI_D + D_D
OUT_O = D_C + D_D

F32 = jnp.float32
BF16 = jnp.bfloat16

TILE_M = 512
TILE_N = 512


def _round_up(n, m):
    return (n + m - 1) // m * m


def _norm_matmul_kernel(x_ref, g_ref, w_ref, o_ref, h_ref):
    @pl.when(pl.program_id(1) == 0)
    def _():
        x = x_ref[...]
        ms = jnp.mean(x * x, axis=-1, keepdims=True)
        h_ref[...] = (x * lax.rsqrt(ms + NORM_EPS) * g_ref[...]).astype(BF16)

    o_ref[...] = jnp.dot(h_ref[...], w_ref[...], preferred_element_type=F32)


def norm_matmul(x2d, g, w_bf16, tn=TILE_N):
    m, d = x2d.shape
    n = w_bf16.shape[1]
    tm = min(TILE_M, m)
    assert m % tm == 0 and n % tn == 0
    return pl.pallas_call(
        _norm_matmul_kernel,
        grid=(m // tm, n // tn),
        in_specs=[
            pl.BlockSpec((tm, d), lambda i, j: (i, 0)),
            pl.BlockSpec((1, d), lambda i, j: (0, 0)),
            pl.BlockSpec((d, tn), lambda i, j: (0, j)),
        ],
        out_specs=pl.BlockSpec((tm, tn), lambda i, j: (i, j)),
        out_shape=jax.ShapeDtypeStruct((m, n), F32),
        scratch_shapes=[pltpu.VMEM((tm, d), BF16)],
        compiler_params=pltpu.CompilerParams(dimension_semantics=("parallel", "arbitrary")),
        name="norm_matmul",
    )(x2d, g.reshape(1, d), w_bf16)


def _matmul_res_kernel(a_ref, w_ref, r_ref, o_ref):
    o_ref[...] = r_ref[...] + jnp.dot(a_ref[...].astype(BF16), w_ref[...], preferred_element_type=F32)


def matmul_residual(a2d, w_bf16, res2d):
    m, k = a2d.shape
    n = w_bf16.shape[1]
    tm = min(TILE_M, m)
    assert m % tm == 0
    return pl.pallas_call(
        _matmul_res_kernel,
        grid=(m // tm,),
        in_specs=[
            pl.BlockSpec((tm, k), lambda i: (i, 0)),
            pl.BlockSpec((k, n), lambda i: (0, 0)),
            pl.BlockSpec((tm, n), lambda i: (i, 0)),
        ],
        out_specs=pl.BlockSpec((tm, n), lambda i: (i, 0)),
        out_shape=jax.ShapeDtypeStruct((m, n), F32),
        compiler_params=pltpu.CompilerParams(dimension_semantics=("parallel",)),
        name="matmul_residual",
    )(a2d, w_bf16, res2d)


def _rmsnorm_kernel(x_ref, g_ref, o_ref):
    x = x_ref[...]
    ms = jnp.mean(x * x, axis=-1, keepdims=True)
    o_ref[...] = x * lax.rsqrt(ms + NORM_EPS) * g_ref[...]


def rmsnorm_rows(x2d, g):
    m, d = x2d.shape
    tm = min(TILE_M, m)
    return pl.pallas_call(
        _rmsnorm_kernel,
        grid=(m // tm,),
        in_specs=[pl.BlockSpec((tm, d), lambda i: (i, 0)), pl.BlockSpec((1, d), lambda i: (0, 0))],
        out_specs=pl.BlockSpec((tm, d), lambda i: (i, 0)),
        out_shape=jax.ShapeDtypeStruct((m, d), F32),
        compiler_params=pltpu.CompilerParams(dimension_semantics=("parallel",)),
        name="final_rmsnorm",
    )(x2d, g.reshape(1, d))


def _mm(a, b):
    return jnp.dot(a.astype(BF16), b.astype(BF16), preferred_element_type=F32)


def _mm_nt(a, b):
    return lax.dot_general(a.astype(BF16), b.astype(BF16), (((1,), (1,)), ((), ())), preferred_element_type=F32)


def _mm_tn(a, b):
    return lax.dot_general(a.astype(BF16), b.astype(BF16), (((0,), (0,)), ((), ())), preferred_element_type=F32)


def _hi_lo(x):
    hi = x.astype(BF16)
    lo = (x - hi.astype(F32)).astype(BF16)
    return hi, lo


def _mm_exact_lhs(a_bf16, x):
    hi, lo = _hi_lo(x)
    return (jnp.dot(a_bf16, hi, preferred_element_type=F32) + jnp.dot(a_bf16, lo, preferred_element_type=F32))


def _mm_exact_rhs(x, b_bf16):
    hi, lo = _hi_lo(x)
    return (jnp.dot(hi, b_bf16, preferred_element_type=F32) + jnp.dot(lo, b_bf16, preferred_element_type=F32))


def _softplus(x):
    return jnp.maximum(x, 0.0) + jnp.log1p(jnp.exp(-jnp.abs(x)))


def _sigmoid(x):
    return 1.0 / (1.0 + jnp.exp(-x))


def _unit_lower_inverses(a_list, row, col, n):
    eye = jnp.where(row == col, 1.0, 0.0).astype(F32)
    ts = [eye for _ in a_list]
    m = 1
    while m < n:
        in_pair = (row // (2 * m)) == (col // (2 * m))
        lvl = in_pair & ((row % (2 * m)) >= m) & ((col % (2 * m)) < m)
        ls = [jnp.where(lvl, a, 0.0) for a in a_list]
        if m == 1:
            ts = [t - l for t, l in zip(ts, ls)]
        else:
            tl = [_mm(t, l) for t, l in zip(ts, ls)]
            ts = [t - _mm(x, t) for t, x in zip(ts, tl)]
        m *= 2
    return ts


EV_GA, EV_Z, EV_XBC, EV_R, EV_K, EV_V, EV_WA, EV_DT = 0, 1024, 2048, 4096, 5120, 6144, 7168, 7296
EV_N = 7424
RWKV_CHUNK = 64


def _rwkv_kernel(r_ref, k_ref, v_ref, wa_ref, g_ref, sh0_ref, s0_ref, mu_ref, pv_ref, w2_ref, a2_ref, bones_ref,
                 o_ref, snew_ref, shnew_ref, st_ref, carry_ref, *, chunk, n_valid):
    C = chunk
    c = pl.program_id(1)

    @pl.when(c == 0)
    def _():
        carry_ref[...] = sh0_ref[0]
        st_ref[...] = s0_ref[0]

    mu = mu_ref[...]
    pv = pv_ref[...]
    w0, a0, k_k, k_a, r_k, ln_g, ln_b = (pv[i:i + 1] for i in range(7))
    carry = carry_ref[...]
    row1 = lax.broadcasted_iota(jnp.int32, (C, 1), 0)

    def tok_shift(x, lo, hi):
        prev = jnp.where(row1 == 0, carry[:, lo:hi], pltpu.roll(x, 1, 0))
        return x + (prev - x) * mu[:, lo:hi]

    pr, pk, pvv, pwa = r_ref[0], k_ref[0], v_ref[0], wa_ref[0]
    r = tok_shift(pr, 0, D_A)
    k = tok_shift(pk, D_A, 2 * D_A)
    v = tok_shift(pvv, 2 * D_A, 3 * D_A)
    wa = tok_shift(pwa, 3 * D_A, A_SHIFT)
    last = n_valid - 1
    new_carry = jnp.concatenate([pr[last:last + 1], pk[last:last + 1], pvv[last:last + 1], pwa[last:last + 1]], axis=1)
    carry_ref[...] = new_carry
    shnew_ref[0] = new_carry

    wl, al = wa[:, :W_LORA], wa[:, W_LORA:]
    w_log = -_softplus(-(w0 + _mm(jnp.tanh(wl), w2_ref[...]))) - 0.5
    logw = -jnp.exp(w_log)
    a = _sigmoid(a0 + _mm(al, a2_ref[...]))
    bones = bones_ref[...]
    kk = k * k_k
    kk = kk / jnp.maximum(jnp.sqrt(_mm_exact_rhs(kk * kk, bones)), 1e-12)
    kp = k * (1.0 + (a - 1.0) * k_a)
    bonus = _mm_exact_rhs(r * kp * r_k, bones) * v
    if n_valid < C:
        ok = row1 < n_valid
        logw = jnp.where(ok, logw, 0.0)
        kk = jnp.where(ok, kk, 0.0)
        kp = jnp.where(ok, kp, 0.0)

    row = lax.broadcasted_iota(jnp.int32, (C, C), 0)
    col = lax.broadcasted_iota(jnp.int32, (C, C), 1)
    incl = col <= row
    strict = col < row
    cum = _mm_exact_lhs(incl.astype(BF16), logw)
    cum_last = cum[C - 1:C]
    eneg = jnp.exp(-cum)
    alpha = kk * jnp.exp(cum - logw)
    ka = kk * a
    beta = ka * eneg
    kappa = kp * eneg
    rho = r * jnp.exp(cum)
    dec_end = jnp.exp(cum_last - cum)
    kappa_e = kp * dec_end
    beta_e = ka * dec_end
    wc = jnp.exp(cum_last)

    heads = range(H_A)
    sls = [slice(h * HD_A, (h + 1) * HD_A) for h in heads]
    s0s = [st_ref[h] for h in heads]
    bks = [jnp.concatenate([beta[:, sl], kappa[:, sl]], axis=0) for sl in sls]
    g_as = [_mm_nt(alpha[:, sl], bk) for sl, bk in zip(sls, bks)]
    g_rs = [_mm_nt(rho[:, sl], bk) for sl, bk in zip(sls, bks)]
    t_invs = _unit_lower_inverses([jnp.where(strict, g[:, :C], 0.0) for g in g_as], row, col, C)
    rhss = [_mm_nt(alpha[:, sl], s0) + _mm(jnp.where(strict, g[:, C:], 0.0), v[:, sl])
            for sl, s0, g in zip(sls, s0s, g_as)]
    us = [_mm(t, x) for t, x in zip(t_invs, rhss)]
    outs = [_mm_nt(rho[:, sl], s0) + _mm(jnp.where(incl, g[:, C:], 0.0), v[:, sl])
            - _mm(jnp.where(incl, g[:, :C], 0.0), u) for sl, s0, g, u in zip(sls, s0s, g_rs, us)]
    for h, sl, s0, u in zip(heads, sls, s0s, us):
        x = jnp.concatenate([v[:, sl], -u], axis=0)
        y = jnp.concatenate([kappa_e[:, sl], beta_e[:, sl]], axis=0)
        st_ref[h] = s0 * wc[:, sl] + _mm_tn(x, y)
    out = jnp.concatenate(outs, axis=1)

    mean = _mm_exact_rhs(out, bones) * (1.0 / HD_A)
    d = out - mean
    var = _mm_exact_rhs(d * d, bones) * (1.0 / HD_A)
    y = d * lax.rsqrt(var + RWKV_LN_EPS) * ln_g + ln_b
    gate = g_ref[0]
    o_ref[0] = (y + bonus) * (gate * _sigmoid(gate))
    snew_ref[0] = st_ref[...]


def rwkv7_pallas(proj3d, shift0, wkv0, mu, w0, w2, a0, a2, k_k, k_a, r_k, ln_g, ln_b, n_valid_tokens):
    b, l, _ = proj3d.shape
    chunk = min(RWKV_CHUNK, l)
    n_chunks = l // chunk
    assert l % chunk == 0 and chunk % 8 == 0
    n_valid = n_valid_tokens - (n_chunks - 1) * chunk
    assert 0 < n_valid <= chunk and (n_valid == chunk or n_chunks == 1)
    pvec = jnp.stack([w0, a0, k_k, k_a, r_k.reshape(D_A), ln_g, ln_b, jnp.zeros_like(w0)])
    hid = jnp.arange(D_A) // HD_A
    bones = (hid[:, None] == hid[None, :]).astype(BF16)
    blk = lambda w, j: pl.BlockSpec((1, chunk, w), lambda bi, ci: (bi, ci, j))
    const = lambda shape: pl.BlockSpec(shape, lambda bi, ci: (0,) * len(shape))
    out, wkv_new, shift_new = pl.pallas_call(
        functools.partial(_rwkv_kernel, chunk=chunk, n_valid=n_valid),
        grid=(b, n_chunks),
        in_specs=[
            blk(D_A, EV_R // D_A), blk(D_A, EV_K // D_A), blk(D_A, EV_V // D_A), blk(128, EV_WA // 128),
            blk(D_A, EV_GA // D_A),
            pl.BlockSpec((1, 1, A_SHIFT), lambda bi, ci: (bi, 0, 0)),
            pl.BlockSpec((1, H_A, HD_A, HD_A), lambda bi, ci: (bi, 0, 0, 0)),
            const((1, A_SHIFT)), const((8, D_A)), const((W_LORA, D_A)), const((A_LORA, D_A)), const((D_A, D_A)),
        ],
        out_specs=[
            pl.BlockSpec((1, chunk, D_A), lambda bi, ci: (bi, ci, 0)),
            pl.BlockSpec((1, H_A, HD_A, HD_A), lambda bi, ci: (bi, 0, 0, 0)),
            pl.BlockSpec((1, 1, A_SHIFT), lambda bi, ci: (bi, 0, 0)),
        ],
        out_shape=[
            jax.ShapeDtypeStruct((b, l, D_A), F32),
            jax.ShapeDtypeStruct((b, H_A, HD_A, HD_A), F32),
            jax.ShapeDtypeStruct((b, 1, A_SHIFT), F32),
        ],
        scratch_shapes=[pltpu.VMEM((H_A, HD_A, HD_A), F32), pltpu.VMEM((1, A_SHIFT), F32)],
        compiler_params=pltpu.CompilerParams(dimension_semantics=("parallel", "arbitrary")),
        name="rwkv7_chunked",
    )(proj3d, proj3d, proj3d, proj3d, proj3d, shift0.reshape(b, 1, A_SHIFT), wkv0,
      mu.reshape(1, A_SHIFT), pvec, w2.astype(BF16), a2.astype(BF16), bones)
    return out, wkv_new, shift_new.reshape(b, A_SHIFT)


def rwkv7_pallas_from_parts(pa, ga, shift0, wkv0, mu, w0, w2, a0, a2, k_k, k_a, r_k, ln_g, ln_b):
    b, l, _ = pa.shape
    lp = _round_up(l, 8)
    proj = jnp.zeros((b, lp, EV_N), F32)
    proj = proj.at[:, :l, EV_GA:EV_GA + D_A].set(ga)
    proj = proj.at[:, :l, EV_R:EV_R + 3 * D_A].set(pa[..., :3 * D_A])
    proj = proj.at[:, :l, EV_WA:EV_WA + 128].set(pa[..., 3 * D_A:])
    out, wkv_new, shift_new = rwkv7_pallas(proj, shift0, wkv0, mu, w0, w2, a0, a2, k_k, k_a, r_k, ln_g, ln_b, l)
    return out[:, :l], wkv_new, shift_new


OD_U, OD_GC, OD_Q, OD_GD, OD_K, OD_V, OD_QI, OD_KIWI = 0, 512, 1024, 2048, 3072, 3328, 3584, 4096
OD_N = 4224
KV_W = KVH_D * DH_D
QI_W = HI_D * DI_D
INT_MIN = -2 ** 31


def _rope_tables(pos, head_dim):
    rd = head_dim // ROPE_FRAC
    half = rd // 2
    inv_freq = ROPE_THETA ** (-jnp.arange(half, dtype=F32) / half)
    ang = pos.astype(F32)[:, None] * inv_freq[None, :]
    cos, sin = jnp.cos(ang), jnp.sin(ang)
    t = pos.shape[0]
    ones = jnp.ones((t, head_dim - rd), F32)
    cos_h = jnp.concatenate([cos, cos, ones], axis=1)
    sin_h = jnp.concatenate([-sin, sin, 0.0 * ones], axis=1)
    reps = 128 // head_dim
    return jnp.tile(cos_h, (1, reps)), jnp.tile(sin_h, (1, reps))


def _rotate(x, cos_t, sin_t, head_dim):
    w = x.shape[1]
    half = head_dim // ROPE_FRAC // 2
    lane = lax.broadcasted_iota(jnp.int32, x.shape, 1) % head_dim
    if w >= 128:
        cos_f = jnp.tile(cos_t, (1, w // 128))
        sin_f = jnp.tile(sin_t, (1, w // 128))
        partner = jnp.where(lane < half, pltpu.roll(x, w - half, 1), pltpu.roll(x, half, 1))
    else:
        cos_f, sin_f = cos_t[:, :w], sin_t[:, :w]
        partner = jnp.where(lane < half, jnp.concatenate([x[:, half:], x[:, :half]], axis=1),
                            jnp.concatenate([x[:, w - half:], x[:, :w - half]], axis=1))
    return x * cos_f + partner * sin_f


def _rope_kernel(q_ref, k_ref, v_ref, qi_ref, kiwi_ref, c128_ref, s128_ref, c64_ref, s64_ref,
                 qo_ref, qio_ref, ko_ref, vo_ref, kio_ref, kb_ref, vb_ref, kib_ref):
    c128, s128, c64, s64 = c128_ref[...], s128_ref[...], c64_ref[...], s64_ref[...]
    qo_ref[0] = _rotate(q_ref[0], c128, s128, DH_D).astype(BF16)
    qio_ref[0] = _rotate(qi_ref[0], c64, s64, DI_D).astype(BF16)
    k_rot = _rotate(k_ref[0], c128, s128, DH_D)
    ko_ref[0] = k_rot
    kb_ref[0] = k_rot.astype(BF16)
    v = v_ref[0]
    vo_ref[0] = v
    vb_ref[0] = v.astype(BF16)
    ki_rot = _rotate(kiwi_ref[0][:, :DI_D], c64, s64, DI_D)
    kio_ref[0] = ki_rot
    kib_ref[0] = ki_rot.astype(BF16)


def rope_pallas(proj3d, pos):
    b, l, _ = proj3d.shape
    tr = min(512, l)
    assert l % tr == 0
    c128, s128 = _rope_tables(pos, DH_D)
    c64, s64 = _rope_tables(pos, DI_D)
    blk = lambda w, j: pl.BlockSpec((1, tr, w), lambda bi, ti: (bi, ti, j))
    tab = pl.BlockSpec((tr, 128), lambda bi, ti: (ti, 0))
    oblk = lambda w: pl.BlockSpec((1, tr, w), lambda bi, ti: (bi, ti, 0))
    shp = lambda w, dt: jax.ShapeDtypeStruct((b, l, w), dt)
    return pl.pallas_call(
        _rope_kernel,
        grid=(b, l // tr),
        in_specs=[blk(D_D, OD_Q // D_D), blk(KV_W, OD_K // KV_W), blk(KV_W, OD_V // KV_W), blk(QI_W, OD_QI // QI_W),
                  blk(128, OD_KIWI // 128), tab, tab, tab, tab],
        out_specs=[oblk(D_D), oblk(QI_W), oblk(KV_W), oblk(KV_W), oblk(DI_D), oblk(KV_W), oblk(KV_W), oblk(DI_D)],
        out_shape=[shp(D_D, BF16), shp(QI_W, BF16), shp(KV_W, F32), shp(KV_W, F32), shp(DI_D, F32),
                   shp(KV_W, BF16), shp(KV_W, BF16), shp(DI_D, BF16)],
        compiler_params=pltpu.CompilerParams(dimension_semantics=("parallel", "parallel")),
        name="dsa_rope",
    )(proj3d, proj3d, proj3d, proj3d, proj3d, c128, s128, c64, s64)


DSA_TQ = 128
DSA_TK = 512
NEG_BIG = -1e30


def _sortable_key(s):
    bits = pltpu.bitcast(s + 0.0, jnp.int32)
    return bits ^ ((bits >> 31) & 0x7FFFFFFF)


def _dsa_prompt_kernel(q_ref, qi_ref, kiwi_ref, gd_ref, k_ref, v_ref, ki_ref, o_ref, key_ref, *, n_sel):
    tq, tk = DSA_TQ, DSA_TK
    t0 = pl.program_id(1) * tq
    n_kt = (t0 + tq + tk - 1) // tk
    qi = qi_ref[0]
    wi = kiwi_ref[0][:, DI_D:DI_D + HI_D]
    pos_q = t0 + lax.broadcasted_iota(jnp.int32, (tq, 1), 0)
    lane_k = lax.broadcasted_iota(jnp.int32, (tq, tk), 1)

    def key_slice(kt):
        return pl.ds(pl.multiple_of(kt * tk, tk), tk)

    def fold_lanes(x):
        return sum(x[:, i * 128:(i + 1) * 128] for i in range(tk // 128))

    def score_body(kt, carry):
        ks = key_slice(kt)
        kit = ki_ref[0, ks, :]
        acc = jnp.zeros((tq, tk), F32)
        for h in range(HI_D):
            d = lax.dot_general(qi[:, h * DI_D:(h + 1) * DI_D], kit, (((1,), (1,)), ((), ())),
                                preferred_element_type=F32)
            acc = acc + wi[:, h:h + 1] * jnp.maximum(d, 0.0)
        key = _sortable_key(acc * IDX_SCALE)
        key_ref[:, ks] = jnp.where(kt * tk + lane_k <= pos_q, key, INT_MIN)
        return carry

    lax.fori_loop(0, n_kt, score_body, 0)

    def count(pred_fn):
        def body(kt, c):
            return c + fold_lanes(jnp.where(pred_fn(key_ref[:, key_slice(kt)]), 1.0, 0.0))
        c = lax.fori_loop(0, n_kt, body, jnp.zeros((tq, 128), F32))
        return jnp.sum(c, axis=1, keepdims=True)

    def bit_body(i, ans):
        cand = ans | lax.shift_left(jnp.int32(1), 31 - i)
        scand = cand ^ INT_MIN
        cnt = count(lambda key: key >= scand)
        return jnp.where(cnt >= n_sel, cand, ans)

    thr = lax.fori_loop(0, 32, bit_body, jnp.zeros((tq, 1), jnp.int32)) ^ INT_MIN
    need = n_sel - count(lambda key: key > thr)

    tri = (lax.broadcasted_iota(jnp.int32, (tk, tk), 0) <= lax.broadcasted_iota(jnp.int32, (tk, tk), 1)).astype(BF16)

    def sel_body(kt, tie_seen):
        ks = key_slice(kt)
        key = key_ref[:, ks]
        tie = (key == thr) & (key != INT_MIN)
        tie_f = jnp.where(tie, 1.0, 0.0)
        rank = tie_seen + jnp.dot(tie_f.astype(BF16), tri, preferred_element_type=F32)
        sel = (key > thr) | (tie & (rank <= need))
        key_ref[:, ks] = jnp.where(sel, 1, 0)
        return tie_seen + jnp.sum(tie_f, axis=1, keepdims=True)

    lax.fori_loop(0, n_kt, sel_body, jnp.zeros((tq, 1), F32))

    gd = gd_ref[0]
    outs = []
    for j in range(KVH_D):
        qs = jnp.concatenate([q_ref[0, :, (j * QPK_D + g) * DH_D:(j * QPK_D + g + 1) * DH_D] for g in range(QPK_D)],
                             axis=0)

        def att_body(kt, carry):
            m, l, acc = carry
            ks = key_slice(kt)
            kt_j = k_ref[0, ks, j * DH_D:(j + 1) * DH_D]
            vt_j = v_ref[0, ks, j * DH_D:(j + 1) * DH_D]
            sel = key_ref[:, ks] != 0
            sel4 = jnp.concatenate([sel] * QPK_D, axis=0)
            logits = lax.dot_general(qs, kt_j, (((1,), (1,)), ((), ())), preferred_element_type=F32) * (DH_D ** -0.5)
            logits = jnp.where(sel4, logits, NEG_BIG)
            m_new = jnp.maximum(m, jnp.max(logits, axis=1, keepdims=True))
            p = jnp.where(sel4, jnp.exp(logits - m_new), 0.0)
            scale = jnp.exp(m - m_new)
            l_new = l * scale + jnp.sum(p, axis=1, keepdims=True)
            acc_new = acc * scale + jnp.dot(p.astype(BF16), vt_j, preferred_element_type=F32)
            return m_new, l_new, acc_new

        m0 = jnp.full((QPK_D * tq, 1), NEG_BIG, F32)
        l0 = jnp.zeros((QPK_D * tq, 1), F32)
        a0 = jnp.zeros((QPK_D * tq, DH_D), F32)
        _, l_f, acc_f = lax.fori_loop(0, n_kt, att_body, (m0, l0, a0))
        o = acc_f / l_f
        outs.extend(o[g * tq:(g + 1) * tq] for g in range(QPK_D))
    out = jnp.concatenate(outs, axis=1)
    o_ref[0] = out * (gd * _sigmoid(gd))


def dsa_prompt_pallas(proj3d, q_b, qi_b, k_b, v_b, ki_b):
    b, l, _ = proj3d.shape
    n_sel = min(TOPK_MAX, l // 4)
    assert l % DSA_TQ == 0 and l % DSA_TK == 0
    qblk = lambda w, j: pl.BlockSpec((1, DSA_TQ, w), lambda bi, ti: (bi, ti, j))
    full = lambda w: pl.BlockSpec((1, l, w), lambda bi, ti: (bi, 0, 0))
    return pl.pallas_call(
        functools.partial(_dsa_prompt_kernel, n_sel=float(n_sel)),
        grid=(b, l // DSA_TQ),
        in_specs=[qblk(D_D, 0), qblk(QI_W, 0), qblk(128, OD_KIWI // 128), qblk(D_D, OD_GD // D_D),
                  full(KV_W), full(KV_W), full(DI_D)],
        out_specs=pl.BlockSpec((1, DSA_TQ, D_D), lambda bi, ti: (bi, ti, 0)),
        out_shape=jax.ShapeDtypeStruct((b, l, D_D), F32),
        scratch_shapes=[pltpu.VMEM((DSA_TQ, l), jnp.int32)],
        compiler_params=pltpu.CompilerParams(dimension_semantics=("parallel", "arbitrary")),
        name="dsa_prompt",
    )(q_b, qi_b, proj3d, proj3d, k_b, v_b, ki_b)


def _split(x, sizes):
    return jnp.split(x, np.cumsum(sizes)[:-1].tolist(), axis=-1)


def partial_rotary(x, pos):
    rd = x.shape[-1] // ROPE_FRAC
    half = rd // 2
    inv_freq = ROPE_THETA ** (-jnp.arange(half, dtype=jnp.float32) / half)
    ang = pos.astype(jnp.float32)[:, None] * inv_freq[None, :]
    cos = jnp.cos(ang)[:, None, :]
    sin = jnp.sin(ang)[:, None, :]
    xf = x.astype(jnp.float32)
    x1, x2 = xf[..., :half], xf[..., half:rd]
    rot = jnp.concatenate([x1 * cos - x2 * sin, x1 * sin + x2 * cos, xf[..., rd:]], axis=-1)
    return rot.astype(x.dtype)


def wkv_step(S, inp):
    r, w, k, v, kk, a = inp
    sa = jnp.einsum('bhvk,bhk->bhv', S, -kk)
    S = S * w[:, :, None, :] + sa[..., None] * (kk * a)[:, :, None, :] + v[..., None] * k[:, :, None, :]
    return S, jnp.einsum('bhvk,bhk->bhv', S, r)


def rwkv7_branch(pa, gate, shift0, wkv0, mu, w0, w2, a0, a2, k_k, k_a, r_k, ln_g, ln_b):
    f32 = jnp.float32
    b, l, _ = pa.shape
    paf = pa.astype(f32)
    prev = jnp.concatenate([shift0.astype(f32)[:, None], paf[:, :-1]], axis=1)
    xs = paf + (prev - paf) * mu.astype(f32)
    r, k, v, wl, al = _split(xs, [D_A, D_A, D_A, W_LORA, A_LORA])
    w_log = -jax.nn.softplus(-(w0.astype(f32) + jnp.tanh(wl) @ w2.astype(f32))) - 0.5
    decay = jnp.exp(-jnp.exp(w_log))
    a = jax.nn.sigmoid(a0.astype(f32) + al @ a2.astype(f32))
    kk = (k * k_k.astype(f32)).reshape(b, l, H_A, HD_A)
    kk = kk / jnp.maximum(jnp.sqrt(jnp.sum(kk * kk, axis=-1, keepdims=True)), 1e-12)
    k = k * (1.0 + (a - 1.0) * k_a.astype(f32))
    r, decay, k, v, a = [t.reshape(b, l, H_A, HD_A) for t in (r, decay, k, v, a)]
    seq = tuple(jnp.moveaxis(t, 1, 0) for t in (r, decay, k, v, kk, a))
    S_last, out = lax.scan(wkv_step, wkv0.astype(f32), seq)
    out = jnp.moveaxis(out, 0, 1)
    mean = jnp.mean(out, axis=-1, keepdims=True)
    var = jnp.mean(jnp.square(out - mean), axis=-1, keepdims=True)
    out = ((out - mean) * lax.rsqrt(var + RWKV_LN_EPS)).reshape(b, l, D_A)
    out = out * ln_g.astype(f32) + ln_b.astype(f32)
    bonus = jnp.sum(r * k * r_k.astype(f32), axis=-1, keepdims=True) * v
    out = (out + bonus.reshape(b, l, D_A)) * jax.nn.silu(gate.astype(f32))
    return out.astype(pa.dtype), S_last, pa[:, -1]


def segsum(x):
    T = x.shape[-1]
    xr = jnp.broadcast_to(x[..., None], x.shape + (T,))
    xr = jnp.where(jnp.tril(jnp.ones((T, T), bool), -1), xr, 0.0)
    cs = jnp.cumsum(xr, axis=-2)
    return jnp.where(jnp.tril(jnp.ones((T, T), bool), 0), cs, -jnp.inf)


def ssd_chunked(x, a, bm, cm, h0):
    b, l = x.shape[:2]
    cl = math.gcd(l, SSD_CHUNK)
    nc = l // cl
    x = x.reshape(b, nc, cl, G_B, HPG_B, P_B)
    a = a.reshape(b, nc, cl, G_B, HPG_B).transpose(0, 3, 4, 1, 2)
    bm = bm.reshape(b, nc, cl, G_B, N_B)
    cm = cm.reshape(b, nc, cl, G_B, N_B)
    a_cum = jnp.cumsum(a, axis=-1)
    lmat = jnp.exp(segsum(a))
    cb = jnp.einsum('bclgn,bcsgn->bcgls', cm, bm)
    y_diag = jnp.einsum('bcgls,bgrcls,bcsgrp->bclgrp', cb, lmat, x)
    decay_states = jnp.exp(a_cum[..., -1:] - a_cum)
    states = jnp.einsum('bcsgn,bgrcs,bcsgrp->bcgrpn', bm, decay_states, x)
    states = jnp.concatenate([h0.reshape(b, 1, G_B, HPG_B, P_B, N_B), states], axis=1)
    chunk_tot = jnp.pad(a_cum[..., -1], ((0, 0), (0, 0), (0, 0), (1, 0)))
    decay_chunk = jnp.exp(segsum(chunk_tot))
    new_states = jnp.einsum('bgrzc,bcgrpn->bzgrpn', decay_chunk, states)
    states, h_last = new_states[:, :-1], new_states[:, -1]
    y_off = jnp.einsum('bclgn,bcgrpn,bgrcl->bclgrp', cm, states, jnp.exp(a_cum))
    y = (y_diag + y_off).reshape(b, l, H_B, P_B)
    return y, h_last.reshape(b, H_B, P_B, N_B)


def ssd_branch(z, xbc, dt_raw, conv0, ssm0, conv_w, conv_b, dt_bias, a_log, d_skip, gnorm_g):
    f32 = jnp.float32
    b, l, _ = xbc.shape
    ext = jnp.concatenate([conv0.astype(xbc.dtype), xbc], axis=1)
    conv = conv_b.astype(f32) + sum(ext[:, i:i + l].astype(f32) * conv_w[i].astype(f32) for i in range(CONV_W))
    conv_new = ext[:, l:]
    xs, bm, cm = _split(jax.nn.silu(conv), [D_B, G_B * N_B, G_B * N_B])
    xs = xs.reshape(b, l, H_B, P_B)
    bm = bm.reshape(b, l, G_B, N_B)
    cm = cm.reshape(b, l, G_B, N_B)
    dt = jax.nn.softplus(dt_raw.astype(f32) + dt_bias.astype(f32))
    a_cont = -jnp.exp(a_log.astype(f32))
    y, ssm_new = ssd_chunked(xs * dt[..., None], dt * a_cont, bm, cm, ssm0.astype(f32))
    y = y + xs * d_skip.astype(f32)[:, None]
    y = y.reshape(b, l, D_B) * jax.nn.silu(z.astype(f32))
    yg = y.reshape(b, l, G_B, D_B // G_B)
    yg = yg * lax.rsqrt(jnp.mean(yg * yg, axis=-1, keepdims=True) + NORM_EPS)
    out = yg.reshape(b, l, D_B) * gnorm_g.astype(f32)
    return out.astype(z.dtype), ssm_new, conv_new


def even_mixer(x, g, lv, shift0, wkv0, conv0, ssm0, w_in, w_out, mu, w0, w2, a0, a2, k_k, k_a, r_k,
               ln_g, ln_b, conv_w, conv_b, dt_bias, a_log, d_skip, gnorm_g):
    b, l, d = x.shape
    pa_w, ga_w, z_w, xbc_w, dt_w = _split(w_in, [A_SHIFT, D_A, D_B, CONV_DIM, H_B])
    w_ev = jnp.concatenate([ga_w, z_w, xbc_w, pa_w, dt_w, jnp.zeros((d, EV_N - IN_E), w_in.dtype)], axis=1)
    proj = norm_matmul(x.reshape(b * l, d), g, w_ev.astype(BF16), tn=256).reshape(b, l, EV_N)
    out_a, wkv_new, shift_new = rwkv7_pallas(proj, shift0, wkv0, mu, w0, w2, a0, a2, k_k, k_a, r_k, ln_g, ln_b, lv)
    z, xbc, dt_raw = (proj[:, :lv, o:o + w] for o, w in ((EV_Z, D_B), (EV_XBC, CONV_DIM), (EV_DT, H_B)))
    out_b, ssm_new, conv_new = ssd_branch(z, xbc, dt_raw, conv0, ssm0, conv_w, conv_b, dt_bias, a_log, d_skip, gnorm_g)
    out_b = jnp.pad(out_b, ((0, 0), (0, l - lv), (0, 0)))
    cat = jnp.concatenate([out_a, out_b], axis=-1).reshape(b * l, OUT_E)
    x_new = matmul_residual(cat, w_out.astype(BF16), x.reshape(b * l, d)).reshape(b, l, d)
    return x_new, (wkv_new, shift_new, ssm_new, conv_new)


def s5_combine(e1, e2):
    a1, b1 = e1
    a2, b2 = e2
    return a1 * a2, a2 * b1 + b2


def s5_branch(u, gate, h0_re, h0_im, lam_re, lam_im, log_dt, b_re, b_im, c_re, c_im, d_skip, glu_w, glu_b):
    f32 = jnp.float32
    b, l, _ = u.shape
    lam = lax.complex(lam_re.astype(f32), lam_im.astype(f32))
    delta = jnp.exp(log_dt.astype(f32))[:, None]
    lam_bar = jnp.exp(lam * delta)
    b_bar = ((lam_bar - 1.0) / lam)[..., None] * lax.complex(b_re.astype(f32), b_im.astype(f32))
    c_cplx = lax.complex(c_re.astype(f32), c_im.astype(f32))
    uf = u.astype(f32)
    bu = jnp.einsum('gpm,blgm->blgp', b_bar, uf.reshape(b, l, G_C, CH_C).astype(jnp.complex64))
    h0 = lax.complex(h0_re.astype(f32), h0_im.astype(f32))
    bu = bu.at[:, 0].add(lam_bar[None] * h0)
    _, hs = lax.associative_scan(s5_combine, (jnp.broadcast_to(lam_bar, bu.shape), bu), axis=1)
    y = jnp.real(jnp.einsum('gmp,blgp->blgm', c_cplx, hs)).reshape(b, l, D_C) + d_skip.astype(f32) * uf
    y = jax.nn.gelu(y)
    y = y * jax.nn.sigmoid(y @ glu_w.astype(f32) + glu_b.astype(f32))
    y = y * jax.nn.silu(gate.astype(f32))
    h_last = hs[:, -1]
    return y.astype(u.dtype), jnp.real(h_last), jnp.imag(h_last)


def dsa_select_attend(q, qi, wi, tpos, ki_all, gather_kv, n_sel):
    f32 = jnp.float32
    b, t = q.shape[:2]
    n_keys = ki_all.shape[1]
    dots = jnp.einsum('bthd,bsd->bths', qi.astype(f32), ki_all.astype(f32))
    score = jnp.einsum('bth,bths->bts', wi.astype(f32), jax.nn.relu(dots)) * IDX_SCALE
    admissible = jnp.arange(n_keys)[None, :] <= tpos[:, None]
    score = jnp.where(admissible[None], score, -jnp.inf)
    top_val, top_idx = lax.top_k(score, n_sel)
    valid = jnp.isfinite(top_val)
    k_sel, v_sel = gather_kv(top_idx)
    logits = jnp.einsum('btjgd,btnjd->btjgn', q.astype(f32), k_sel.astype(f32)) * (DH_D ** -0.5)
    logits = jnp.where(valid[:, :, None, None, :], logits, -jnp.inf)
    probs = jax.nn.softmax(logits, axis=-1)
    out = jnp.einsum('btjgn,btnjd->btjgd', probs, v_sel.astype(f32))
    return out.reshape(b, t, D_D)


def dsa_prompt(q, k, v, qi, ki, wi, pos):
    b, l = q.shape[:2]
    n_sel = min(TOPK_MAX, l // 4)
    qb = math.gcd(l, QBLK)
    nb = l // qb
    bidx = jnp.arange(b)[:, None, None]

    def gather_kv(idx):
        return k[bidx, idx], v[bidx, idx]

    def blk(t):
        return jnp.moveaxis(t.reshape((b, nb, qb) + t.shape[2:]), 1, 0)

    def one_block(args):
        qq, qqi, wwi, pp = args
        return dsa_select_attend(qq, qqi, wwi, pp, ki, gather_kv, n_sel)

    out = lax.map(one_block, (blk(q), blk(qi), blk(wi), pos.reshape(nb, qb)))
    return jnp.moveaxis(out, 0, 1).reshape(b, l, D_D)


def make_dsa_sample(cache_k, cache_v, cache_ki, page_table):
    past_len = page_table.shape[1] * PAGE_SIZE

    def attend(q, k, v, qi, ki, wi, pos):
        b, t = q.shape[:2]
        n_sel = min(TOPK_MAX, (past_len + t) // 4)
        ki_past = cache_ki[page_table].reshape(b, past_len, DI_D)
        ki_all = jnp.concatenate([ki_past.astype(ki.dtype), ki], axis=1)
        k_flat = cache_k.reshape(-1, KVH_D, DH_D)
        v_flat = cache_v.reshape(-1, KVH_D, DH_D)
        bidx = jnp.arange(b)[:, None, None]

        def gather_kv(idx):
            is_past = (idx < past_len)[..., None, None]
            ic = jnp.minimum(idx, past_len - 1)
            phys = page_table[bidx, ic // PAGE_SIZE] * PAGE_SIZE + ic % PAGE_SIZE
            inew = jnp.clip(idx - past_len, 0, t - 1)
            k_sel = jnp.where(is_past, k_flat[phys].astype(k.dtype), k[bidx, inew])
            v_sel = jnp.where(is_past, v_flat[phys].astype(v.dtype), v[bidx, inew])
            return k_sel, v_sel

        return dsa_select_attend(q, qi, wi, pos, ki_all, gather_kv, n_sel)

    return attend


def odd_mixer(x, g, lv, pos, c_re0, c_im0, attend, w_in, w_out, lam_re, lam_im, log_dt, b_re, b_im,
              c_re, c_im, d_skip, glu_w, glu_b):
    b, l, d = x.shape
    u_w, gc_w, q_w, k_w, v_w, qi_w, ki_w, wi_w, gd_w = _split(
        w_in, [D_C, D_C, D_D, KV_W, KV_W, QI_W, DI_D, HI_D, D_D])
    w_od = jnp.concatenate([u_w, gc_w, q_w, gd_w, k_w, v_w, qi_w, ki_w, wi_w,
                            jnp.zeros((d, OD_N - IN_O), w_in.dtype)], axis=1)
    proj = norm_matmul(x.reshape(b * l, d), g, w_od.astype(BF16), tn=384).reshape(b, l, OD_N)
    pv = proj[:, :lv]
    u, gc = pv[..., OD_U:OD_U + D_C], pv[..., OD_GC:OD_GC + D_C]
    out_c, re_last, im_last = s5_branch(u, gc, c_re0, c_im0, lam_re, lam_im, log_dt, b_re, b_im,
                                        c_re, c_im, d_skip, glu_w, glu_b)
    if attend is None:
        q_b, qi_b, k_f, v_f, ki, k_b, v_b, ki_b = rope_pallas(proj, pos)
        out_d = dsa_prompt_pallas(proj, q_b, qi_b, k_b, v_b, ki_b)
        k = k_f.reshape(b, l, KVH_D, DH_D)
        v = v_f.reshape(b, l, KVH_D, DH_D)
    else:
        q, gd, k, v, qi, ki, wi = (pv[..., o:o + w] for o, w in (
            (OD_Q, D_D), (OD_GD, D_D), (OD_K, KV_W), (OD_V, KV_W), (OD_QI, QI_W), (OD_KIWI, DI_D),
            (OD_KIWI + DI_D, HI_D)))
        q = partial_rotary(q.reshape(b, lv, H_D, DH_D), pos).reshape(b, lv, KVH_D, QPK_D, DH_D)
        k = partial_rotary(k.reshape(b, lv, KVH_D, DH_D), pos)
        v = v.reshape(b, lv, KVH_D, DH_D)
        qi = partial_rotary(qi.reshape(b, lv, HI_D, DI_D), pos)
        ki = partial_rotary(ki[:, :, None, :], pos)[:, :, 0]
        out_d = attend(q, k, v, qi, ki, wi, pos) * jax.nn.silu(gd.astype(jnp.float32))
    cat = jnp.pad(jnp.concatenate([out_c, out_d], axis=-1), ((0, 0), (0, l - lv), (0, 0))).reshape(b * l, OUT_O)
    x_new = matmul_residual(cat, w_out.astype(BF16), x.reshape(b * l, d)).reshape(b, l, d)
    return x_new, (re_last, im_last, k, v, ki)


def kernel(x_prompt, x_sample, state_a_wkv, state_a_shift, state_b_ssm, state_b_conv, state_c_re, state_c_im, cache_d_k, cache_d_v, cache_d_kidx, page_table, norm_g, final_norm_g, w_in_e, w_out_e, rwkv_mu, rwkv_w0, rwkv_w2, rwkv_a0, rwkv_a2, rwkv_kk, rwkv_ka, rwkv_rk, rwkv_ln_g, rwkv_ln_b, ssd_conv_w, ssd_conv_b, ssd_dt_bias, ssd_a_log, ssd_d, ssd_norm_g, w_in_o, w_out_o, s5_lam_re, s5_lam_im, s5_log_dt, s5_b_re, s5_b_im, s5_c_re, s5_c_im, s5_d, s5_glu_w, s5_glu_b):
    f32 = jnp.float32
    bp, lp = x_prompt.shape[:2]
    bs, ls = x_sample.shape[:2]
    pos_p = jnp.arange(lp)
    pos_s = PAST_LEN + jnp.arange(ls)
    ls_pad = _round_up(ls, 8)
    xp, xs = x_prompt, jnp.pad(x_sample, ((0, 0), (0, ls_pad - ls), (0, 0)))
    even_p, even_s, odd_p, odd_s = [], [], [], []
    for i in range(DEPTH):
        j = i // 2
        if i % 2 == 0:
            pe = (w_in_e[j], w_out_e[j], rwkv_mu[j], rwkv_w0[j], rwkv_w2[j], rwkv_a0[j], rwkv_a2[j],
                  rwkv_kk[j], rwkv_ka[j], rwkv_rk[j], rwkv_ln_g[j], rwkv_ln_b[j], ssd_conv_w[j],
                  ssd_conv_b[j], ssd_dt_bias[j], ssd_a_log[j], ssd_d[j], ssd_norm_g[j])
            xp, st_p = even_mixer(xp, norm_g[i], lp, jnp.zeros((bp, A_SHIFT), f32), jnp.zeros((bp, H_A, HD_A, HD_A), f32),
                                  jnp.zeros((bp, CONV_W - 1, CONV_DIM), f32),
                                  jnp.zeros((bp, H_B, P_B, N_B), f32), *pe)
            xs, st_s = even_mixer(xs, norm_g[i], ls, state_a_shift[j], state_a_wkv[j], state_b_conv[j], state_b_ssm[j], *pe)
            even_p.append(st_p)
            even_s.append(st_s)
        else:
            po = (w_in_o[j], w_out_o[j], s5_lam_re[j], s5_lam_im[j], s5_log_dt[j], s5_b_re[j], s5_b_im[j],
                  s5_c_re[j], s5_c_im[j], s5_d[j], s5_glu_w[j], s5_glu_b[j])
            zc = jnp.zeros((bp, G_C, P_C), f32)
            xp, st_p = odd_mixer(xp, norm_g[i], lp, pos_p, zc, zc, None, *po)
            attend_s = make_dsa_sample(cache_d_k[j], cache_d_v[j], cache_d_kidx[j], page_table)
            xs, st_s = odd_mixer(xs, norm_g[i], ls, pos_s, state_c_re[j], state_c_im[j], attend_s, *po)
            odd_p.append(st_p)
            odd_s.append(st_s)
    y_prompt = rmsnorm_rows(xp.reshape(bp * lp, D_MODEL), final_norm_g).reshape(bp, lp, D_MODEL)
    y_sample = rmsnorm_rows(xs.reshape(bs * ls_pad, D_MODEL), final_norm_g).reshape(bs, ls_pad, D_MODEL)[:, :ls]
    new_a_wkv_p, new_a_shift_p, new_b_ssm_p, new_b_conv_p = [jnp.stack(t) for t in zip(*even_p)]
    new_a_wkv_s, new_a_shift_s, new_b_ssm_s, new_b_conv_s = [jnp.stack(t) for t in zip(*even_s)]
    new_c_re_p, new_c_im_p, new_d_k_p, new_d_v_p, new_d_kidx_p = [jnp.stack(t) for t in zip(*odd_p)]
    new_c_re_s, new_c_im_s, new_d_k_s, new_d_v_s, new_d_kidx_s = [jnp.stack(t) for t in zip(*odd_s)]
    return (y_prompt, y_sample,
            new_a_wkv_p, new_a_shift_p, new_b_ssm_p, new_b_conv_p,
            new_c_re_p, new_c_im_p, new_d_k_p, new_d_v_p, new_d_kidx_p,
            new_a_wkv_s, new_a_shift_s, new_b_ssm_s, new_b_conv_s,
            new_c_re_s, new_c_im_s, new_d_k_s, new_d_v_s, new_d_kidx_s)
```

```python
import functools
import math

import jax
import jax.numpy as jnp
import numpy as np
from jax import lax
from jax.experimental import pallas as pl
from jax.experimental.pallas import tpu as pltpu

D_MODEL = 1024
DEPTH = 4
PAST_LEN = 2048
PAGE_SIZE = 128
NORM_EPS = 1e-6

D_A = D_MODEL
HD_A = 64
H_A = D_A // HD_A
W_LORA = 64
A_LORA = 64
A_SHIFT = 3 * D_A + W_LORA + A_LORA
RWKV_LN_EPS = 64e-5
D_B = D_MODEL
P_B = 64
H_B = D_B // P_B
N_B = 128
G_B = 4
HPG_B = H_B // G_B
CONV_W = 4
CONV_DIM = D_B + 2 * G_B * N_B
SSD_CHUNK = 128
D_C = D_MODEL // 2
CH_C = 16
G_C = D_C // CH_C
P_C = 64
H_D = 8
DH_D = 128
KVH_D = 2
QPK_D = H_D // KVH_D
D_D = H_D * DH_D
HI_D = 8
DI_D = 64
IDX_SCALE = (DI_D ** -0.5) * (HI_D ** -0.5)
TOPK_MAX = 256
QBLK = 128
ROPE_THETA = 500000.0
ROPE_FRAC = 4

IN_E = A_SHIFT + D_A + D_B + CONV_DIM + H_B
OUT_E = D_A + D_B
IN_O = 2 * D_C + D_D + 2 * KVH_D * DH_D + HI_D * DI_D + DI_D + HI_D + D_D
OUT_O = D_C + D_D

F32 = jnp.float32
BF16 = jnp.bfloat16

TILE_M = 512
TILE_N = 512


def _round_up(n, m):
    return (n + m - 1) // m * m


def _norm_matmul_kernel(x_ref, g_ref, w_ref, o_ref, h_ref):
    @pl.when(pl.program_id(1) == 0)
    def _():
        x = x_ref[...]
        ms = jnp.mean(x * x, axis=-1, keepdims=True)
        h_ref[...] = (x * lax.rsqrt(ms + NORM_EPS) * g_ref[...]).astype(BF16)

    o_ref[...] = jnp.dot(h_ref[...], w_ref[...], preferred_element_type=F32)


def norm_matmul(x2d, g, w_bf16, tn=TILE_N):
    m, d = x2d.shape
    n = w_bf16.shape[1]
    tm = min(TILE_M, m)
    assert m % tm == 0 and n % tn == 0
    return pl.pallas_call(
        _norm_matmul_kernel,
        grid=(m // tm, n // tn),
        in_specs=[
            pl.BlockSpec((tm, d), lambda i, j: (i, 0)),
            pl.BlockSpec((1, d), lambda i, j: (0, 0)),
            pl.BlockSpec((d, tn), lambda i, j: (0, j)),
        ],
        out_specs=pl.BlockSpec((tm, tn), lambda i, j: (i, j)),
        out_shape=jax.ShapeDtypeStruct((m, n), F32),
        scratch_shapes=[pltpu.VMEM((tm, d), BF16)],
        compiler_params=pltpu.CompilerParams(dimension_semantics=("parallel", "arbitrary")),
        name="norm_matmul",
    )(x2d, g.reshape(1, d), w_bf16)


def _matmul_res_kernel(a1_ref, a2_ref, w1_ref, w2_ref, r_ref, o_ref):
    o_ref[...] = (r_ref[...] + jnp.dot(a1_ref[...].astype(BF16), w1_ref[...], preferred_element_type=F32)
                  + jnp.dot(a2_ref[...].astype(BF16), w2_ref[...], preferred_element_type=F32))


def matmul_residual(a1, a2, w_bf16, res2d):
    m, k1 = a1.shape
    k2 = a2.shape[1]
    n = w_bf16.shape[1]
    tm = min(TILE_M, m)
    assert m % tm == 0 and w_bf16.shape[0] == k1 + k2
    return pl.pallas_call(
        _matmul_res_kernel,
        grid=(m // tm,),
        in_specs=[
            pl.BlockSpec((tm, k1), lambda i: (i, 0)),
            pl.BlockSpec((tm, k2), lambda i: (i, 0)),
            pl.BlockSpec((k1, n), lambda i: (0, 0)),
            pl.BlockSpec((k2, n), lambda i: (0, 0)),
            pl.BlockSpec((tm, n), lambda i: (i, 0)),
        ],
        out_specs=pl.BlockSpec((tm, n), lambda i: (i, 0)),
        out_shape=jax.ShapeDtypeStruct((m, n), F32),
        compiler_params=pltpu.CompilerParams(dimension_semantics=("parallel",)),
        name="matmul_residual",
    )(a1, a2, w_bf16[:k1], w_bf16[k1:], res2d)


def _rmsnorm_kernel(x_ref, g_ref, o_ref):
    x = x_ref[...]
    ms = jnp.mean(x * x, axis=-1, keepdims=True)
    o_ref[...] = x * lax.rsqrt(ms + NORM_EPS) * g_ref[...]


def rmsnorm_rows(x2d, g):
    m, d = x2d.shape
    tm = min(TILE_M, m)
    return pl.pallas_call(
        _rmsnorm_kernel,
        grid=(m // tm,),
        in_specs=[pl.BlockSpec((tm, d), lambda i: (i, 0)), pl.BlockSpec((1, d), lambda i: (0, 0))],
        out_specs=pl.BlockSpec((tm, d), lambda i: (i, 0)),
        out_shape=jax.ShapeDtypeStruct((m, d), F32),
        compiler_params=pltpu.CompilerParams(dimension_semantics=("parallel",)),
        name="final_rmsnorm",
    )(x2d, g.reshape(1, d))


def _mm(a, b):
    return jnp.dot(a.astype(BF16), b.astype(BF16), preferred_element_type=F32)


def _mm_nt(a, b):
    return lax.dot_general(a.astype(BF16), b.astype(BF16), (((1,), (1,)), ((), ())), preferred_element_type=F32)


def _mm_tn(a, b):
    return lax.dot_general(a.astype(BF16), b.astype(BF16), (((0,), (0,)), ((), ())), preferred_element_type=F32)


def _hi_lo(x):
    hi = x.astype(BF16)
    lo = (x - hi.astype(F32)).astype(BF16)
    return hi, lo


def _mm_exact_lhs(a_bf16, x):
    hi, lo = _hi_lo(x)
    return (jnp.dot(a_bf16, hi, preferred_element_type=F32) + jnp.dot(a_bf16, lo, preferred_element_type=F32))


def _mm_exact_rhs(x, b_bf16):
    hi, lo = _hi_lo(x)
    return (jnp.dot(hi, b_bf16, preferred_element_type=F32) + jnp.dot(lo, b_bf16, preferred_element_type=F32))


def _softplus(x):
    return jnp.maximum(x, 0.0) + jnp.log1p(jnp.exp(-jnp.abs(x)))


def _sigmoid(x):
    return 1.0 / (1.0 + jnp.exp(-x))


def _unit_lower_inverses(a_list, row, col, n):
    eye = jnp.where(row == col, 1.0, 0.0).astype(F32)
    ts = [eye for _ in a_list]
    m = 1
    while m < n:
        in_pair = (row // (2 * m)) == (col // (2 * m))
        lvl = in_pair & ((row % (2 * m)) >= m) & ((col % (2 * m)) < m)
        ls = [jnp.where(lvl, a, 0.0) for a in a_list]
        if m == 1:
            ts = [t - l for t, l in zip(ts, ls)]
        else:
            tl = [_mm(t, l) for t, l in zip(ts, ls)]
            ts = [t - _mm(x, t) for t, x in zip(ts, tl)]
        m *= 2
    return ts


EV_GA, EV_Z, EV_XBC, EV_R, EV_K, EV_V, EV_WA, EV_DT = 0, 1024, 2048, 4096, 5120, 6144, 7168, 7296
EV_N = 7424
RWKV_CHUNK = 64


def _rwkv_kernel(r_ref, k_ref, v_ref, wa_ref, g_ref, sh0_ref, s0_ref, mu_ref, pv_ref, w2_ref, a2_ref, bones_ref,
                 o_ref, snew_ref, shnew_ref, st_ref, carry_ref, *, chunk, n_valid):
    C = chunk
    c = pl.program_id(1)

    @pl.when(c == 0)
    def _():
        carry_ref[...] = sh0_ref[0]
        st_ref[...] = s0_ref[0]

    mu = mu_ref[...]
    pv = pv_ref[...]
    w0, a0, k_k, k_a, r_k, ln_g, ln_b = (pv[i:i + 1] for i in range(7))
    carry = carry_ref[...]
    row1 = lax.broadcasted_iota(jnp.int32, (C, 1), 0)

    def tok_shift(x, lo, hi):
        prev = jnp.where(row1 == 0, carry[:, lo:hi], pltpu.roll(x, 1, 0))
        return x + (prev - x) * mu[:, lo:hi]

    pr, pk, pvv, pwa = r_ref[0], k_ref[0], v_ref[0], wa_ref[0]
    r = tok_shift(pr, 0, D_A)
    k = tok_shift(pk, D_A, 2 * D_A)
    v = tok_shift(pvv, 2 * D_A, 3 * D_A)
    wa = tok_shift(pwa, 3 * D_A, A_SHIFT)
    last = n_valid - 1
    new_carry = jnp.concatenate([pr[last:last + 1], pk[last:last + 1], pvv[last:last + 1], pwa[last:last + 1]], axis=1)
    carry_ref[...] = new_carry
    shnew_ref[0] = new_carry

    wl, al = wa[:, :W_LORA], wa[:, W_LORA:]
    w_log = -_softplus(-(w0 + _mm(jnp.tanh(wl), w2_ref[...]))) - 0.5
    logw = -jnp.exp(w_log)
    a = _sigmoid(a0 + _mm(al, a2_ref[...]))
    bones = bones_ref[...]
    kk = k * k_k
    kk = kk / jnp.maximum(jnp.sqrt(_mm_exact_rhs(kk * kk, bones)), 1e-12)
    kp = k * (1.0 + (a - 1.0) * k_a)
    bonus = _mm_exact_rhs(r * kp * r_k, bones) * v
    if n_valid < C:
        ok = row1 < n_valid
        logw = jnp.where(ok, logw, 0.0)
        kk = jnp.where(ok, kk, 0.0)
        kp = jnp.where(ok, kp, 0.0)

    row = lax.broadcasted_iota(jnp.int32, (C, C), 0)
    col = lax.broadcasted_iota(jnp.int32, (C, C), 1)
    incl = col <= row
    strict = col < row
    cum = _mm_exact_lhs(incl.astype(BF16), logw)
    cum_last = cum[C - 1:C]
    eneg = jnp.exp(-cum)
    alpha = kk * jnp.exp(cum - logw)
    ka = kk * a
    beta = ka * eneg
    kappa = kp * eneg
    rho = r * jnp.exp(cum)
    dec_end = jnp.exp(cum_last - cum)
    kappa_e = kp * dec_end
    beta_e = ka * dec_end
    wc = jnp.exp(cum_last)

    heads = range(H_A)
    sls = [slice(h * HD_A, (h + 1) * HD_A) for h in heads]
    s0s = [st_ref[h] for h in heads]
    bks = [jnp.concatenate([beta[:, sl], kappa[:, sl]], axis=0) for sl in sls]
    g_as = [_mm_nt(alpha[:, sl], bk) for sl, bk in zip(sls, bks)]
    g_rs = [_mm_nt(rho[:, sl], bk) for sl, bk in zip(sls, bks)]
    t_invs = _unit_lower_inverses([jnp.where(strict, g[:, :C], 0.0) for g in g_as], row, col, C)
    rhss = [_mm_nt(alpha[:, sl], s0) + _mm(jnp.where(strict, g[:, C:], 0.0), v[:, sl])
            for sl, s0, g in zip(sls, s0s, g_as)]
    us = [_mm(t, x) for t, x in zip(t_invs, rhss)]
    outs = [_mm_nt(rho[:, sl], s0) + _mm(jnp.where(incl, g[:, C:], 0.0), v[:, sl])
            - _mm(jnp.where(incl, g[:, :C], 0.0), u) for sl, s0, g, u in zip(sls, s0s, g_rs, us)]
    for h, sl, s0, u in zip(heads, sls, s0s, us):
        x = jnp.concatenate([v[:, sl], -u], axis=0)
        y = jnp.concatenate([kappa_e[:, sl], beta_e[:, sl]], axis=0)
        st_ref[h] = s0 * wc[:, sl] + _mm_tn(x, y)
    out = jnp.concatenate(outs, axis=1)

    mean = _mm_exact_rhs(out, bones) * (1.0 / HD_A)
    d = out - mean
    var = _mm_exact_rhs(d * d, bones) * (1.0 / HD_A)
    y = d * lax.rsqrt(var + RWKV_LN_EPS) * ln_g + ln_b
    gate = g_ref[0]
    o_ref[0] = (y + bonus) * (gate * _sigmoid(gate))
    snew_ref[0] = st_ref[...]


def rwkv7_pallas(proj3d, shift0, wkv0, mu, w0, w2, a0, a2, k_k, k_a, r_k, ln_g, ln_b, n_valid_tokens):
    b, l, _ = proj3d.shape
    chunk = min(RWKV_CHUNK, l)
    n_chunks = l // chunk
    assert l % chunk == 0 and chunk % 8 == 0
    n_valid = n_valid_tokens - (n_chunks - 1) * chunk
    assert 0 < n_valid <= chunk and (n_valid == chunk or n_chunks == 1)
    pvec = jnp.stack([w0, a0, k_k, k_a, r_k.reshape(D_A), ln_g, ln_b, jnp.zeros_like(w0)])
    hid = jnp.arange(D_A) // HD_A
    bones = (hid[:, None] == hid[None, :]).astype(BF16)
    blk = lambda w, j: pl.BlockSpec((1, chunk, w), lambda bi, ci: (bi, ci, j))
    const = lambda shape: pl.BlockSpec(shape, lambda bi, ci: (0,) * len(shape))
    out, wkv_new, shift_new = pl.pallas_call(
        functools.partial(_rwkv_kernel, chunk=chunk, n_valid=n_valid),
        grid=(b, n_chunks),
        in_specs=[
            blk(D_A, EV_R // D_A), blk(D_A, EV_K // D_A), blk(D_A, EV_V // D_A), blk(128, EV_WA // 128),
            blk(D_A, EV_GA // D_A),
            pl.BlockSpec((1, 1, A_SHIFT), lambda bi, ci: (bi, 0, 0)),
            pl.BlockSpec((1, H_A, HD_A, HD_A), lambda bi, ci: (bi, 0, 0, 0)),
            const((1, A_SHIFT)), const((8, D_A)), const((W_LORA, D_A)), const((A_LORA, D_A)), const((D_A, D_A)),
        ],
        out_specs=[
            pl.BlockSpec((1, chunk, D_A), lambda bi, ci: (bi, ci, 0)),
            pl.BlockSpec((1, H_A, HD_A, HD_A), lambda bi, ci: (bi, 0, 0, 0)),
            pl.BlockSpec((1, 1, A_SHIFT), lambda bi, ci: (bi, 0, 0)),
        ],
        out_shape=[
            jax.ShapeDtypeStruct((b, l, D_A), F32),
            jax.ShapeDtypeStruct((b, H_A, HD_A, HD_A), F32),
            jax.ShapeDtypeStruct((b, 1, A_SHIFT), F32),
        ],
        scratch_shapes=[pltpu.VMEM((H_A, HD_A, HD_A), F32), pltpu.VMEM((1, A_SHIFT), F32)],
        compiler_params=pltpu.CompilerParams(dimension_semantics=("parallel", "arbitrary")),
        name="rwkv7_chunked",
    )(proj3d, proj3d, proj3d, proj3d, proj3d, shift0.reshape(b, 1, A_SHIFT), wkv0,
      mu.reshape(1, A_SHIFT), pvec, w2.astype(BF16), a2.astype(BF16), bones)
    return out, wkv_new, shift_new.reshape(b, A_SHIFT)


def rwkv7_pallas_from_parts(pa, ga, shift0, wkv0, mu, w0, w2, a0, a2, k_k, k_a, r_k, ln_g, ln_b):
    b, l, _ = pa.shape
    lp = _round_up(l, 8)
    proj = jnp.zeros((b, lp, EV_N), F32)
    proj = proj.at[:, :l, EV_GA:EV_GA + D_A].set(ga)
    proj = proj.at[:, :l, EV_R:EV_R + 3 * D_A].set(pa[..., :3 * D_A])
    proj = proj.at[:, :l, EV_WA:EV_WA + 128].set(pa[..., 3 * D_A:])
    out, wkv_new, shift_new = rwkv7_pallas(proj, shift0, wkv0, mu, w0, w2, a0, a2, k_k, k_a, r_k, ln_g, ln_b, l)
    return out[:, :l], wkv_new, shift_new


OD_U, OD_GC, OD_Q, OD_GD, OD_K, OD_V, OD_QI, OD_KIWI = 0, 512, 1024, 2048, 3072, 3328, 3584, 4096
OD_N = 4224
KV_W = KVH_D * DH_D
QI_W = HI_D * DI_D
INT_MIN = -2 ** 31


def _rope_tables(pos, head_dim):
    rd = head_dim // ROPE_FRAC
    half = rd // 2
    inv_freq = ROPE_THETA ** (-jnp.arange(half, dtype=F32) / half)
    ang = pos.astype(F32)[:, None] * inv_freq[None, :]
    cos, sin = jnp.cos(ang), jnp.sin(ang)
    t = pos.shape[0]
    ones = jnp.ones((t, head_dim - rd), F32)
    cos_h = jnp.concatenate([cos, cos, ones], axis=1)
    sin_h = jnp.concatenate([-sin, sin, 0.0 * ones], axis=1)
    reps = 128 // head_dim
    return jnp.tile(cos_h, (1, reps)), jnp.tile(sin_h, (1, reps))


def _rotate(x, cos_t, sin_t, head_dim):
    w = x.shape[1]
    half = head_dim // ROPE_FRAC // 2
    lane = lax.broadcasted_iota(jnp.int32, x.shape, 1) % head_dim
    if w >= 128:
        cos_f = jnp.tile(cos_t, (1, w // 128))
        sin_f = jnp.tile(sin_t, (1, w // 128))
        partner = jnp.where(lane < half, pltpu.roll(x, w - half, 1), pltpu.roll(x, half, 1))
    else:
        cos_f, sin_f = cos_t[:, :w], sin_t[:, :w]
        partner = jnp.where(lane < half, jnp.concatenate([x[:, half:], x[:, :half]], axis=1),
                            jnp.concatenate([x[:, w - half:], x[:, :w - half]], axis=1))
    return x * cos_f + partner * sin_f


def _rope_kernel(q_ref, k_ref, v_ref, qi_ref, kiwi_ref, c128_ref, s128_ref, c64_ref, s64_ref,
                 qo_ref, qio_ref, ko_ref, vo_ref, kio_ref, kb_ref, vb_ref, kib_ref):
    c128, s128, c64, s64 = c128_ref[...], s128_ref[...], c64_ref[...], s64_ref[...]
    qo_ref[0] = _rotate(q_ref[0], c128, s128, DH_D).astype(BF16)
    qio_ref[0] = _rotate(qi_ref[0], c64, s64, DI_D).astype(BF16)
    k_rot = _rotate(k_ref[0], c128, s128, DH_D)
    ko_ref[0] = k_rot
    kb_ref[0] = k_rot.astype(BF16)
    v = v_ref[0]
    vo_ref[0] = v
    vb_ref[0] = v.astype(BF16)
    ki_rot = _rotate(kiwi_ref[0][:, :DI_D], c64, s64, DI_D)
    kio_ref[0] = ki_rot
    kib_ref[0] = ki_rot.astype(BF16)


def rope_pallas(proj3d, pos):
    b, l, _ = proj3d.shape
    tr = min(512, l)
    assert l % tr == 0
    c128, s128 = _rope_tables(pos, DH_D)
    c64, s64 = _rope_tables(pos, DI_D)
    blk = lambda w, j: pl.BlockSpec((1, tr, w), lambda bi, ti: (bi, ti, j))
    tab = pl.BlockSpec((tr, 128), lambda bi, ti: (ti, 0))
    oblk = lambda w: pl.BlockSpec((1, tr, w), lambda bi, ti: (bi, ti, 0))
    shp = lambda w, dt: jax.ShapeDtypeStruct((b, l, w), dt)
    return pl.pallas_call(
        _rope_kernel,
        grid=(b, l // tr),
        in_specs=[blk(D_D, OD_Q // D_D), blk(KV_W, OD_K // KV_W), blk(KV_W, OD_V // KV_W), blk(QI_W, OD_QI // QI_W),
                  blk(128, OD_KIWI // 128), tab, tab, tab, tab],
        out_specs=[oblk(D_D), oblk(QI_W), oblk(KV_W), oblk(KV_W), oblk(DI_D), oblk(KV_W), oblk(KV_W), oblk(DI_D)],
        out_shape=[shp(D_D, BF16), shp(QI_W, BF16), shp(KV_W, F32), shp(KV_W, F32), shp(DI_D, F32),
                   shp(KV_W, BF16), shp(KV_W, BF16), shp(DI_D, BF16)],
        compiler_params=pltpu.CompilerParams(dimension_semantics=("parallel", "parallel")),
        name="dsa_rope",
    )(proj3d, proj3d, proj3d, proj3d, proj3d, c128, s128, c64, s64)


DSA_TQ = 128
DSA_TK = 512
NEG_BIG = -1e30


def _sortable_key(s):
    bits = pltpu.bitcast(s + 0.0, jnp.int32)
    return bits ^ ((bits >> 31) & 0x7FFFFFFF)


def _dsa_prompt_kernel(q_ref, qi_ref, kiwi_ref, gd_ref, k_ref, v_ref, ki_ref, o_ref, key_ref, *, n_sel):
    tq, tk = DSA_TQ, DSA_TK
    t0 = pl.program_id(1) * tq
    n_kt = (t0 + tq + tk - 1) // tk
    qi = qi_ref[0]
    wi = kiwi_ref[0][:, DI_D:DI_D + HI_D]
    pos_q = t0 + lax.broadcasted_iota(jnp.int32, (tq, 1), 0)
    lane_k = lax.broadcasted_iota(jnp.int32, (tq, tk), 1)

    def key_slice(kt):
        return pl.ds(pl.multiple_of(kt * tk, tk), tk)

    def fold_lanes(x):
        return sum(x[:, i * 128:(i + 1) * 128] for i in range(tk // 128))

    def score_body(kt, carry):
        ks = key_slice(kt)
        kit = ki_ref[0, ks, :]
        acc = jnp.zeros((tq, tk), F32)
        for h in range(HI_D):
            d = lax.dot_general(qi[:, h * DI_D:(h + 1) * DI_D], kit, (((1,), (1,)), ((), ())),
                                preferred_element_type=F32)
            acc = acc + wi[:, h:h + 1] * jnp.maximum(d, 0.0)
        key = _sortable_key(acc * IDX_SCALE)
        key_ref[:, ks] = jnp.where(kt * tk + lane_k <= pos_q, key, INT_MIN)
        return carry

    lax.fori_loop(0, n_kt, score_body, 0)

    def count(pred_fn):
        def body(kt, c):
            return c + fold_lanes(jnp.where(pred_fn(key_ref[:, key_slice(kt)]), 1.0, 0.0))
        c = lax.fori_loop(0, n_kt, body, jnp.zeros((tq, 128), F32))
        return jnp.sum(c, axis=1, keepdims=True)

    def bit_body(i, ans):
        cand = ans | lax.shift_left(jnp.int32(1), 31 - i)
        scand = cand ^ INT_MIN
        cnt = count(lambda key: key >= scand)
        return jnp.where(cnt >= n_sel, cand, ans)

    thr = lax.fori_loop(0, 32, bit_body, jnp.zeros((tq, 1), jnp.int32)) ^ INT_MIN
    need = n_sel - count(lambda key: key > thr)

    tri = (lax.broadcasted_iota(jnp.int32, (tk, tk), 0) <= lax.broadcasted_iota(jnp.int32, (tk, tk), 1)).astype(BF16)

    def sel_body(kt, tie_seen):
        ks = key_slice(kt)
        key = key_ref[:, ks]
        tie = (key == thr) & (key != INT_MIN)
        tie_f = jnp.where(tie, 1.0, 0.0)
        rank = tie_seen + jnp.dot(tie_f.astype(BF16), tri, preferred_element_type=F32)
        sel = (key > thr) | (tie & (rank <= need))
        key_ref[:, ks] = jnp.where(sel, 1, 0)
        return tie_seen + jnp.sum(tie_f, axis=1, keepdims=True)

    lax.fori_loop(0, n_kt, sel_body, jnp.zeros((tq, 1), F32))

    gd = gd_ref[0]
    outs = []
    for j in range(KVH_D):
        qs = jnp.concatenate([q_ref[0, :, (j * QPK_D + g) * DH_D:(j * QPK_D + g + 1) * DH_D] for g in range(QPK_D)],
                             axis=0)

        def att_body(kt, carry):
            m, l, acc = carry
            ks = key_slice(kt)
            kt_j = k_ref[0, ks, j * DH_D:(j + 1) * DH_D]
            vt_j = v_ref[0, ks, j * DH_D:(j + 1) * DH_D]
            sel = key_ref[:, ks] != 0
            sel4 = jnp.concatenate([sel] * QPK_D, axis=0)
            logits = lax.dot_general(qs, kt_j, (((1,), (1,)), ((), ())), preferred_element_type=F32) * (DH_D ** -0.5)
            logits = jnp.where(sel4, logits, NEG_BIG)
            m_new = jnp.maximum(m, jnp.max(logits, axis=1, keepdims=True))
            p = jnp.where(sel4, jnp.exp(logits - m_new), 0.0)
            scale = jnp.exp(m - m_new)
            l_new = l * scale + jnp.sum(p, axis=1, keepdims=True)
            acc_new = acc * scale + jnp.dot(p.astype(BF16), vt_j, preferred_element_type=F32)
            return m_new, l_new, acc_new

        m0 = jnp.full((QPK_D * tq, 1), NEG_BIG, F32)
        l0 = jnp.zeros((QPK_D * tq, 1), F32)
        a0 = jnp.zeros((QPK_D * tq, DH_D), F32)
        _, l_f, acc_f = lax.fori_loop(0, n_kt, att_body, (m0, l0, a0))
        o = acc_f / l_f
        outs.extend(o[g * tq:(g + 1) * tq] for g in range(QPK_D))
    out = jnp.concatenate(outs, axis=1)
    o_ref[0] = out * (gd * _sigmoid(gd))


def dsa_prompt_pallas(proj3d, q_b, qi_b, k_b, v_b, ki_b):
    b, l, _ = proj3d.shape
    n_sel = min(TOPK_MAX, l // 4)
    assert l % DSA_TQ == 0 and l % DSA_TK == 0
    qblk = lambda w, j: pl.BlockSpec((1, DSA_TQ, w), lambda bi, ti: (bi, ti, j))
    full = lambda w: pl.BlockSpec((1, l, w), lambda bi, ti: (bi, 0, 0))
    return pl.pallas_call(
        functools.partial(_dsa_prompt_kernel, n_sel=float(n_sel)),
        grid=(b, l // DSA_TQ),
        in_specs=[qblk(D_D, 0), qblk(QI_W, 0), qblk(128, OD_KIWI // 128), qblk(D_D, OD_GD // D_D),
                  full(KV_W), full(KV_W), full(DI_D)],
        out_specs=pl.BlockSpec((1, DSA_TQ, D_D), lambda bi, ti: (bi, ti, 0)),
        out_shape=jax.ShapeDtypeStruct((b, l, D_D), F32),
        scratch_shapes=[pltpu.VMEM((DSA_TQ, l), jnp.int32)],
        compiler_params=pltpu.CompilerParams(dimension_semantics=("parallel", "arbitrary")),
        name="dsa_prompt",
    )(q_b, qi_b, proj3d, proj3d, k_b, v_b, ki_b)


S5_N = G_C * P_C
S5_ROWS = 2 * S5_N // 128
S5_TC = 256


def _s5_kernel(u_ref, g_ref, h0re_ref, h0im_ref, lam_t_ref, lam_r_ref, bd_ref, cd_ref, dsk_ref, gw_ref, gb_ref,
               o_ref, hre_ref, him_ref, scr_ref, h_ref, *, bb, tc, n_valid):
    c = pl.program_id(1)
    rows = bb * tc
    half = S5_ROWS // 2

    @pl.when(c == 0)
    def _():
        for i in range(bb):
            h_ref[i * S5_ROWS:i * S5_ROWS + half] = h0re_ref[i]
            h_ref[i * S5_ROWS + half:(i + 1) * S5_ROWS] = h0im_ref[i]

    def discretize(lam_ref):
        lr, li, dt = lam_ref[0], lam_ref[1], jnp.exp(lam_ref[2])
        mag = jnp.exp(lr * dt)
        br, bi = mag * jnp.cos(li * dt), mag * jnp.sin(li * dt)
        den = lr * lr + li * li
        fr = ((br - 1.0) * lr + bi * li) / den
        fi = (bi * lr - (br - 1.0) * li) / den
        return br, bi, fr, fi

    _, _, fr, fi = discretize(lam_r_ref)
    u = u_ref[...].reshape(rows, D_C)
    braw = _mm(u, bd_ref[...])
    b_re, b_im = braw[:, :S5_N], braw[:, S5_N:]
    bu = jnp.concatenate([fr * b_re - fi * b_im, fr * b_im + fi * b_re], axis=1)
    for r in range(S5_ROWS):
        scr_ref[pl.ds(r, rows, stride=S5_ROWS), :] = bu[:, r * 128:(r + 1) * 128]

    lbr, lbi, _, _ = discretize(lam_t_ref)
    for i in range(bb):
        def step(t, h):
            hr, hi = h
            idx = pl.multiple_of((i * tc + t) * S5_ROWS, S5_ROWS)
            blk = scr_ref[pl.ds(idx, S5_ROWS), :]
            nr = lbr * hr - lbi * hi + blk[:half]
            ni = lbr * hi + lbi * hr + blk[half:]
            scr_ref[pl.ds(idx, S5_ROWS), :] = jnp.concatenate([nr, ni], axis=0)
            return nr, ni

        h_in = (h_ref[i * S5_ROWS:i * S5_ROWS + half], h_ref[i * S5_ROWS + half:(i + 1) * S5_ROWS])
        hr, hi = lax.fori_loop(0, n_valid, step, h_in, unroll=4)
        h_ref[i * S5_ROWS:i * S5_ROWS + half] = hr
        h_ref[i * S5_ROWS + half:(i + 1) * S5_ROWS] = hi
        hre_ref[i] = hr
        him_ref[i] = hi

    y = dsk_ref[...] * u
    for r in range(S5_ROWS):
        y = y + _mm(scr_ref[pl.ds(r, rows, stride=S5_ROWS), :], cd_ref[r * 128:(r + 1) * 128, :])
    y = 0.5 * y * (1.0 + jnp.tanh(math.sqrt(2.0 / math.pi) * (y + 0.044715 * (y * y * y))))
    y = y * _sigmoid(_mm(y, gw_ref[...]) + gb_ref[...])
    gate = g_ref[...].reshape(rows, D_C)
    o_ref[...] = (y * (gate * _sigmoid(gate))).reshape(bb, tc, D_C)


def s5_pallas(proj3d, n_valid_tokens, h0_re, h0_im, lam_re, lam_im, log_dt, b_re, b_im, c_re, c_im, d_skip,
              glu_w, glu_b):
    b, l, _ = proj3d.shape
    tc = min(S5_TC, l)
    n_chunks = l // tc
    bb = max(1, min(b, S5_TC // tc))
    assert l % tc == 0 and b % bb == 0 and tc % 8 == 0
    n_valid = n_valid_tokens - (n_chunks - 1) * tc
    assert 0 < n_valid <= tc and (n_valid == tc or n_chunks == 1)
    half = S5_ROWS // 2
    dt_full = jnp.repeat(log_dt, P_C)
    lam_t = jnp.stack([lam_re.reshape(half, 128), lam_im.reshape(half, 128), dt_full.reshape(half, 128)])
    lam_r = jnp.stack([lam_re.reshape(1, S5_N), lam_im.reshape(1, S5_N), dt_full.reshape(1, S5_N)])
    eye = jnp.eye(G_C, dtype=F32)
    bd = jnp.concatenate([jnp.einsum('gpm,gh->gmhp', t, eye).reshape(D_C, S5_N) for t in (b_re, b_im)], axis=1)
    cd = jnp.concatenate([jnp.einsum('gmp,gh->gphm', t, eye).reshape(S5_N, D_C) for t in (c_re, -c_im)], axis=0)
    const = lambda shape: pl.BlockSpec(shape, lambda bi, ci: (0,) * len(shape))
    st_spec = pl.BlockSpec((bb, half, 128), lambda bi, ci: (bi, 0, 0))
    out, h_re, h_im = pl.pallas_call(
        functools.partial(_s5_kernel, bb=bb, tc=tc, n_valid=n_valid),
        grid=(b // bb, n_chunks),
        in_specs=[
            pl.BlockSpec((bb, tc, D_C), lambda bi, ci: (bi, ci, OD_U // D_C)),
            pl.BlockSpec((bb, tc, D_C), lambda bi, ci: (bi, ci, OD_GC // D_C)),
            st_spec, st_spec,
            const((3, half, 128)), const((3, 1, S5_N)), const((D_C, 2 * S5_N)), const((2 * S5_N, D_C)),
            const((1, D_C)), const((D_C, D_C)), const((1, D_C)),
        ],
        out_specs=[pl.BlockSpec((bb, tc, D_C), lambda bi, ci: (bi, ci, 0)), st_spec, st_spec],
        out_shape=[jax.ShapeDtypeStruct((b, l, D_C), F32), jax.ShapeDtypeStruct((b, half, 128), F32),
                   jax.ShapeDtypeStruct((b, half, 128), F32)],
        scratch_shapes=[pltpu.VMEM((bb * tc * S5_ROWS, 128), F32), pltpu.VMEM((bb * S5_ROWS, 128), F32)],
        compiler_params=pltpu.CompilerParams(dimension_semantics=("parallel", "arbitrary")),
        name="s5_scan",
    )(proj3d, proj3d, h0_re.reshape(b, half, 128), h0_im.reshape(b, half, 128), lam_t, lam_r,
      bd.astype(BF16), cd.astype(BF16), d_skip.reshape(1, D_C), glu_w.astype(BF16), glu_b.reshape(1, D_C))
    return out, h_re.reshape(b, G_C, P_C), h_im.reshape(b, G_C, P_C)


def _hi_mid_lo(x):
    hi = x.astype(BF16)
    r1 = x - hi.astype(F32)
    mid = r1.astype(BF16)
    lo = (r1 - mid.astype(F32)).astype(BF16)
    return hi, mid, lo


def _mm3_rhs01(x, b_bf16):
    return sum(jnp.dot(p, b_bf16, preferred_element_type=F32) for p in _hi_mid_lo(x))


def _ssd_kernel(z_ref, xbc_ref, dt_ref, conv0_ref, ssm0_ref, cw_ref, cb_ref, sm_ref, pv_ref, e16_ref, bones_ref,
                o_ref, ssm_ref, convn_ref, buf_ref, st_ref, *, cl, n_valid):
    c = pl.program_id(1)
    keep = CONV_W - 1

    @pl.when(c == 0)
    def _():
        buf_ref[8 - keep:8] = conv0_ref[0]
        st_ref[...] = ssm0_ref[0]

    buf_ref[8:8 + cl] = xbc_ref[0]
    cw = cw_ref[...]
    conv = cb_ref[...] + sum(buf_ref[8 - keep + i:8 - keep + i + cl] * cw[i:i + 1] for i in range(CONV_W))
    convn_ref[0] = buf_ref[8 + n_valid - keep:8 + n_valid]
    buf_ref[8 - keep:8] = buf_ref[8 + cl - keep:8 + cl]
    act = conv * _sigmoid(conv)
    xs, bm, cm = act[:, :D_B], act[:, D_B:D_B + G_B * N_B], act[:, D_B + G_B * N_B:]

    sm = sm_ref[...]
    dt = _softplus(dt_ref[0][:, :H_B] + sm[0:1, :H_B])
    row1 = lax.broadcasted_iota(jnp.int32, (cl, 1), 0)
    if n_valid < cl:
        dt = jnp.where(row1 < n_valid, dt, 0.0)
    a = dt * (-jnp.exp(sm[1:2, :H_B]))
    row = lax.broadcasted_iota(jnp.int32, (cl, cl), 0)
    col = lax.broadcasted_iota(jnp.int32, (cl, cl), 1)
    incl = col <= row
    a_parts = _hi_mid_lo(a)
    a_cum = sum(jnp.dot(incl.astype(BF16), p, preferred_element_type=F32) for p in a_parts)
    a_cum_t = sum(lax.dot_general(p, (row <= col).astype(BF16), (((0,), (0,)), ((), ())),
                                  preferred_element_type=F32) for p in a_parts)
    a_last = a_cum[cl - 1:cl]
    e16 = e16_ref[...]
    xdt = xs * _mm3_rhs01(dt, e16)
    ea_full = jnp.exp(_mm3_rhs01(a_cum, e16))
    xdec = xdt * jnp.exp(_mm3_rhs01(a_last - a_cum, e16))
    chunk_decay = jnp.exp(a_last)

    groups = range(G_B)
    gsl = [slice(g * N_B, (g + 1) * N_B) for g in groups]
    cbs = [_mm_nt(cm[:, s], bm[:, s]) for s in gsl]
    y_off = jnp.concatenate(
        [_mm_nt(cm[:, s], st_ref[g * HPG_B:(g + 1) * HPG_B].reshape(HPG_B * P_B, N_B)) for g, s in zip(groups, gsl)],
        axis=1)
    y_diag = []
    for h in range(H_B):
        hs = slice(h * P_B, (h + 1) * P_B)
        lmat = jnp.exp(jnp.where(incl, a_cum[:, h:h + 1] - a_cum_t[h:h + 1, :], NEG_BIG))
        y_diag.append(_mm(cbs[h // HPG_B] * lmat, xdt[:, hs]))
    for h in range(H_B):
        hs = slice(h * P_B, (h + 1) * P_B)
        st_ref[h] = st_ref[h] * chunk_decay[:, h:h + 1] + _mm_tn(xdec[:, hs], bm[:, gsl[h // HPG_B]])

    pv = pv_ref[...]
    y = jnp.concatenate(y_diag, axis=1) + y_off * ea_full + xs * pv[0:1]
    z = z_ref[0]
    y = y * (z * _sigmoid(z))
    ms = _mm_exact_rhs(y * y, bones_ref[...]) * (1.0 / (D_B // G_B))
    o_ref[0] = y * lax.rsqrt(ms + NORM_EPS) * pv[1:2]

    @pl.when(c == pl.num_programs(1) - 1)
    def _():
        ssm_ref[0] = st_ref[...]


def ssd_pallas(proj3d, n_valid_tokens, conv0, ssm0, conv_w, conv_b, dt_bias, a_log, d_skip, gnorm_g):
    b, l, _ = proj3d.shape
    cl = min(SSD_CHUNK, l)
    n_chunks = l // cl
    assert l % cl == 0 and cl % 8 == 0
    n_valid = n_valid_tokens - (n_chunks - 1) * cl
    assert 0 < n_valid <= cl and (n_valid == cl or n_chunks == 1)
    sm = jnp.zeros((8, 128), F32).at[0, :H_B].set(dt_bias).at[1, :H_B].set(a_log)
    pv = jnp.zeros((8, D_B), F32).at[0].set(jnp.repeat(d_skip, P_B)).at[1].set(gnorm_g)
    e16 = (jnp.arange(D_B)[None, :] // P_B == jnp.arange(H_B)[:, None]).astype(BF16)
    gid = jnp.arange(D_B) // (D_B // G_B)
    bones = (gid[:, None] == gid[None, :]).astype(BF16)
    const = lambda shape: pl.BlockSpec(shape, lambda bi, ci: (0,) * len(shape))
    return pl.pallas_call(
        functools.partial(_ssd_kernel, cl=cl, n_valid=n_valid),
        grid=(b, n_chunks),
        in_specs=[
            pl.BlockSpec((1, cl, D_B), lambda bi, ci: (bi, ci, EV_Z // D_B)),
            pl.BlockSpec((1, cl, CONV_DIM), lambda bi, ci: (bi, ci, EV_XBC // CONV_DIM)),
            pl.BlockSpec((1, cl, 128), lambda bi, ci: (bi, ci, EV_DT // 128)),
            pl.BlockSpec((1, CONV_W - 1, CONV_DIM), lambda bi, ci: (bi, 0, 0)),
            pl.BlockSpec((1, H_B, P_B, N_B), lambda bi, ci: (bi, 0, 0, 0)),
            const((CONV_W, CONV_DIM)), const((1, CONV_DIM)), const((8, 128)), const((8, D_B)),
            const((H_B, D_B)), const((D_B, D_B)),
        ],
        out_specs=[
            pl.BlockSpec((1, cl, D_B), lambda bi, ci: (bi, ci, 0)),
            pl.BlockSpec((1, H_B, P_B, N_B), lambda bi, ci: (bi, 0, 0, 0)),
            pl.BlockSpec((1, CONV_W - 1, CONV_DIM), lambda bi, ci: (bi, 0, 0)),
        ],
        out_shape=[jax.ShapeDtypeStruct((b, l, D_B), F32), jax.ShapeDtypeStruct((b, H_B, P_B, N_B), F32),
                   jax.ShapeDtypeStruct((b, CONV_W - 1, CONV_DIM), F32)],
        scratch_shapes=[pltpu.VMEM((8 + cl, CONV_DIM), F32), pltpu.VMEM((H_B, P_B, N_B), F32)],
        compiler_params=pltpu.CompilerParams(dimension_semantics=("parallel", "arbitrary")),
        name="ssd_chunked",
    )(proj3d, proj3d, proj3d, conv0, ssm0, conv_w, conv_b.reshape(1, CONV_DIM), sm, pv, e16, bones)


def _dsa_sample_kernel(pt_ref, q_ref, qi_ref, kiwi_ref, gd_ref, kn_ref, vn_ref, kin_ref, *rest, n_pages, n_sel, tq):
    kp_refs = rest[:n_pages]
    vp_refs = rest[n_pages:2 * n_pages]
    kip_refs = rest[2 * n_pages:3 * n_pages]
    o_ref = rest[3 * n_pages]
    ps = PAGE_SIZE
    n_tiles = n_pages + 1
    pad_rows = lambda x: jnp.concatenate([x, jnp.zeros((ps - tq, x.shape[1]), x.dtype)], axis=0)

    qi = qi_ref[0]
    qs_i = jnp.concatenate([qi[:, h * DI_D:(h + 1) * DI_D] for h in range(HI_D)], axis=0)
    wi = kiwi_ref[0][:, DI_D:DI_D + HI_D]
    wcol = jnp.concatenate([wi[:, h:h + 1] for h in range(HI_D)], axis=0)
    ki_tiles = [r[0].astype(BF16) for r in kip_refs] + [pad_rows(kin_ref[0])]
    scores = []
    for kit in ki_tiles:
        d = wcol * jnp.maximum(_mm_nt(qs_i, kit), 0.0)
        scores.append(sum(d[h * tq:(h + 1) * tq] for h in range(HI_D)))
    key = _sortable_key(jnp.concatenate(scores, axis=1) * IDX_SCALE)
    n_keys = n_tiles * ps
    qrow = lax.broadcasted_iota(jnp.int32, (tq, n_keys), 0)
    kcol = lax.broadcasted_iota(jnp.int32, (tq, n_keys), 1)
    key = jnp.where(kcol - n_pages * ps <= qrow, key, INT_MIN)

    def count(pred):
        return jnp.sum(jnp.where(pred, 1.0, 0.0), axis=1, keepdims=True)

    def bit_body(i, ans):
        cand = ans | lax.shift_left(jnp.int32(1), 31 - i)
        return jnp.where(count(key >= (cand ^ INT_MIN)) >= n_sel, cand, ans)

    thr = lax.fori_loop(0, 32, bit_body, jnp.zeros((tq, 1), jnp.int32)) ^ INT_MIN
    gt = key > thr
    need = n_sel - count(gt)
    tie = (key == thr) & (key != INT_MIN)
    tie_f = jnp.where(tie, 1.0, 0.0)
    tri = (lax.broadcasted_iota(jnp.int32, (ps, ps), 0) <= lax.broadcasted_iota(jnp.int32, (ps, ps), 1)).astype(BF16)
    seen = jnp.zeros((tq, 1), F32)
    ranks = []
    for t in range(n_tiles):
        tf = tie_f[:, t * ps:(t + 1) * ps]
        ranks.append(seen + jnp.dot(tf.astype(BF16), tri, preferred_element_type=F32))
        seen = seen + jnp.sum(tf, axis=1, keepdims=True)
    sel = gt | (tie & (jnp.concatenate(ranks, axis=1) <= need))
    sel4 = jnp.concatenate([sel] * QPK_D, axis=0)

    kn, vn = pad_rows(kn_ref[0]), pad_rows(vn_ref[0])
    q = q_ref[0]
    outs = []
    for j in range(KVH_D):
        hs = slice(j * DH_D, (j + 1) * DH_D)
        qs = jnp.concatenate([q[:, (j * QPK_D + g) * DH_D:(j * QPK_D + g + 1) * DH_D] for g in range(QPK_D)], axis=0)
        k_tiles = [r[0, :, hs].astype(BF16) for r in kp_refs] + [kn[:, hs]]
        v_tiles = [r[0, :, hs].astype(BF16) for r in vp_refs] + [vn[:, hs]]
        logits = jnp.concatenate([_mm_nt(qs, kt) for kt in k_tiles], axis=1) * (DH_D ** -0.5)
        logits = jnp.where(sel4, logits, NEG_BIG)
        m = jnp.max(logits, axis=1, keepdims=True)
        p = jnp.where(sel4, jnp.exp(logits - m), 0.0)
        l = jnp.sum(p, axis=1, keepdims=True)
        pb = p.astype(BF16)
        acc = sum(jnp.dot(pb[:, t * ps:(t + 1) * ps], vt, preferred_element_type=F32) for t, vt in enumerate(v_tiles))
        o = acc / l
        outs.extend(o[g * tq:(g + 1) * tq] for g in range(QPK_D))
    gd = gd_ref[0]
    o_ref[0] = jnp.concatenate(outs, axis=1) * (gd * _sigmoid(gd))


def dsa_sample_pallas(proj3d, q_b, qi_b, k_b, v_b, ki_b, cache_k, cache_v, cache_ki, page_table, n_valid_tokens):
    b, tq, _ = proj3d.shape
    n_pages = page_table.shape[1]
    n_pool = cache_k.shape[0]
    n_sel = min(TOPK_MAX, (n_pages * PAGE_SIZE + n_valid_tokens) // 4)
    ck = cache_k.reshape(n_pool, PAGE_SIZE, KV_W)
    cv = cache_v.reshape(n_pool, PAGE_SIZE, KV_W)
    row = lambda w, j=0: pl.BlockSpec((1, tq, w), lambda bi, pt, j=j: (bi, 0, j))
    page = lambda w, p: pl.BlockSpec((1, PAGE_SIZE, w), lambda bi, pt, p=p: (pt[bi, p], 0, 0))
    grid_spec = pltpu.PrefetchScalarGridSpec(
        num_scalar_prefetch=1,
        grid=(b,),
        in_specs=[row(D_D), row(QI_W), row(128, OD_KIWI // 128), row(D_D, OD_GD // D_D), row(KV_W), row(KV_W), row(DI_D)]
        + [page(KV_W, p) for p in range(n_pages)] + [page(KV_W, p) for p in range(n_pages)]
        + [page(DI_D, p) for p in range(n_pages)],
        out_specs=pl.BlockSpec((1, tq, D_D), lambda bi, pt: (bi, 0, 0)),
    )
    return pl.pallas_call(
        functools.partial(_dsa_sample_kernel, n_pages=n_pages, n_sel=float(n_sel), tq=tq),
        grid_spec=grid_spec,
        out_shape=jax.ShapeDtypeStruct((b, tq, D_D), F32),
        compiler_params=pltpu.CompilerParams(dimension_semantics=("parallel",)),
        name="dsa_sample",
    )(page_table, q_b, qi_b, proj3d, proj3d, k_b, v_b, ki_b, *([ck] * n_pages), *([cv] * n_pages),
      *([cache_ki] * n_pages))


def _split(x, sizes):
    return jnp.split(x, np.cumsum(sizes)[:-1].tolist(), axis=-1)


def partial_rotary(x, pos):
    rd = x.shape[-1] // ROPE_FRAC
    half = rd // 2
    inv_freq = ROPE_THETA ** (-jnp.arange(half, dtype=jnp.float32) / half)
    ang = pos.astype(jnp.float32)[:, None] * inv_freq[None, :]
    cos = jnp.cos(ang)[:, None, :]
    sin = jnp.sin(ang)[:, None, :]
    xf = x.astype(jnp.float32)
    x1, x2 = xf[..., :half], xf[..., half:rd]
    rot = jnp.concatenate([x1 * cos - x2 * sin, x1 * sin + x2 * cos, xf[..., rd:]], axis=-1)
    return rot.astype(x.dtype)


def wkv_step(S, inp):
    r, w, k, v, kk, a = inp
    sa = jnp.einsum('bhvk,bhk->bhv', S, -kk)
    S = S * w[:, :, None, :] + sa[..., None] * (kk * a)[:, :, None, :] + v[..., None] * k[:, :, None, :]
    return S, jnp.einsum('bhvk,bhk->bhv', S, r)


def rwkv7_branch(pa, gate, shift0, wkv0, mu, w0, w2, a0, a2, k_k, k_a, r_k, ln_g, ln_b):
    f32 = jnp.float32
    b, l, _ = pa.shape
    paf = pa.astype(f32)
    prev = jnp.concatenate([shift0.astype(f32)[:, None], paf[:, :-1]], axis=1)
    xs = paf + (prev - paf) * mu.astype(f32)
    r, k, v, wl, al = _split(xs, [D_A, D_A, D_A, W_LORA, A_LORA])
    w_log = -jax.nn.softplus(-(w0.astype(f32) + jnp.tanh(wl) @ w2.astype(f32))) - 0.5
    decay = jnp.exp(-jnp.exp(w_log))
    a = jax.nn.sigmoid(a0.astype(f32) + al @ a2.astype(f32))
    kk = (k * k_k.astype(f32)).reshape(b, l, H_A, HD_A)
    kk = kk / jnp.maximum(jnp.sqrt(jnp.sum(kk * kk, axis=-1, keepdims=True)), 1e-12)
    k = k * (1.0 + (a - 1.0) * k_a.astype(f32))
    r, decay, k, v, a = [t.reshape(b, l, H_A, HD_A) for t in (r, decay, k, v, a)]
    seq = tuple(jnp.moveaxis(t, 1, 0) for t in (r, decay, k, v, kk, a))
    S_last, out = lax.scan(wkv_step, wkv0.astype(f32), seq)
    out = jnp.moveaxis(out, 0, 1)
    mean = jnp.mean(out, axis=-1, keepdims=True)
    var = jnp.mean(jnp.square(out - mean), axis=-1, keepdims=True)
    out = ((out - mean) * lax.rsqrt(var + RWKV_LN_EPS)).reshape(b, l, D_A)
    out = out * ln_g.astype(f32) + ln_b.astype(f32)
    bonus = jnp.sum(r * k * r_k.astype(f32), axis=-1, keepdims=True) * v
    out = (out + bonus.reshape(b, l, D_A)) * jax.nn.silu(gate.astype(f32))
    return out.astype(pa.dtype), S_last, pa[:, -1]


def segsum(x):
    T = x.shape[-1]
    xr = jnp.broadcast_to(x[..., None], x.shape + (T,))
    xr = jnp.where(jnp.tril(jnp.ones((T, T), bool), -1), xr, 0.0)
    cs = jnp.cumsum(xr, axis=-2)
    return jnp.where(jnp.tril(jnp.ones((T, T), bool), 0), cs, -jnp.inf)


def ssd_chunked(x, a, bm, cm, h0):
    b, l = x.shape[:2]
    cl = math.gcd(l, SSD_CHUNK)
    nc = l // cl
    x = x.reshape(b, nc, cl, G_B, HPG_B, P_B)
    a = a.reshape(b, nc, cl, G_B, HPG_B).transpose(0, 3, 4, 1, 2)
    bm = bm.reshape(b, nc, cl, G_B, N_B)
    cm = cm.reshape(b, nc, cl, G_B, N_B)
    a_cum = jnp.cumsum(a, axis=-1)
    lmat = jnp.exp(segsum(a))
    cb = jnp.einsum('bclgn,bcsgn->bcgls', cm, bm)
    y_diag = jnp.einsum('bcgls,bgrcls,bcsgrp->bclgrp', cb, lmat, x)
    decay_states = jnp.exp(a_cum[..., -1:] - a_cum)
    states = jnp.einsum('bcsgn,bgrcs,bcsgrp->bcgrpn', bm, decay_states, x)
    states = jnp.concatenate([h0.reshape(b, 1, G_B, HPG_B, P_B, N_B), states], axis=1)
    chunk_tot = jnp.pad(a_cum[..., -1], ((0, 0), (0, 0), (0, 0), (1, 0)))
    decay_chunk = jnp.exp(segsum(chunk_tot))
    new_states = jnp.einsum('bgrzc,bcgrpn->bzgrpn', decay_chunk, states)
    states, h_last = new_states[:, :-1], new_states[:, -1]
    y_off = jnp.einsum('bclgn,bcgrpn,bgrcl->bclgrp', cm, states, jnp.exp(a_cum))
    y = (y_diag + y_off).reshape(b, l, H_B, P_B)
    return y, h_last.reshape(b, H_B, P_B, N_B)


def ssd_branch(z, xbc, dt_raw, conv0, ssm0, conv_w, conv_b, dt_bias, a_log, d_skip, gnorm_g):
    f32 = jnp.float32
    b, l, _ = xbc.shape
    ext = jnp.concatenate([conv0.astype(xbc.dtype), xbc], axis=1)
    conv = conv_b.astype(f32) + sum(ext[:, i:i + l].astype(f32) * conv_w[i].astype(f32) for i in range(CONV_W))
    conv_new = ext[:, l:]
    xs, bm, cm = _split(jax.nn.silu(conv), [D_B, G_B * N_B, G_B * N_B])
    xs = xs.reshape(b, l, H_B, P_B)
    bm = bm.reshape(b, l, G_B, N_B)
    cm = cm.reshape(b, l, G_B, N_B)
    dt = jax.nn.softplus(dt_raw.astype(f32) + dt_bias.astype(f32))
    a_cont = -jnp.exp(a_log.astype(f32))
    y, ssm_new = ssd_chunked(xs * dt[..., None], dt * a_cont, bm, cm, ssm0.astype(f32))
    y = y + xs * d_skip.astype(f32)[:, None]
    y = y.reshape(b, l, D_B) * jax.nn.silu(z.astype(f32))
    yg = y.reshape(b, l, G_B, D_B // G_B)
    yg = yg * lax.rsqrt(jnp.mean(yg * yg, axis=-1, keepdims=True) + NORM_EPS)
    out = yg.reshape(b, l, D_B) * gnorm_g.astype(f32)
    return out.astype(z.dtype), ssm_new, conv_new


def even_mixer(x, g, lv, shift0, wkv0, conv0, ssm0, w_in, w_out, mu, w0, w2, a0, a2, k_k, k_a, r_k,
               ln_g, ln_b, conv_w, conv_b, dt_bias, a_log, d_skip, gnorm_g):
    b, l, d = x.shape
    pa_w, ga_w, z_w, xbc_w, dt_w = _split(w_in, [A_SHIFT, D_A, D_B, CONV_DIM, H_B])
    w_ev = jnp.concatenate([ga_w, z_w, xbc_w, pa_w, dt_w, jnp.zeros((d, EV_N - IN_E), w_in.dtype)], axis=1)
    proj = norm_matmul(x.reshape(b * l, d), g, w_ev.astype(BF16), tn=256).reshape(b, l, EV_N)
    out_a, wkv_new, shift_new = rwkv7_pallas(proj, shift0, wkv0, mu, w0, w2, a0, a2, k_k, k_a, r_k, ln_g, ln_b, lv)
    out_b, ssm_new, conv_new = ssd_pallas(proj, lv, conv0, ssm0, conv_w, conv_b, dt_bias, a_log, d_skip, gnorm_g)
    x_new = matmul_residual(out_a.reshape(b * l, D_A), out_b.reshape(b * l, D_B), w_out.astype(BF16),
                            x.reshape(b * l, d)).reshape(b, l, d)
    return x_new, (wkv_new, shift_new, ssm_new, conv_new)


def s5_combine(e1, e2):
    a1, b1 = e1
    a2, b2 = e2
    return a1 * a2, a2 * b1 + b2


def s5_branch(u, gate, h0_re, h0_im, lam_re, lam_im, log_dt, b_re, b_im, c_re, c_im, d_skip, glu_w, glu_b):
    f32 = jnp.float32
    b, l, _ = u.shape
    lam = lax.complex(lam_re.astype(f32), lam_im.astype(f32))
    delta = jnp.exp(log_dt.astype(f32))[:, None]
    lam_bar = jnp.exp(lam * delta)
    b_bar = ((lam_bar - 1.0) / lam)[..., None] * lax.complex(b_re.astype(f32), b_im.astype(f32))
    c_cplx = lax.complex(c_re.astype(f32), c_im.astype(f32))
    uf = u.astype(f32)
    bu = jnp.einsum('gpm,blgm->blgp', b_bar, uf.reshape(b, l, G_C, CH_C).astype(jnp.complex64))
    h0 = lax.complex(h0_re.astype(f32), h0_im.astype(f32))
    bu = bu.at[:, 0].add(lam_bar[None] * h0)
    _, hs = lax.associative_scan(s5_combine, (jnp.broadcast_to(lam_bar, bu.shape), bu), axis=1)
    y = jnp.real(jnp.einsum('gmp,blgp->blgm', c_cplx, hs)).reshape(b, l, D_C) + d_skip.astype(f32) * uf
    y = jax.nn.gelu(y)
    y = y * jax.nn.sigmoid(y @ glu_w.astype(f32) + glu_b.astype(f32))
    y = y * jax.nn.silu(gate.astype(f32))
    h_last = hs[:, -1]
    return y.astype(u.dtype), jnp.real(h_last), jnp.imag(h_last)


def dsa_select_attend(q, qi, wi, tpos, ki_all, gather_kv, n_sel):
    f32 = jnp.float32
    b, t = q.shape[:2]
    n_keys = ki_all.shape[1]
    dots = jnp.einsum('bthd,bsd->bths', qi.astype(f32), ki_all.astype(f32))
    score = jnp.einsum('bth,bths->bts', wi.astype(f32), jax.nn.relu(dots)) * IDX_SCALE
    admissible = jnp.arange(n_keys)[None, :] <= tpos[:, None]
    score = jnp.where(admissible[None], score, -jnp.inf)
    top_val, top_idx = lax.top_k(score, n_sel)
    valid = jnp.isfinite(top_val)
    k_sel, v_sel = gather_kv(top_idx)
    logits = jnp.einsum('btjgd,btnjd->btjgn', q.astype(f32), k_sel.astype(f32)) * (DH_D ** -0.5)
    logits = jnp.where(valid[:, :, None, None, :], logits, -jnp.inf)
    probs = jax.nn.softmax(logits, axis=-1)
    out = jnp.einsum('btjgn,btnjd->btjgd', probs, v_sel.astype(f32))
    return out.reshape(b, t, D_D)


def dsa_prompt(q, k, v, qi, ki, wi, pos):
    b, l = q.shape[:2]
    n_sel = min(TOPK_MAX, l // 4)
    qb = math.gcd(l, QBLK)
    nb = l // qb
    bidx = jnp.arange(b)[:, None, None]

    def gather_kv(idx):
        return k[bidx, idx], v[bidx, idx]

    def blk(t):
        return jnp.moveaxis(t.reshape((b, nb, qb) + t.shape[2:]), 1, 0)

    def one_block(args):
        qq, qqi, wwi, pp = args
        return dsa_select_attend(qq, qqi, wwi, pp, ki, gather_kv, n_sel)

    out = lax.map(one_block, (blk(q), blk(qi), blk(wi), pos.reshape(nb, qb)))
    return jnp.moveaxis(out, 0, 1).reshape(b, l, D_D)


def make_dsa_sample(cache_k, cache_v, cache_ki, page_table):
    past_len = page_table.shape[1] * PAGE_SIZE

    def attend(q, k, v, qi, ki, wi, pos):
        b, t = q.shape[:2]
        n_sel = min(TOPK_MAX, (past_len + t) // 4)
        ki_past = cache_ki[page_table].reshape(b, past_len, DI_D)
        ki_all = jnp.concatenate([ki_past.astype(ki.dtype), ki], axis=1)
        k_flat = cache_k.reshape(-1, KVH_D, DH_D)
        v_flat = cache_v.reshape(-1, KVH_D, DH_D)
        bidx = jnp.arange(b)[:, None, None]

        def gather_kv(idx):
            is_past = (idx < past_len)[..., None, None]
            ic = jnp.minimum(idx, past_len - 1)
            phys = page_table[bidx, ic // PAGE_SIZE] * PAGE_SIZE + ic % PAGE_SIZE
            inew = jnp.clip(idx - past_len, 0, t - 1)
            k_sel = jnp.where(is_past, k_flat[phys].astype(k.dtype), k[bidx, inew])
            v_sel = jnp.where(is_past, v_flat[phys].astype(v.dtype), v[bidx, inew])
            return k_sel, v_sel

        return dsa_select_attend(q, qi, wi, pos, ki_all, gather_kv, n_sel)

    return attend


def odd_mixer(x, g, lv, pos, c_re0, c_im0, attend, w_in, w_out, lam_re, lam_im, log_dt, b_re, b_im,
              c_re, c_im, d_skip, glu_w, glu_b):
    b, l, d = x.shape
    u_w, gc_w, q_w, k_w, v_w, qi_w, ki_w, wi_w, gd_w = _split(
        w_in, [D_C, D_C, D_D, KV_W, KV_W, QI_W, DI_D, HI_D, D_D])
    w_od = jnp.concatenate([u_w, gc_w, q_w, gd_w, k_w, v_w, qi_w, ki_w, wi_w,
                            jnp.zeros((d, OD_N - IN_O), w_in.dtype)], axis=1)
    proj = norm_matmul(x.reshape(b * l, d), g, w_od.astype(BF16), tn=384).reshape(b, l, OD_N)
    out_c, re_last, im_last = s5_pallas(proj, lv, c_re0, c_im0, lam_re, lam_im, log_dt, b_re, b_im,
                                        c_re, c_im, d_skip, glu_w, glu_b)
    q_b, qi_b, k_f, v_f, ki, k_b, v_b, ki_b = rope_pallas(proj, pos)
    if attend is None:
        out_d = dsa_prompt_pallas(proj, q_b, qi_b, k_b, v_b, ki_b)
    else:
        out_d = dsa_sample_pallas(proj, q_b, qi_b, k_b, v_b, ki_b, *attend, lv)
    k = k_f[:, :lv].reshape(b, lv, KVH_D, DH_D)
    v = v_f[:, :lv].reshape(b, lv, KVH_D, DH_D)
    ki = ki[:, :lv]
    x_new = matmul_residual(out_c.reshape(b * l, D_C), out_d.reshape(b * l, D_D), w_out.astype(BF16),
                            x.reshape(b * l, d)).reshape(b, l, d)
    return x_new, (re_last, im_last, k, v, ki)


def kernel(x_prompt, x_sample, state_a_wkv, state_a_shift, state_b_ssm, state_b_conv, state_c_re, state_c_im, cache_d_k, cache_d_v, cache_d_kidx, page_table, norm_g, final_norm_g, w_in_e, w_out_e, rwkv_mu, rwkv_w0, rwkv_w2, rwkv_a0, rwkv_a2, rwkv_kk, rwkv_ka, rwkv_rk, rwkv_ln_g, rwkv_ln_b, ssd_conv_w, ssd_conv_b, ssd_dt_bias, ssd_a_log, ssd_d, ssd_norm_g, w_in_o, w_out_o, s5_lam_re, s5_lam_im, s5_log_dt, s5_b_re, s5_b_im, s5_c_re, s5_c_im, s5_d, s5_glu_w, s5_glu_b):
    f32 = jnp.float32
    bp, lp = x_prompt.shape[:2]
    bs, ls = x_sample.shape[:2]
    pos_p = jnp.arange(lp)
    ls_pad = _round_up(ls, 8)
    pos_s = PAST_LEN + jnp.arange(ls_pad)
    xp, xs = x_prompt, jnp.pad(x_sample, ((0, 0), (0, ls_pad - ls), (0, 0)))
    even_p, even_s, odd_p, odd_s = [], [], [], []
    for i in range(DEPTH):
        j = i // 2
        if i % 2 == 0:
            pe = (w_in_e[j], w_out_e[j], rwkv_mu[j], rwkv_w0[j], rwkv_w2[j], rwkv_a0[j], rwkv_a2[j],
                  rwkv_kk[j], rwkv_ka[j], rwkv_rk[j], rwkv_ln_g[j], rwkv_ln_b[j], ssd_conv_w[j],
                  ssd_conv_b[j], ssd_dt_bias[j], ssd_a_log[j], ssd_d[j], ssd_norm_g[j])
            xp, st_p = even_mixer(xp, norm_g[i], lp, jnp.zeros((bp, A_SHIFT), f32), jnp.zeros((bp, H_A, HD_A, HD_A), f32),
                                  jnp.zeros((bp, CONV_W - 1, CONV_DIM), f32),
                                  jnp.zeros((bp, H_B, P_B, N_B), f32), *pe)
            xs, st_s = even_mixer(xs, norm_g[i], ls, state_a_shift[j], state_a_wkv[j], state_b_conv[j], state_b_ssm[j], *pe)
            even_p.append(st_p)
            even_s.append(st_s)
        else:
            po = (w_in_o[j], w_out_o[j], s5_lam_re[j], s5_lam_im[j], s5_log_dt[j], s5_b_re[j], s5_b_im[j],
                  s5_c_re[j], s5_c_im[j], s5_d[j], s5_glu_w[j], s5_glu_b[j])
            zc = jnp.zeros((bp, G_C, P_C), f32)
            xp, st_p = odd_mixer(xp, norm_g[i], lp, pos_p, zc, zc, None, *po)
            n_pool = cache_d_k.shape[1]
            attend_s = (cache_d_k.reshape(-1, PAGE_SIZE, KV_W), cache_d_v.reshape(-1, PAGE_SIZE, KV_W),
                        cache_d_kidx.reshape(-1, PAGE_SIZE, DI_D), page_table + j * n_pool)
            xs, st_s = odd_mixer(xs, norm_g[i], ls, pos_s, state_c_re[j], state_c_im[j], attend_s, *po)
            odd_p.append(st_p)
            odd_s.append(st_s)
    y_prompt = rmsnorm_rows(xp.reshape(bp * lp, D_MODEL), final_norm_g).reshape(bp, lp, D_MODEL)
    y_sample = rmsnorm_rows(xs.reshape(bs * ls_pad, D_MODEL), final_norm_g).reshape(bs, ls_pad, D_MODEL)[:, :ls]
    new_a_wkv_p, new_a_shift_p, new_b_ssm_p, new_b_conv_p = [jnp.stack(t) for t in zip(*even_p)]
    new_a_wkv_s, new_a_shift_s, new_b_ssm_s, new_b_conv_s = [jnp.stack(t) for t in zip(*even_s)]
    new_c_re_p, new_c_im_p, new_d_k_p, new_d_v_p, new_d_kidx_p = [jnp.stack(t) for t in zip(*odd_p)]
    new_c_re_s, new_c_im_s, new_d_k_s, new_d_v_s, new_d_kidx_s = [jnp.stack(t) for t in zip(*odd_s)]
    return (y_prompt, y_sample,
            new_a_wkv_p, new_a_shift_p, new_b_ssm_p, new_b_conv_p,
            new_c_re_p, new_c_im_p, new_d_k_p, new_d_v_p, new_d_kidx_p,
            new_a_wkv_s, new_a_shift_s, new_b_ssm_s, new_b_conv_s,
            new_c_re_s, new_c_im_s, new_d_k_s, new_d_v_s, new_d_kidx_s)
```

```python
import functools
import math

import jax
import jax.numpy as jnp
import numpy as np
from jax import lax
from jax.experimental import pallas as pl
from jax.experimental.pallas import tpu as pltpu

D_MODEL = 1024
DEPTH = 4
PAST_LEN = 2048
PAGE_SIZE = 128
NORM_EPS = 1e-6

D_A = D_MODEL
HD_A = 64
H_A = D_A // HD_A
W_LORA = 64
A_LORA = 64
A_SHIFT = 3 * D_A + W_LORA + A_LORA
RWKV_LN_EPS = 64e-5
D_B = D_MODEL
P_B = 64
H_B = D_B // P_B
N_B = 128
G_B = 4
HPG_B = H_B // G_B
CONV_W = 4
CONV_DIM = D_B + 2 * G_B * N_B
SSD_CHUNK = 128
D_C = D_MODEL // 2
CH_C = 16
G_C = D_C // CH_C
P_C = 64
H_D = 8
DH_D = 128
KVH_D = 2
QPK_D = H_D // KVH_D
D_D = H_D * DH_D
HI_D = 8
DI_D = 64
IDX_SCALE = (DI_D ** -0.5) * (HI_D ** -0.5)
TOPK_MAX = 256
QBLK = 128
ROPE_THETA = 500000.0
ROPE_FRAC = 4

IN_E = A_SHIFT + D_A + D_B + CONV_DIM + H_B
OUT_E = D_A + D_B
IN_O = 2 * D_C + D_D + 2 * KVH_D * DH_D + HI_D * DI_D + DI_D + HI_D + D_D
OUT_O = D_C + D_D

F32 = jnp.float32
BF16 = jnp.bfloat16

TILE_M = 512
TILE_N = 512
PROJ_TILE_M = 1024


def _round_up(n, m):
    return (n + m - 1) // m * m


def _norm_matmul_kernel(x_ref, g_ref, w_ref, o_ref, h_ref):
    @pl.when(pl.program_id(1) == 0)
    def _():
        x = x_ref[...]
        ms = jnp.mean(x * x, axis=-1, keepdims=True)
        h_ref[...] = (x * lax.rsqrt(ms + NORM_EPS) * g_ref[...]).astype(BF16)

    o_ref[...] = jnp.dot(h_ref[...], w_ref[...], preferred_element_type=F32)


def norm_matmul(x2d, g, w_bf16, tn=TILE_N):
    m, d = x2d.shape
    n = w_bf16.shape[1]
    tm = min(PROJ_TILE_M, m)
    assert m % tm == 0 and n % tn == 0
    return pl.pallas_call(
        _norm_matmul_kernel,
        grid=(m // tm, n // tn),
        in_specs=[
            pl.BlockSpec((tm, d), lambda i, j: (i, 0)),
            pl.BlockSpec((1, d), lambda i, j: (0, 0)),
            pl.BlockSpec((d, tn), lambda i, j: (0, j)),
        ],
        out_specs=pl.BlockSpec((tm, tn), lambda i, j: (i, j)),
        out_shape=jax.ShapeDtypeStruct((m, n), F32),
        scratch_shapes=[pltpu.VMEM((tm, d), BF16)],
        compiler_params=pltpu.CompilerParams(dimension_semantics=("parallel", "arbitrary")),
        name="norm_matmul",
    )(x2d, g.reshape(1, d), w_bf16)


def _matmul_res_kernel(a1_ref, a2_ref, w1_ref, w2_ref, r_ref, o_ref):
    o_ref[...] = (r_ref[...] + jnp.dot(a1_ref[...].astype(BF16), w1_ref[...], preferred_element_type=F32)
                  + jnp.dot(a2_ref[...].astype(BF16), w2_ref[...], preferred_element_type=F32))


def matmul_residual(a1, a2, w_bf16, res2d):
    m, k1 = a1.shape
    k2 = a2.shape[1]
    n = w_bf16.shape[1]
    tm = min(TILE_M, m)
    assert m % tm == 0 and w_bf16.shape[0] == k1 + k2
    return pl.pallas_call(
        _matmul_res_kernel,
        grid=(m // tm,),
        in_specs=[
            pl.BlockSpec((tm, k1), lambda i: (i, 0)),
            pl.BlockSpec((tm, k2), lambda i: (i, 0)),
            pl.BlockSpec((k1, n), lambda i: (0, 0)),
            pl.BlockSpec((k2, n), lambda i: (0, 0)),
            pl.BlockSpec((tm, n), lambda i: (i, 0)),
        ],
        out_specs=pl.BlockSpec((tm, n), lambda i: (i, 0)),
        out_shape=jax.ShapeDtypeStruct((m, n), F32),
        compiler_params=pltpu.CompilerParams(dimension_semantics=("parallel",)),
        name="matmul_residual",
    )(a1, a2, w_bf16[:k1], w_bf16[k1:], res2d)


def _rmsnorm_kernel(x_ref, g_ref, o_ref):
    x = x_ref[...]
    ms = jnp.mean(x * x, axis=-1, keepdims=True)
    o_ref[...] = x * lax.rsqrt(ms + NORM_EPS) * g_ref[...]


def rmsnorm_rows(x2d, g):
    m, d = x2d.shape
    tm = min(TILE_M, m)
    return pl.pallas_call(
        _rmsnorm_kernel,
        grid=(m // tm,),
        in_specs=[pl.BlockSpec((tm, d), lambda i: (i, 0)), pl.BlockSpec((1, d), lambda i: (0, 0))],
        out_specs=pl.BlockSpec((tm, d), lambda i: (i, 0)),
        out_shape=jax.ShapeDtypeStruct((m, d), F32),
        compiler_params=pltpu.CompilerParams(dimension_semantics=("parallel",)),
        name="final_rmsnorm",
    )(x2d, g.reshape(1, d))


def _mm(a, b):
    return jnp.dot(a.astype(BF16), b.astype(BF16), preferred_element_type=F32)


def _mm_nt(a, b):
    return lax.dot_general(a.astype(BF16), b.astype(BF16), (((1,), (1,)), ((), ())), preferred_element_type=F32)


def _mm_tn(a, b):
    return lax.dot_general(a.astype(BF16), b.astype(BF16), (((0,), (0,)), ((), ())), preferred_element_type=F32)


def _hi_lo(x):
    hi = x.astype(BF16)
    lo = (x - hi.astype(F32)).astype(BF16)
    return hi, lo


def _mm_exact_lhs(a_bf16, x):
    hi, lo = _hi_lo(x)
    return (jnp.dot(a_bf16, hi, preferred_element_type=F32) + jnp.dot(a_bf16, lo, preferred_element_type=F32))


def _mm_exact_rhs(x, b_bf16):
    hi, lo = _hi_lo(x)
    return (jnp.dot(hi, b_bf16, preferred_element_type=F32) + jnp.dot(lo, b_bf16, preferred_element_type=F32))


def _softplus(x):
    return jnp.maximum(x, 0.0) + jnp.log1p(jnp.exp(-jnp.abs(x)))


def _sigmoid(x):
    return 1.0 / (1.0 + jnp.exp(-x))


def _unit_lower_inverses(a_list, row, col, n):
    eye = jnp.where(row == col, 1.0, 0.0).astype(F32)
    ts = [eye for _ in a_list]
    m = 1
    while m < n:
        in_pair = (row // (2 * m)) == (col // (2 * m))
        lvl = in_pair & ((row % (2 * m)) >= m) & ((col % (2 * m)) < m)
        ls = [jnp.where(lvl, a, 0.0) for a in a_list]
        if m == 1:
            ts = [t - l for t, l in zip(ts, ls)]
        else:
            tl = [_mm(t, l) for t, l in zip(ts, ls)]
            ts = [t - _mm(x, t) for t, x in zip(ts, tl)]
        m *= 2
    return ts


EV_GA, EV_Z, EV_XBC, EV_R, EV_K, EV_V, EV_WA, EV_DT = 0, 1024, 2048, 4096, 5120, 6144, 7168, 7296
EV_N = 7680
RWKV_CHUNK = 64


def _rwkv_kernel(r_ref, k_ref, v_ref, wa_ref, g_ref, sh0_ref, s0_ref, mu_ref, pv_ref, w2_ref, a2_ref, bones_ref,
                 o_ref, snew_ref, shnew_ref, st_ref, carry_ref, *, chunk, n_valid):
    C = chunk
    c = pl.program_id(1)

    @pl.when(c == 0)
    def _():
        carry_ref[...] = sh0_ref[0]
        st_ref[...] = s0_ref[0]

    mu = mu_ref[...]
    pv = pv_ref[...]
    w0, a0, k_k, k_a, r_k, ln_g, ln_b = (pv[i:i + 1] for i in range(7))
    carry = carry_ref[...]
    row1 = lax.broadcasted_iota(jnp.int32, (C, 1), 0)

    def tok_shift(x, lo, hi):
        prev = jnp.where(row1 == 0, carry[:, lo:hi], pltpu.roll(x, 1, 0))
        return x + (prev - x) * mu[:, lo:hi]

    pr, pk, pvv, pwa = r_ref[0], k_ref[0], v_ref[0], wa_ref[0]
    r = tok_shift(pr, 0, D_A)
    k = tok_shift(pk, D_A, 2 * D_A)
    v = tok_shift(pvv, 2 * D_A, 3 * D_A)
    wa = tok_shift(pwa, 3 * D_A, A_SHIFT)
    last = n_valid - 1
    new_carry = jnp.concatenate([pr[last:last + 1], pk[last:last + 1], pvv[last:last + 1], pwa[last:last + 1]], axis=1)
    carry_ref[...] = new_carry
    shnew_ref[0] = new_carry

    wl, al = wa[:, :W_LORA], wa[:, W_LORA:]
    w_log = -_softplus(-(w0 + _mm(jnp.tanh(wl), w2_ref[...]))) - 0.5
    logw = -jnp.exp(w_log)
    a = _sigmoid(a0 + _mm(al, a2_ref[...]))
    bones = bones_ref[...]
    kk = k * k_k
    kk = kk / jnp.maximum(jnp.sqrt(_mm_exact_rhs(kk * kk, bones)), 1e-12)
    kp = k * (1.0 + (a - 1.0) * k_a)
    bonus = _mm_exact_rhs(r * kp * r_k, bones) * v
    if n_valid < C:
        ok = row1 < n_valid
        logw = jnp.where(ok, logw, 0.0)
        kk = jnp.where(ok, kk, 0.0)
        kp = jnp.where(ok, kp, 0.0)

    row = lax.broadcasted_iota(jnp.int32, (C, C), 0)
    col = lax.broadcasted_iota(jnp.int32, (C, C), 1)
    incl = col <= row
    strict = col < row
    cum = _mm_exact_lhs(incl.astype(BF16), logw)
    cum_last = cum[C - 1:C]
    eneg = jnp.exp(-cum)
    alpha = kk * jnp.exp(cum - logw)
    ka = kk * a
    beta = ka * eneg
    kappa = kp * eneg
    rho = r * jnp.exp(cum)
    dec_end = jnp.exp(cum_last - cum)
    kappa_e = kp * dec_end
    beta_e = ka * dec_end
    wc = jnp.exp(cum_last)

    heads = range(H_A)
    sls = [slice(h * HD_A, (h + 1) * HD_A) for h in heads]
    s0s = [st_ref[h] for h in heads]
    bks = [jnp.concatenate([beta[:, sl], kappa[:, sl]], axis=0) for sl in sls]
    g_as = [_mm_nt(alpha[:, sl], bk) for sl, bk in zip(sls, bks)]
    g_rs = [_mm_nt(rho[:, sl], bk) for sl, bk in zip(sls, bks)]
    t_invs = _unit_lower_inverses([jnp.where(strict, g[:, :C], 0.0) for g in g_as], row, col, C)
    rhss = [_mm_nt(alpha[:, sl], s0) + _mm(jnp.where(strict, g[:, C:], 0.0), v[:, sl])
            for sl, s0, g in zip(sls, s0s, g_as)]
    us = [_mm(t, x) for t, x in zip(t_invs, rhss)]
    outs = [_mm_nt(rho[:, sl], s0) + _mm(jnp.where(incl, g[:, C:], 0.0), v[:, sl])
            - _mm(jnp.where(incl, g[:, :C], 0.0), u) for sl, s0, g, u in zip(sls, s0s, g_rs, us)]
    for h, sl, s0, u in zip(heads, sls, s0s, us):
        x = jnp.concatenate([v[:, sl], -u], axis=0)
        y = jnp.concatenate([kappa_e[:, sl], beta_e[:, sl]], axis=0)
        st_ref[h] = s0 * wc[:, sl] + _mm_tn(x, y)
    out = jnp.concatenate(outs, axis=1)

    mean = _mm_exact_rhs(out, bones) * (1.0 / HD_A)
    d = out - mean
    var = _mm_exact_rhs(d * d, bones) * (1.0 / HD_A)
    y = d * lax.rsqrt(var + RWKV_LN_EPS) * ln_g + ln_b
    gate = g_ref[0]
    o_ref[0] = (y + bonus) * (gate * _sigmoid(gate))
    snew_ref[0] = st_ref[...]


def rwkv7_pallas(proj3d, shift0, wkv0, mu, w0, w2, a0, a2, k_k, k_a, r_k, ln_g, ln_b, n_valid_tokens):
    b, l, _ = proj3d.shape
    chunk = min(RWKV_CHUNK, l)
    n_chunks = l // chunk
    assert l % chunk == 0 and chunk % 8 == 0
    n_valid = n_valid_tokens - (n_chunks - 1) * chunk
    assert 0 < n_valid <= chunk and (n_valid == chunk or n_chunks == 1)
    pvec = jnp.stack([w0, a0, k_k, k_a, r_k.reshape(D_A), ln_g, ln_b, jnp.zeros_like(w0)])
    hid = jnp.arange(D_A) // HD_A
    bones = (hid[:, None] == hid[None, :]).astype(BF16)
    blk = lambda w, j: pl.BlockSpec((1, chunk, w), lambda bi, ci: (bi, ci, j))
    const = lambda shape: pl.BlockSpec(shape, lambda bi, ci: (0,) * len(shape))
    out, wkv_new, shift_new = pl.pallas_call(
        functools.partial(_rwkv_kernel, chunk=chunk, n_valid=n_valid),
        grid=(b, n_chunks),
        in_specs=[
            blk(D_A, EV_R // D_A), blk(D_A, EV_K // D_A), blk(D_A, EV_V // D_A), blk(128, EV_WA // 128),
            blk(D_A, EV_GA // D_A),
            pl.BlockSpec((1, 1, A_SHIFT), lambda bi, ci: (bi, 0, 0)),
            pl.BlockSpec((1, H_A, HD_A, HD_A), lambda bi, ci: (bi, 0, 0, 0)),
            const((1, A_SHIFT)), const((8, D_A)), const((W_LORA, D_A)), const((A_LORA, D_A)), const((D_A, D_A)),
        ],
        out_specs=[
            pl.BlockSpec((1, chunk, D_A), lambda bi, ci: (bi, ci, 0)),
            pl.BlockSpec((1, H_A, HD_A, HD_A), lambda bi, ci: (bi, 0, 0, 0)),
            pl.BlockSpec((1, 1, A_SHIFT), lambda bi, ci: (bi, 0, 0)),
        ],
        out_shape=[
            jax.ShapeDtypeStruct((b, l, D_A), F32),
            jax.ShapeDtypeStruct((b, H_A, HD_A, HD_A), F32),
            jax.ShapeDtypeStruct((b, 1, A_SHIFT), F32),
        ],
        scratch_shapes=[pltpu.VMEM((H_A, HD_A, HD_A), F32), pltpu.VMEM((1, A_SHIFT), F32)],
        compiler_params=pltpu.CompilerParams(dimension_semantics=("parallel", "arbitrary")),
        name="rwkv7_chunked",
    )(proj3d, proj3d, proj3d, proj3d, proj3d, shift0.reshape(b, 1, A_SHIFT), wkv0,
      mu.reshape(1, A_SHIFT), pvec, w2.astype(BF16), a2.astype(BF16), bones)
    return out, wkv_new, shift_new.reshape(b, A_SHIFT)


def rwkv7_pallas_from_parts(pa, ga, shift0, wkv0, mu, w0, w2, a0, a2, k_k, k_a, r_k, ln_g, ln_b):
    b, l, _ = pa.shape
    lp = _round_up(l, 8)
    proj = jnp.zeros((b, lp, EV_N), F32)
    proj = proj.at[:, :l, EV_GA:EV_GA + D_A].set(ga)
    proj = proj.at[:, :l, EV_R:EV_R + 3 * D_A].set(pa[..., :3 * D_A])
    proj = proj.at[:, :l, EV_WA:EV_WA + 128].set(pa[..., 3 * D_A:])
    out, wkv_new, shift_new = rwkv7_pallas(proj, shift0, wkv0, mu, w0, w2, a0, a2, k_k, k_a, r_k, ln_g, ln_b, l)
    return out[:, :l], wkv_new, shift_new


OD_U, OD_GC, OD_Q, OD_GD, OD_K, OD_V, OD_QI, OD_KIWI = 0, 512, 1024, 2048, 3072, 3328, 3584, 4096
OD_N = 4608
KV_W = KVH_D * DH_D
QI_W = HI_D * DI_D
INT_MIN = -2 ** 31


def _rope_tables(pos, head_dim):
    rd = head_dim // ROPE_FRAC
    half = rd // 2
    inv_freq = ROPE_THETA ** (-jnp.arange(half, dtype=F32) / half)
    ang = pos.astype(F32)[:, None] * inv_freq[None, :]
    cos, sin = jnp.cos(ang), jnp.sin(ang)
    t = pos.shape[0]
    ones = jnp.ones((t, head_dim - rd), F32)
    cos_h = jnp.concatenate([cos, cos, ones], axis=1)
    sin_h = jnp.concatenate([-sin, sin, 0.0 * ones], axis=1)
    reps = 128 // head_dim
    return jnp.tile(cos_h, (1, reps)), jnp.tile(sin_h, (1, reps))


def _rotate(x, cos_t, sin_t, head_dim):
    w = x.shape[1]
    half = head_dim // ROPE_FRAC // 2
    lane = lax.broadcasted_iota(jnp.int32, x.shape, 1) % head_dim
    if w >= 128:
        cos_f = jnp.tile(cos_t, (1, w // 128))
        sin_f = jnp.tile(sin_t, (1, w // 128))
        partner = jnp.where(lane < half, pltpu.roll(x, w - half, 1), pltpu.roll(x, half, 1))
    else:
        cos_f, sin_f = cos_t[:, :w], sin_t[:, :w]
        partner = jnp.where(lane < half, jnp.concatenate([x[:, half:], x[:, :half]], axis=1),
                            jnp.concatenate([x[:, w - half:], x[:, :w - half]], axis=1))
    return x * cos_f + partner * sin_f


def _rope_kernel(q_ref, k_ref, v_ref, qi_ref, kiwi_ref, c128_ref, s128_ref, c64_ref, s64_ref,
                 qo_ref, qio_ref, ko_ref, vo_ref, kio_ref, kb_ref, vb_ref, kib_ref, *, v_transposed):
    c128, s128, c64, s64 = c128_ref[...], s128_ref[...], c64_ref[...], s64_ref[...]
    qo_ref[0] = _rotate(q_ref[0], c128, s128, DH_D).astype(BF16)
    qio_ref[0] = _rotate(qi_ref[0], c64, s64, DI_D).astype(BF16)
    k_rot = _rotate(k_ref[0], c128, s128, DH_D)
    ko_ref[0] = k_rot
    kb_ref[0] = k_rot.astype(BF16)
    v = v_ref[0]
    vo_ref[0] = v
    vb_ref[0] = (v.T if v_transposed else v).astype(BF16)
    ki_rot = _rotate(kiwi_ref[0][:, :DI_D], c64, s64, DI_D)
    kio_ref[0] = ki_rot
    kib_ref[0] = ki_rot.astype(BF16)


def rope_pallas(proj3d, pos, v_transposed=False):
    b, l, _ = proj3d.shape
    tr = min(512, l)
    assert l % tr == 0
    c128, s128 = _rope_tables(pos, DH_D)
    c64, s64 = _rope_tables(pos, DI_D)
    blk = lambda w, j: pl.BlockSpec((1, tr, w), lambda bi, ti: (bi, ti, j))
    tab = pl.BlockSpec((tr, 128), lambda bi, ti: (ti, 0))
    oblk = lambda w: pl.BlockSpec((1, tr, w), lambda bi, ti: (bi, ti, 0))
    shp = lambda w, dt: jax.ShapeDtypeStruct((b, l, w), dt)
    vb_spec, vb_shape = oblk(KV_W), shp(KV_W, BF16)
    if v_transposed:
        vb_spec = pl.BlockSpec((1, KV_W, tr), lambda bi, ti: (bi, 0, ti))
        vb_shape = jax.ShapeDtypeStruct((b, KV_W, l), BF16)
    return pl.pallas_call(
        functools.partial(_rope_kernel, v_transposed=v_transposed),
        grid=(b, l // tr),
        in_specs=[blk(D_D, OD_Q // D_D), blk(KV_W, OD_K // KV_W), blk(KV_W, OD_V // KV_W), blk(QI_W, OD_QI // QI_W),
                  blk(128, OD_KIWI // 128), tab, tab, tab, tab],
        out_specs=[oblk(D_D), oblk(QI_W), oblk(KV_W), oblk(KV_W), oblk(DI_D), oblk(KV_W), vb_spec, oblk(DI_D)],
        out_shape=[shp(D_D, BF16), shp(QI_W, BF16), shp(KV_W, F32), shp(KV_W, F32), shp(DI_D, F32),
                   shp(KV_W, BF16), vb_shape, shp(DI_D, BF16)],
        compiler_params=pltpu.CompilerParams(dimension_semantics=("parallel", "parallel")),
        name="dsa_rope",
    )(proj3d, proj3d, proj3d, proj3d, proj3d, c128, s128, c64, s64)


DSA_TQ = 128
DSA_TK = 512
NEG_BIG = -1e30
DSA_ONES = 16


def _sortable_key(s):
    bits = pltpu.bitcast(s + 0.0, jnp.int32)
    return bits ^ ((bits >> 31) & 0x7FFFFFFF)


def _dsa_prompt_kernel(q_ref, qi_ref, kiwi_ref, gd_ref, k_ref, vt_ref, ki_ref, o_ref, key_ref, *, n_sel):
    tq, tk = DSA_TQ, DSA_TK
    t0 = pl.program_id(1) * tq
    n_kt = (t0 + tq + tk - 1) // tk
    qi = qi_ref[0]
    wi_t = kiwi_ref[0].T[DI_D:DI_D + HI_D, :]
    pos_q = t0 + lax.broadcasted_iota(jnp.int32, (1, tq), 1)
    row_k = lax.broadcasted_iota(jnp.int32, (tk, tq), 0)

    def key_slice(kt):
        return pl.ds(pl.multiple_of(kt * tk, tk), tk)

    def fold_rows(x):
        x = x.reshape(tk // 8, 8, tq)
        while x.shape[0] > 1:
            half = x.shape[0] // 2
            x = x[:half] + x[half:]
        return x[0]

    def score_body(kt, carry):
        ks = key_slice(kt)
        kit = ki_ref[0, ks, :]
        acc = jnp.zeros((tk, tq), F32)
        for h in range(HI_D):
            d = lax.dot_general(kit, qi[:, h * DI_D:(h + 1) * DI_D], (((1,), (1,)), ((), ())),
                                preferred_element_type=F32)
            acc = acc + wi_t[h:h + 1, :] * jnp.maximum(d, 0.0)
        key = _sortable_key(acc * IDX_SCALE)
        key_ref[ks, :] = jnp.where(kt * tk + row_k <= pos_q, key, INT_MIN)
        return carry

    lax.fori_loop(0, n_kt, score_body, 0)

    def count(pred_fn):
        def body(kt, c):
            return c + fold_rows(jnp.where(pred_fn(key_ref[key_slice(kt), :]), 1.0, 0.0))
        c = lax.fori_loop(0, n_kt, body, jnp.zeros((8, tq), F32))
        return jnp.sum(c, axis=0, keepdims=True)

    def bit_body(i, ans):
        cand = ans | lax.shift_left(jnp.int32(1), 31 - i)
        scand = cand ^ INT_MIN
        cnt = count(lambda key: key >= scand)
        return jnp.where(cnt >= n_sel, cand, ans)

    thr = lax.fori_loop(0, 32, bit_body, jnp.zeros((1, tq), jnp.int32)) ^ INT_MIN
    need = n_sel - count(lambda key: key > thr)

    tri = (lax.broadcasted_iota(jnp.int32, (tk, tk), 1) <= lax.broadcasted_iota(jnp.int32, (tk, tk), 0)).astype(BF16)

    def sel_body(kt, tie_seen):
        ks = key_slice(kt)
        key = key_ref[ks, :]
        tie = (key == thr) & (key != INT_MIN)
        tie_f = jnp.where(tie, 1.0, 0.0)
        rank = tie_seen + jnp.dot(tri, tie_f.astype(BF16), preferred_element_type=F32)
        sel = (key > thr) | (tie & (rank <= need))
        key_ref[ks, :] = jnp.where(sel, 1, 0)
        return tie_seen + jnp.sum(tie_f, axis=0, keepdims=True)

    lax.fori_loop(0, n_kt, sel_body, jnp.zeros((1, tq), F32))

    c_exp = (DH_D ** -0.5) * math.log2(math.e)
    ones_rows = jnp.ones((DSA_ONES, tk), BF16)
    qs = [q_ref[0, :, h * DH_D:(h + 1) * DH_D] for h in range(H_D)]

    def att_body(kt, carry):
        m, accs = carry
        ks = key_slice(kt)
        sel = key_ref[ks, :] != 0
        hs = range(H_D)
        kts = [k_ref[0, ks, j * DH_D:(j + 1) * DH_D] for j in range(KVH_D)]
        vts = [jnp.concatenate([vt_ref[0, j * DH_D:(j + 1) * DH_D, ks], ones_rows], axis=0) for j in range(KVH_D)]
        logits = [lax.dot_general(kts[h // QPK_D], qs[h], (((1,), (1,)), ((), ())), preferred_element_type=F32)
                  for h in hs]
        logits = [jnp.where(sel, x, NEG_BIG) for x in logits]
        m_rows = [jnp.maximum(m[h:h + 1, :], jnp.max(x, axis=0, keepdims=True)) for h, x in zip(hs, logits)]
        ps = [jnp.exp2((x - mn) * c_exp).astype(BF16) for x, mn in zip(logits, m_rows)]
        scales = [jnp.exp2((m[h:h + 1, :] - mn) * c_exp) for h, mn in zip(hs, m_rows)]
        pv = [jnp.dot(vts[h // QPK_D], p, preferred_element_type=F32) for h, p in zip(hs, ps)]
        new_accs = [accs[h] * sc + x for h, sc, x in zip(hs, scales, pv)]
        return jnp.concatenate(m_rows, axis=0), tuple(new_accs)

    init = (jnp.full((H_D, tq), NEG_BIG, F32), tuple(jnp.zeros((DH_D + DSA_ONES, tq), F32) for _ in range(H_D)))
    _, accs = lax.fori_loop(0, n_kt, att_body, init)
    out = jnp.concatenate([(a[:DH_D] / a[DH_D:DH_D + 1]).T for a in accs], axis=1)
    gd = gd_ref[0]
    o_ref[0] = out * (gd * _sigmoid(gd))


def dsa_prompt_pallas(proj3d, q_b, qi_b, k_b, vt_b, ki_b):
    b, l, _ = proj3d.shape
    n_sel = min(TOPK_MAX, l // 4)
    assert l % DSA_TQ == 0 and l % DSA_TK == 0
    qblk = lambda w, j: pl.BlockSpec((1, DSA_TQ, w), lambda bi, ti: (bi, ti, j))
    full = lambda w: pl.BlockSpec((1, l, w), lambda bi, ti: (bi, 0, 0))
    return pl.pallas_call(
        functools.partial(_dsa_prompt_kernel, n_sel=float(n_sel)),
        grid=(b, l // DSA_TQ),
        in_specs=[qblk(D_D, 0), qblk(QI_W, 0), qblk(128, OD_KIWI // 128), qblk(D_D, OD_GD // D_D),
                  full(KV_W), pl.BlockSpec((1, KV_W, l), lambda bi, ti: (bi, 0, 0)), full(DI_D)],
        out_specs=pl.BlockSpec((1, DSA_TQ, D_D), lambda bi, ti: (bi, ti, 0)),
        out_shape=jax.ShapeDtypeStruct((b, l, D_D), F32),
        scratch_shapes=[pltpu.VMEM((l, DSA_TQ), jnp.int32)],
        compiler_params=pltpu.CompilerParams(dimension_semantics=("parallel", "arbitrary")),
        name="dsa_prompt",
    )(q_b, qi_b, proj3d, proj3d, k_b, vt_b, ki_b)


S5_N = G_C * P_C
S5_ROWS = 2 * S5_N // 128
S5_TC = 256


def _s5_kernel(u_ref, g_ref, h0re_ref, h0im_ref, lam_t_ref, lam_r_ref, bd_ref, cd_ref, dsk_ref, gw_ref, gb_ref,
               o_ref, hre_ref, him_ref, scr_ref, h_ref, *, bb, tc, n_valid):
    c = pl.program_id(1)
    rows = bb * tc
    half = S5_ROWS // 2

    @pl.when(c == 0)
    def _():
        for i in range(bb):
            h_ref[i * S5_ROWS:i * S5_ROWS + half] = h0re_ref[i]
            h_ref[i * S5_ROWS + half:(i + 1) * S5_ROWS] = h0im_ref[i]

    def discretize(lam_ref):
        lr, li, dt = lam_ref[0], lam_ref[1], jnp.exp(lam_ref[2])
        mag = jnp.exp(lr * dt)
        br, bi = mag * jnp.cos(li * dt), mag * jnp.sin(li * dt)
        den = lr * lr + li * li
        fr = ((br - 1.0) * lr + bi * li) / den
        fi = (bi * lr - (br - 1.0) * li) / den
        return br, bi, fr, fi

    _, _, fr, fi = discretize(lam_r_ref)
    u = u_ref[...].reshape(rows, D_C)
    braw = _mm(u, bd_ref[...])
    b_re, b_im = braw[:, :S5_N], braw[:, S5_N:]
    bu = jnp.concatenate([fr * b_re - fi * b_im, fr * b_im + fi * b_re], axis=1)
    for r in range(S5_ROWS):
        scr_ref[pl.ds(r, rows, stride=S5_ROWS), :] = bu[:, r * 128:(r + 1) * 128]

    lbr, lbi, _, _ = discretize(lam_t_ref)
    for i in range(bb):
        def step(t, h):
            hr, hi = h
            idx = pl.multiple_of((i * tc + t) * S5_ROWS, S5_ROWS)
            blk = scr_ref[pl.ds(idx, S5_ROWS), :]
            nr = lbr * hr - lbi * hi + blk[:half]
            ni = lbr * hi + lbi * hr + blk[half:]
            scr_ref[pl.ds(idx, S5_ROWS), :] = jnp.concatenate([nr, ni], axis=0)
            return nr, ni

        h_in = (h_ref[i * S5_ROWS:i * S5_ROWS + half], h_ref[i * S5_ROWS + half:(i + 1) * S5_ROWS])
        hr, hi = lax.fori_loop(0, n_valid, step, h_in, unroll=4)
        h_ref[i * S5_ROWS:i * S5_ROWS + half] = hr
        h_ref[i * S5_ROWS + half:(i + 1) * S5_ROWS] = hi
        hre_ref[i] = hr
        him_ref[i] = hi

    y = dsk_ref[...] * u
    for r in range(S5_ROWS):
        y = y + _mm(scr_ref[pl.ds(r, rows, stride=S5_ROWS), :], cd_ref[r * 128:(r + 1) * 128, :])
    y = 0.5 * y * (1.0 + jnp.tanh(math.sqrt(2.0 / math.pi) * (y + 0.044715 * (y * y * y))))
    y = y * _sigmoid(_mm(y, gw_ref[...]) + gb_ref[...])
    gate = g_ref[...].reshape(rows, D_C)
    o_ref[...] = (y * (gate * _sigmoid(gate))).reshape(bb, tc, D_C)


def s5_pallas(proj3d, n_valid_tokens, h0_re, h0_im, lam_re, lam_im, log_dt, b_re, b_im, c_re, c_im, d_skip,
              glu_w, glu_b):
    b, l, _ = proj3d.shape
    tc = min(S5_TC, l)
    n_chunks = l // tc
    bb = max(1, min(b, S5_TC // tc))
    assert l % tc == 0 and b % bb == 0 and tc % 8 == 0
    n_valid = n_valid_tokens - (n_chunks - 1) * tc
    assert 0 < n_valid <= tc and (n_valid == tc or n_chunks == 1)
    half = S5_ROWS // 2
    dt_full = jnp.repeat(log_dt, P_C)
    lam_t = jnp.stack([lam_re.reshape(half, 128), lam_im.reshape(half, 128), dt_full.reshape(half, 128)])
    lam_r = jnp.stack([lam_re.reshape(1, S5_N), lam_im.reshape(1, S5_N), dt_full.reshape(1, S5_N)])
    eye = jnp.eye(G_C, dtype=F32)
    bd = jnp.concatenate([jnp.einsum('gpm,gh->gmhp', t, eye).reshape(D_C, S5_N) for t in (b_re, b_im)], axis=1)
    cd = jnp.concatenate([jnp.einsum('gmp,gh->gphm', t, eye).reshape(S5_N, D_C) for t in (c_re, -c_im)], axis=0)
    const = lambda shape: pl.BlockSpec(shape, lambda bi, ci: (0,) * len(shape))
    st_spec = pl.BlockSpec((bb, half, 128), lambda bi, ci: (bi, 0, 0))
    out, h_re, h_im = pl.pallas_call(
        functools.partial(_s5_kernel, bb=bb, tc=tc, n_valid=n_valid),
        grid=(b // bb, n_chunks),
        in_specs=[
            pl.BlockSpec((bb, tc, D_C), lambda bi, ci: (bi, ci, OD_U // D_C)),
            pl.BlockSpec((bb, tc, D_C), lambda bi, ci: (bi, ci, OD_GC // D_C)),
            st_spec, st_spec,
            const((3, half, 128)), const((3, 1, S5_N)), const((D_C, 2 * S5_N)), const((2 * S5_N, D_C)),
            const((1, D_C)), const((D_C, D_C)), const((1, D_C)),
        ],
        out_specs=[pl.BlockSpec((bb, tc, D_C), lambda bi, ci: (bi, ci, 0)), st_spec, st_spec],
        out_shape=[jax.ShapeDtypeStruct((b, l, D_C), F32), jax.ShapeDtypeStruct((b, half, 128), F32),
                   jax.ShapeDtypeStruct((b, half, 128), F32)],
        scratch_shapes=[pltpu.VMEM((bb * tc * S5_ROWS, 128), F32), pltpu.VMEM((bb * S5_ROWS, 128), F32)],
        compiler_params=pltpu.CompilerParams(dimension_semantics=("parallel", "arbitrary")),
        name="s5_scan",
    )(proj3d, proj3d, h0_re.reshape(b, half, 128), h0_im.reshape(b, half, 128), lam_t, lam_r,
      bd.astype(BF16), cd.astype(BF16), d_skip.reshape(1, D_C), glu_w.astype(BF16), glu_b.reshape(1, D_C))
    return out, h_re.reshape(b, G_C, P_C), h_im.reshape(b, G_C, P_C)


def _hi_mid_lo(x):
    hi = x.astype(BF16)
    r1 = x - hi.astype(F32)
    mid = r1.astype(BF16)
    lo = (r1 - mid.astype(F32)).astype(BF16)
    return hi, mid, lo


def _mm3_rhs01(x, b_bf16):
    return sum(jnp.dot(p, b_bf16, preferred_element_type=F32) for p in _hi_mid_lo(x))


def _ssd_kernel(z_ref, xbc_ref, dt_ref, conv0_ref, ssm0_ref, cw_ref, cb_ref, sm_ref, pv_ref, e16_ref, bones_ref,
                o_ref, ssm_ref, convn_ref, buf_ref, st_ref, *, cl, n_valid):
    c = pl.program_id(1)
    keep = CONV_W - 1

    @pl.when(c == 0)
    def _():
        buf_ref[8 - keep:8] = conv0_ref[0]
        st_ref[...] = ssm0_ref[0]

    buf_ref[8:8 + cl] = xbc_ref[0]
    cw = cw_ref[...]
    conv = cb_ref[...] + sum(buf_ref[8 - keep + i:8 - keep + i + cl] * cw[i:i + 1] for i in range(CONV_W))
    convn_ref[0] = buf_ref[8 + n_valid - keep:8 + n_valid]
    buf_ref[8 - keep:8] = buf_ref[8 + cl - keep:8 + cl]
    act = conv * _sigmoid(conv)
    xs, bm, cm = act[:, :D_B], act[:, D_B:D_B + G_B * N_B], act[:, D_B + G_B * N_B:]

    sm = sm_ref[...]
    dt = _softplus(dt_ref[0][:, :H_B] + sm[0:1, :H_B])
    row1 = lax.broadcasted_iota(jnp.int32, (cl, 1), 0)
    if n_valid < cl:
        dt = jnp.where(row1 < n_valid, dt, 0.0)
    a = dt * (-jnp.exp(sm[1:2, :H_B]))
    row = lax.broadcasted_iota(jnp.int32, (cl, cl), 0)
    col = lax.broadcasted_iota(jnp.int32, (cl, cl), 1)
    incl = col <= row
    a_parts = _hi_mid_lo(a)
    a_cum = sum(jnp.dot(incl.astype(BF16), p, preferred_element_type=F32) for p in a_parts)
    a_cum_t = sum(lax.dot_general(p, (row <= col).astype(BF16), (((0,), (0,)), ((), ())),
                                  preferred_element_type=F32) for p in a_parts)
    a_last = a_cum[cl - 1:cl]
    e16 = e16_ref[...]
    xdt = xs * _mm3_rhs01(dt, e16)
    ea_full = jnp.exp(_mm3_rhs01(a_cum, e16))
    xdec = xdt * jnp.exp(_mm3_rhs01(a_last - a_cum, e16))
    chunk_decay = jnp.exp(a_last)

    groups = range(G_B)
    gsl = [slice(g * N_B, (g + 1) * N_B) for g in groups]
    cbs = [_mm_nt(cm[:, s], bm[:, s]) for s in gsl]
    y_off = jnp.concatenate(
        [_mm_nt(cm[:, s], st_ref[g * HPG_B:(g + 1) * HPG_B].reshape(HPG_B * P_B, N_B)) for g, s in zip(groups, gsl)],
        axis=1)
    y_diag = []
    for h in range(H_B):
        hs = slice(h * P_B, (h + 1) * P_B)
        lmat = jnp.exp(jnp.where(incl, a_cum[:, h:h + 1] - a_cum_t[h:h + 1, :], NEG_BIG))
        y_diag.append(_mm(cbs[h // HPG_B] * lmat, xdt[:, hs]))
    for h in range(H_B):
        hs = slice(h * P_B, (h + 1) * P_B)
        st_ref[h] = st_ref[h] * chunk_decay[:, h:h + 1] + _mm_tn(xdec[:, hs], bm[:, gsl[h // HPG_B]])

    pv = pv_ref[...]
    y = jnp.concatenate(y_diag, axis=1) + y_off * ea_full + xs * pv[0:1]
    z = z_ref[0]
    y = y * (z * _sigmoid(z))
    ms = _mm_exact_rhs(y * y, bones_ref[...]) * (1.0 / (D_B // G_B))
    o_ref[0] = y * lax.rsqrt(ms + NORM_EPS) * pv[1:2]

    @pl.when(c == pl.num_programs(1) - 1)
    def _():
        ssm_ref[0] = st_ref[...]


def ssd_pallas(proj3d, n_valid_tokens, conv0, ssm0, conv_w, conv_b, dt_bias, a_log, d_skip, gnorm_g):
    b, l, _ = proj3d.shape
    cl = min(SSD_CHUNK, l)
    n_chunks = l // cl
    assert l % cl == 0 and cl % 8 == 0
    n_valid = n_valid_tokens - (n_chunks - 1) * cl
    assert 0 < n_valid <= cl and (n_valid == cl or n_chunks == 1)
    sm = jnp.zeros((8, 128), F32).at[0, :H_B].set(dt_bias).at[1, :H_B].set(a_log)
    pv = jnp.zeros((8, D_B), F32).at[0].set(jnp.repeat(d_skip, P_B)).at[1].set(gnorm_g)
    e16 = (jnp.arange(D_B)[None, :] // P_B == jnp.arange(H_B)[:, None]).astype(BF16)
    gid = jnp.arange(D_B) // (D_B // G_B)
    bones = (gid[:, None] == gid[None, :]).astype(BF16)
    const = lambda shape: pl.BlockSpec(shape, lambda bi, ci: (0,) * len(shape))
    return pl.pallas_call(
        functools.partial(_ssd_kernel, cl=cl, n_valid=n_valid),
        grid=(b, n_chunks),
        in_specs=[
            pl.BlockSpec((1, cl, D_B), lambda bi, ci: (bi, ci, EV_Z // D_B)),
            pl.BlockSpec((1, cl, CONV_DIM), lambda bi, ci: (bi, ci, EV_XBC // CONV_DIM)),
            pl.BlockSpec((1, cl, 128), lambda bi, ci: (bi, ci, EV_DT // 128)),
            pl.BlockSpec((1, CONV_W - 1, CONV_DIM), lambda bi, ci: (bi, 0, 0)),
            pl.BlockSpec((1, H_B, P_B, N_B), lambda bi, ci: (bi, 0, 0, 0)),
            const((CONV_W, CONV_DIM)), const((1, CONV_DIM)), const((8, 128)), const((8, D_B)),
            const((H_B, D_B)), const((D_B, D_B)),
        ],
        out_specs=[
            pl.BlockSpec((1, cl, D_B), lambda bi, ci: (bi, ci, 0)),
            pl.BlockSpec((1, H_B, P_B, N_B), lambda bi, ci: (bi, 0, 0, 0)),
            pl.BlockSpec((1, CONV_W - 1, CONV_DIM), lambda bi, ci: (bi, 0, 0)),
        ],
        out_shape=[jax.ShapeDtypeStruct((b, l, D_B), F32), jax.ShapeDtypeStruct((b, H_B, P_B, N_B), F32),
                   jax.ShapeDtypeStruct((b, CONV_W - 1, CONV_DIM), F32)],
        scratch_shapes=[pltpu.VMEM((8 + cl, CONV_DIM), F32), pltpu.VMEM((H_B, P_B, N_B), F32)],
        compiler_params=pltpu.CompilerParams(dimension_semantics=("parallel", "arbitrary")),
        name="ssd_chunked",
    )(proj3d, proj3d, proj3d, conv0, ssm0, conv_w, conv_b.reshape(1, CONV_DIM), sm, pv, e16, bones)


RADIX_BITS = 4


def _dsa_sample_kernel(pt_ref, q_ref, qi_ref, kiwi_ref, gd_ref, kn_ref, vn_ref, kin_ref, *rest, n_pages, n_sel, tq):
    kp_refs = rest[:n_pages]
    vp_refs = rest[n_pages:2 * n_pages]
    kip_refs = rest[2 * n_pages:3 * n_pages]
    o_ref = rest[3 * n_pages]
    ps = PAGE_SIZE
    n_tiles = n_pages + 1
    pad_rows = lambda x: jnp.concatenate([x, jnp.zeros((ps - tq, x.shape[1]), x.dtype)], axis=0)

    qi = qi_ref[0]
    qs_i = jnp.concatenate([qi[:, h * DI_D:(h + 1) * DI_D] for h in range(HI_D)], axis=0)
    wi = kiwi_ref[0][:, DI_D:DI_D + HI_D]
    wcol = jnp.concatenate([wi[:, h:h + 1] for h in range(HI_D)], axis=0)
    ki_tiles = [r[0].astype(BF16) for r in kip_refs] + [pad_rows(kin_ref[0])]
    scores = []
    for kit in ki_tiles:
        d = wcol * jnp.maximum(_mm_nt(qs_i, kit), 0.0)
        scores.append(sum(d[h * tq:(h + 1) * tq] for h in range(HI_D)))
    key = _sortable_key(jnp.concatenate(scores, axis=1) * IDX_SCALE)
    n_keys = n_tiles * ps
    qrow = lax.broadcasted_iota(jnp.int32, (tq, n_keys), 0)
    kcol = lax.broadcasted_iota(jnp.int32, (tq, n_keys), 1)
    key = jnp.where(kcol - n_pages * ps <= qrow, key, INT_MIN)

    def count(pred):
        return jnp.sum(jnp.where(pred, 1.0, 0.0), axis=1, keepdims=True)

    ans = jnp.zeros((tq, 1), jnp.int32)
    for shift in range(32 - RADIX_BITS, -1, -RADIX_BITS):
        digit = jnp.zeros((tq, 1), jnp.int32)
        for d in range(1, 2 ** RADIX_BITS):
            cand = ans | (d << shift) if d << shift < 2 ** 31 else ans | (d << shift) - 2 ** 32
            digit = digit + jnp.where(count(key >= (cand ^ INT_MIN)) >= n_sel, 1, 0)
        ans = ans | lax.shift_left(digit, shift)
    thr = ans ^ INT_MIN
    gt = key > thr
    need = n_sel - count(gt)
    tie = (key == thr) & (key != INT_MIN)
    tie_f = jnp.where(tie, 1.0, 0.0)
    tri = (lax.broadcasted_iota(jnp.int32, (ps, ps), 0) <= lax.broadcasted_iota(jnp.int32, (ps, ps), 1)).astype(BF16)
    seen = jnp.zeros((tq, 1), F32)
    ranks = []
    for t in range(n_tiles):
        tf = tie_f[:, t * ps:(t + 1) * ps]
        ranks.append(seen + jnp.dot(tf.astype(BF16), tri, preferred_element_type=F32))
        seen = seen + jnp.sum(tf, axis=1, keepdims=True)
    sel = jnp.where(gt | (tie & (jnp.concatenate(ranks, axis=1) <= need)), 1.0, 0.0).astype(BF16)

    krow = lax.broadcasted_iota(jnp.int32, (ps, KVH_D * ps), 0)
    ccol = lax.broadcasted_iota(jnp.int32, (ps, KVH_D * ps), 1)
    spread = [(ccol == KVH_D * krow + j).astype(BF16) for j in range(KVH_D)]
    masks = []
    for t in range(n_pages):
        st = sel[:, t * ps:(t + 1) * ps]
        per_kv = [jnp.dot(st, spread[j], preferred_element_type=F32) for j in range(KVH_D)]
        masks.append(jnp.concatenate([per_kv[h // QPK_D] for h in range(H_D)], axis=0))
    masks.append(jnp.concatenate([sel[:, n_pages * ps:].astype(F32)] * H_D, axis=0))
    mask = jnp.concatenate(masks, axis=1) > 0.5

    q = q_ref[0]
    qs = jnp.concatenate([q[:, h * DH_D:(h + 1) * DH_D] for h in range(H_D)], axis=0)
    kn, vn = pad_rows(kn_ref[0]), pad_rows(vn_ref[0])
    half = QPK_D * tq
    new_logits = jnp.concatenate([_mm_nt(qs[j * half:(j + 1) * half], kn[:, j * DH_D:(j + 1) * DH_D])
                                  for j in range(KVH_D)], axis=0)
    logits = jnp.concatenate([_mm_nt(qs, r[0].astype(BF16)) for r in kp_refs] + [new_logits], axis=1)
    logits = jnp.where(mask, logits * (DH_D ** -0.5), NEG_BIG)
    m = jnp.max(logits, axis=1, keepdims=True)
    p = jnp.where(mask, jnp.exp(logits - m), 0.0)
    l = jnp.sum(p, axis=1, keepdims=True)
    pb = p.astype(BF16)
    w2 = KVH_D * ps
    acc = sum(jnp.dot(pb[:, t * w2:(t + 1) * w2], r[0].astype(BF16), preferred_element_type=F32)
              for t, r in enumerate(vp_refs))
    p_new = pb[:, n_pages * w2:]
    acc = acc + jnp.concatenate([jnp.dot(p_new[j * half:(j + 1) * half], vn[:, j * DH_D:(j + 1) * DH_D],
                                         preferred_element_type=F32) for j in range(KVH_D)], axis=0)
    o = acc / l
    gd = gd_ref[0]
    o_ref[0] = jnp.concatenate([o[h * tq:(h + 1) * tq] for h in range(H_D)], axis=1) * (gd * _sigmoid(gd))


def dsa_sample_pallas(proj3d, q_b, qi_b, k_b, v_b, ki_b, cache_k, cache_v, cache_ki, page_table, n_valid_tokens):
    b, tq, _ = proj3d.shape
    n_pages = page_table.shape[1]
    n_sel = min(TOPK_MAX, (n_pages * PAGE_SIZE + n_valid_tokens) // 4)
    row = lambda w, j=0: pl.BlockSpec((1, tq, w), lambda bi, pt, j=j: (bi, 0, j))
    kv_page = lambda p: pl.BlockSpec((1, PAGE_SIZE * KVH_D, DH_D), lambda bi, pt, p=p: (pt[bi, p], 0, 0))
    ki_page = lambda p: pl.BlockSpec((1, PAGE_SIZE, DI_D), lambda bi, pt, p=p: (pt[bi, p], 0, 0))
    grid_spec = pltpu.PrefetchScalarGridSpec(
        num_scalar_prefetch=1,
        grid=(b,),
        in_specs=[row(D_D), row(QI_W), row(128, OD_KIWI // 128), row(D_D, OD_GD // D_D), row(KV_W), row(KV_W), row(DI_D)]
        + [kv_page(p) for p in range(n_pages)] + [kv_page(p) for p in range(n_pages)]
        + [ki_page(p) for p in range(n_pages)],
        out_specs=pl.BlockSpec((1, tq, D_D), lambda bi, pt: (bi, 0, 0)),
    )
    return pl.pallas_call(
        functools.partial(_dsa_sample_kernel, n_pages=n_pages, n_sel=float(n_sel), tq=tq),
        grid_spec=grid_spec,
        out_shape=jax.ShapeDtypeStruct((b, tq, D_D), F32),
        compiler_params=pltpu.CompilerParams(dimension_semantics=("parallel",)),
        name="dsa_sample",
    )(page_table, q_b, qi_b, proj3d, proj3d, k_b, v_b, ki_b, *([cache_k] * n_pages), *([cache_v] * n_pages),
      *([cache_ki] * n_pages))


def _split(x, sizes):
    return jnp.split(x, np.cumsum(sizes)[:-1].tolist(), axis=-1)


def even_mixer(x, g, lv, shift0, wkv0, conv0, ssm0, w_in, w_out, mu, w0, w2, a0, a2, k_k, k_a, r_k,
               ln_g, ln_b, conv_w, conv_b, dt_bias, a_log, d_skip, gnorm_g):
    b, l, d = x.shape
    pa_w, ga_w, z_w, xbc_w, dt_w = _split(w_in, [A_SHIFT, D_A, D_B, CONV_DIM, H_B])
    w_ev = jnp.concatenate([ga_w, z_w, xbc_w, pa_w, dt_w, jnp.zeros((d, EV_N - IN_E), w_in.dtype)], axis=1)
    proj = norm_matmul(x.reshape(b * l, d), g, w_ev.astype(BF16)).reshape(b, l, EV_N)
    out_a, wkv_new, shift_new = rwkv7_pallas(proj, shift0, wkv0, mu, w0, w2, a0, a2, k_k, k_a, r_k, ln_g, ln_b, lv)
    out_b, ssm_new, conv_new = ssd_pallas(proj, lv, conv0, ssm0, conv_w, conv_b, dt_bias, a_log, d_skip, gnorm_g)
    x_new = matmul_residual(out_a.reshape(b * l, D_A), out_b.reshape(b * l, D_B), w_out.astype(BF16),
                            x.reshape(b * l, d)).reshape(b, l, d)
    return x_new, (wkv_new, shift_new, ssm_new, conv_new)


def odd_mixer(x, g, lv, pos, c_re0, c_im0, attend, w_in, w_out, lam_re, lam_im, log_dt, b_re, b_im,
              c_re, c_im, d_skip, glu_w, glu_b):
    b, l, d = x.shape
    u_w, gc_w, q_w, k_w, v_w, qi_w, ki_w, wi_w, gd_w = _split(
        w_in, [D_C, D_C, D_D, KV_W, KV_W, QI_W, DI_D, HI_D, D_D])
    w_od = jnp.concatenate([u_w, gc_w, q_w, gd_w, k_w, v_w, qi_w, ki_w, wi_w,
                            jnp.zeros((d, OD_N - IN_O), w_in.dtype)], axis=1)
    proj = norm_matmul(x.reshape(b * l, d), g, w_od.astype(BF16)).reshape(b, l, OD_N)
    out_c, re_last, im_last = s5_pallas(proj, lv, c_re0, c_im0, lam_re, lam_im, log_dt, b_re, b_im,
                                        c_re, c_im, d_skip, glu_w, glu_b)
    q_b, qi_b, k_f, v_f, ki, k_b, v_b, ki_b = rope_pallas(proj, pos, v_transposed=attend is None)
    if attend is None:
        out_d = dsa_prompt_pallas(proj, q_b, qi_b, k_b, v_b, ki_b)
    else:
        out_d = dsa_sample_pallas(proj, q_b, qi_b, k_b, v_b, ki_b, *attend, lv)
    k = k_f[:, :lv].reshape(b, lv, KVH_D, DH_D)
    v = v_f[:, :lv].reshape(b, lv, KVH_D, DH_D)
    ki = ki[:, :lv]
    x_new = matmul_residual(out_c.reshape(b * l, D_C), out_d.reshape(b * l, D_D), w_out.astype(BF16),
                            x.reshape(b * l, d)).reshape(b, l, d)
    return x_new, (re_last, im_last, k, v, ki)


def kernel(x_prompt, x_sample, state_a_wkv, state_a_shift, state_b_ssm, state_b_conv, state_c_re, state_c_im, cache_d_k, cache_d_v, cache_d_kidx, page_table, norm_g, final_norm_g, w_in_e, w_out_e, rwkv_mu, rwkv_w0, rwkv_w2, rwkv_a0, rwkv_a2, rwkv_kk, rwkv_ka, rwkv_rk, rwkv_ln_g, rwkv_ln_b, ssd_conv_w, ssd_conv_b, ssd_dt_bias, ssd_a_log, ssd_d, ssd_norm_g, w_in_o, w_out_o, s5_lam_re, s5_lam_im, s5_log_dt, s5_b_re, s5_b_im, s5_c_re, s5_c_im, s5_d, s5_glu_w, s5_glu_b):
    f32 = jnp.float32
    bp, lp = x_prompt.shape[:2]
    bs, ls = x_sample.shape[:2]
    pos_p = jnp.arange(lp)
    ls_pad = _round_up(ls, 8)
    pos_s = PAST_LEN + jnp.arange(ls_pad)
    xp, xs = x_prompt, jnp.pad(x_sample, ((0, 0), (0, ls_pad - ls), (0, 0)))
    even_p, even_s, odd_p, odd_s = [], [], [], []
    for i in range(DEPTH):
        j = i // 2
        if i % 2 == 0:
            pe = (w_in_e[j], w_out_e[j], rwkv_mu[j], rwkv_w0[j], rwkv_w2[j], rwkv_a0[j], rwkv_a2[j],
                  rwkv_kk[j], rwkv_ka[j], rwkv_rk[j], rwkv_ln_g[j], rwkv_ln_b[j], ssd_conv_w[j],
                  ssd_conv_b[j], ssd_dt_bias[j], ssd_a_log[j], ssd_d[j], ssd_norm_g[j])
            xp, st_p = even_mixer(xp, norm_g[i], lp, jnp.zeros((bp, A_SHIFT), f32), jnp.zeros((bp, H_A, HD_A, HD_A), f32),
                                  jnp.zeros((bp, CONV_W - 1, CONV_DIM), f32),
                                  jnp.zeros((bp, H_B, P_B, N_B), f32), *pe)
            xs, st_s = even_mixer(xs, norm_g[i], ls, state_a_shift[j], state_a_wkv[j], state_b_conv[j], state_b_ssm[j], *pe)
            even_p.append(st_p)
            even_s.append(st_s)
        else:
            po = (w_in_o[j], w_out_o[j], s5_lam_re[j], s5_lam_im[j], s5_log_dt[j], s5_b_re[j], s5_b_im[j],
                  s5_c_re[j], s5_c_im[j], s5_d[j], s5_glu_w[j], s5_glu_b[j])
            zc = jnp.zeros((bp, G_C, P_C), f32)
            xp, st_p = odd_mixer(xp, norm_g[i], lp, pos_p, zc, zc, None, *po)
            n_pool = cache_d_k.shape[1]
            attend_s = (cache_d_k.reshape(-1, PAGE_SIZE * KVH_D, DH_D), cache_d_v.reshape(-1, PAGE_SIZE * KVH_D, DH_D),
                        cache_d_kidx.reshape(-1, PAGE_SIZE, DI_D), page_table + j * n_pool)
            xs, st_s = odd_mixer(xs, norm_g[i], ls, pos_s, state_c_re[j], state_c_im[j], attend_s, *po)
            odd_p.append(st_p)
            odd_s.append(st_s)
    y_prompt = rmsnorm_rows(xp.reshape(bp * lp, D_MODEL), final_norm_g).reshape(bp, lp, D_MODEL)
    y_sample = rmsnorm_rows(xs.reshape(bs * ls_pad, D_MODEL), final_norm_g).reshape(bs, ls_pad, D_MODEL)[:, :ls]
    new_a_wkv_p, new_a_shift_p, new_b_ssm_p, new_b_conv_p = [jnp.stack(t) for t in zip(*even_p)]
    new_a_wkv_s, new_a_shift_s, new_b_ssm_s, new_b_conv_s = [jnp.stack(t) for t in zip(*even_s)]
    new_c_re_p, new_c_im_p, new_d_k_p, new_d_v_p, new_d_kidx_p = [jnp.stack(t) for t in zip(*odd_p)]
    new_c_re_s, new_c_im_s, new_d_k_s, new_d_v_s, new_d_kidx_s = [jnp.stack(t) for t in zip(*odd_s)]
    return (y_prompt, y_sample,
            new_a_wkv_p, new_a_shift_p, new_b_ssm_p, new_b_conv_p,
            new_c_re_p, new_c_im_p, new_d_k_p, new_d_v_p, new_d_kidx_p,
            new_a_wkv_s, new_a_shift_s, new_b_ssm_s, new_b_conv_s,
            new_c_re_s, new_c_im_s, new_d_k_s, new_d_v_s, new_d_kidx_s)
```

```python
import functools
import math

import jax
import jax.numpy as jnp
import numpy as np
from jax import lax
from jax.experimental import pallas as pl
from jax.experimental.pallas import tpu as pltpu

D_MODEL = 1024
DEPTH = 4
PAST_LEN = 2048
PAGE_SIZE = 128
NORM_EPS = 1e-6

D_A = D_MODEL
HD_A = 64
H_A = D_A // HD_A
W_LORA = 64
A_LORA = 64
A_SHIFT = 3 * D_A + W_LORA + A_LORA
RWKV_LN_EPS = 64e-5
D_B = D_MODEL
P_B = 64
H_B = D_B // P_B
N_B = 128
G_B = 4
HPG_B = H_B // G_B
CONV_W = 4
CONV_DIM = D_B + 2 * G_B * N_B
SSD_CHUNK = 128
D_C = D_MODEL // 2
CH_C = 16
G_C = D_C // CH_C
P_C = 64
H_D = 8
DH_D = 128
KVH_D = 2
QPK_D = H_D // KVH_D
D_D = H_D * DH_D
HI_D = 8
DI_D = 64
IDX_SCALE = (DI_D ** -0.5) * (HI_D ** -0.5)
TOPK_MAX = 256
QBLK = 128
ROPE_THETA = 500000.0
ROPE_FRAC = 4

IN_E = A_SHIFT + D_A + D_B + CONV_DIM + H_B
OUT_E = D_A + D_B
IN_O = 2 * D_C + D_D + 2 * KVH_D * DH_D + HI_D * DI_D + DI_D + HI_D + D_D
OUT_O = D_C + D_D

F32 = jnp.float32
BF16 = jnp.bfloat16

TILE_M = 512
TILE_N = 512
PROJ_TILE_M = 1024


def _round_up(n, m):
    return (n + m - 1) // m * m


def _norm_matmul_kernel(x_ref, g_ref, w_ref, o_ref, h_ref):
    @pl.when(pl.program_id(1) == 0)
    def _():
        x = x_ref[...]
        ms = jnp.mean(x * x, axis=-1, keepdims=True)
        h_ref[...] = (x * lax.rsqrt(ms + NORM_EPS) * g_ref[...]).astype(BF16)

    o_ref[...] = jnp.dot(h_ref[...], w_ref[...], preferred_element_type=F32)


def norm_matmul(x2d, g, w_bf16, tn=TILE_N):
    m, d = x2d.shape
    n = w_bf16.shape[1]
    tm = min(PROJ_TILE_M, m)
    assert m % tm == 0 and n % tn == 0
    return pl.pallas_call(
        _norm_matmul_kernel,
        grid=(m // tm, n // tn),
        in_specs=[
            pl.BlockSpec((tm, d), lambda i, j: (i, 0)),
            pl.BlockSpec((1, d), lambda i, j: (0, 0)),
            pl.BlockSpec((d, tn), lambda i, j: (0, j)),
        ],
        out_specs=pl.BlockSpec((tm, tn), lambda i, j: (i, j)),
        out_shape=jax.ShapeDtypeStruct((m, n), F32),
        scratch_shapes=[pltpu.VMEM((tm, d), BF16)],
        compiler_params=pltpu.CompilerParams(dimension_semantics=("parallel", "arbitrary")),
        name="norm_matmul",
    )(x2d, g.reshape(1, d), w_bf16)


def _matmul_res_kernel(a1_ref, a2_ref, w1_ref, w2_ref, r_ref, o_ref):
    o_ref[...] = (r_ref[...] + jnp.dot(a1_ref[...].astype(BF16), w1_ref[...], preferred_element_type=F32)
                  + jnp.dot(a2_ref[...].astype(BF16), w2_ref[...], preferred_element_type=F32))


def matmul_residual(a1, a2, w_bf16, res2d):
    m, k1 = a1.shape
    k2 = a2.shape[1]
    n = w_bf16.shape[1]
    tm = min(TILE_M, m)
    assert m % tm == 0 and w_bf16.shape[0] == k1 + k2
    return pl.pallas_call(
        _matmul_res_kernel,
        grid=(m // tm,),
        in_specs=[
            pl.BlockSpec((tm, k1), lambda i: (i, 0)),
            pl.BlockSpec((tm, k2), lambda i: (i, 0)),
            pl.BlockSpec((k1, n), lambda i: (0, 0)),
            pl.BlockSpec((k2, n), lambda i: (0, 0)),
            pl.BlockSpec((tm, n), lambda i: (i, 0)),
        ],
        out_specs=pl.BlockSpec((tm, n), lambda i: (i, 0)),
        out_shape=jax.ShapeDtypeStruct((m, n), F32),
        compiler_params=pltpu.CompilerParams(dimension_semantics=("parallel",)),
        name="matmul_residual",
    )(a1, a2, w_bf16[:k1], w_bf16[k1:], res2d)


def _rmsnorm_kernel(x_ref, g_ref, o_ref):
    x = x_ref[...]
    ms = jnp.mean(x * x, axis=-1, keepdims=True)
    o_ref[...] = x * lax.rsqrt(ms + NORM_EPS) * g_ref[...]


def rmsnorm_rows(x2d, g):
    m, d = x2d.shape
    tm = min(TILE_M, m)
    return pl.pallas_call(
        _rmsnorm_kernel,
        grid=(m // tm,),
        in_specs=[pl.BlockSpec((tm, d), lambda i: (i, 0)), pl.BlockSpec((1, d), lambda i: (0, 0))],
        out_specs=pl.BlockSpec((tm, d), lambda i: (i, 0)),
        out_shape=jax.ShapeDtypeStruct((m, d), F32),
        compiler_params=pltpu.CompilerParams(dimension_semantics=("parallel",)),
        name="final_rmsnorm",
    )(x2d, g.reshape(1, d))


def _mm(a, b):
    return jnp.dot(a.astype(BF16), b.astype(BF16), preferred_element_type=F32)


def _mm_nt(a, b):
    return lax.dot_general(a.astype(BF16), b.astype(BF16), (((1,), (1,)), ((), ())), preferred_element_type=F32)


def _mm_tn(a, b):
    return lax.dot_general(a.astype(BF16), b.astype(BF16), (((0,), (0,)), ((), ())), preferred_element_type=F32)


def _hi_lo(x):
    hi = x.astype(BF16)
    lo = (x - hi.astype(F32)).astype(BF16)
    return hi, lo


def _mm_exact_lhs(a_bf16, x):
    hi, lo = _hi_lo(x)
    return (jnp.dot(a_bf16, hi, preferred_element_type=F32) + jnp.dot(a_bf16, lo, preferred_element_type=F32))


def _mm_exact_rhs(x, b_bf16):
    hi, lo = _hi_lo(x)
    return (jnp.dot(hi, b_bf16, preferred_element_type=F32) + jnp.dot(lo, b_bf16, preferred_element_type=F32))


def _softplus(x):
    return jnp.maximum(x, 0.0) + jnp.log1p(jnp.exp(-jnp.abs(x)))


def _sigmoid(x):
    return 1.0 / (1.0 + jnp.exp(-x))


def _unit_lower_inverses(a_list, row, col, n):
    eye = jnp.where(row == col, 1.0, 0.0).astype(F32)
    ts = [eye for _ in a_list]
    m = 1
    while m < n:
        in_pair = (row // (2 * m)) == (col // (2 * m))
        lvl = in_pair & ((row % (2 * m)) >= m) & ((col % (2 * m)) < m)
        ls = [jnp.where(lvl, a, 0.0) for a in a_list]
        if m == 1:
            ts = [t - l for t, l in zip(ts, ls)]
        else:
            tl = [_mm(t, l) for t, l in zip(ts, ls)]
            ts = [t - _mm(x, t) for t, x in zip(ts, tl)]
        m *= 2
    return ts


EV_GA, EV_Z, EV_XBC, EV_R, EV_K, EV_V, EV_WA, EV_DT = 0, 1024, 2048, 4096, 5120, 6144, 7168, 7296
EV_N = 7680
RWKV_CHUNK = 64


RWKV_GROUP = 4
RWKV_BATCH = 4


def _block_rows(x, n, mask):
    return jnp.where(mask, jnp.concatenate([x] * n, axis=0), 0.0)


def _wkv_groups(alpha, beta, kappa, rho, v, kappa_e, beta_e, wc, st_ref, C, bb):
    gh, hd = RWKV_GROUP, HD_A
    gw, tw = gh * hd, gh * C
    iota = lambda shape, d: lax.broadcasted_iota(jnp.int32, shape, d)
    row, cin = iota((C, tw), 0), iota((C, tw), 1) % C
    strict, incl = cin < row, cin <= row
    m_tt = iota((tw, tw), 0) // C == iota((tw, tw), 1) // C
    m_tk = iota((tw, gw), 0) // C == iota((tw, gw), 1) // hd
    m_kk = iota((gw, gw), 0) // hd == iota((gw, gw), 1) // hd
    probs = [(b, slice(b * C, (b + 1) * C), slice(g * gw, (g + 1) * gw)) for b in range(bb) for g in range(H_A // gh)]
    s0s = [st_ref[b, :, sl] for b, _, sl in probs]
    bd_s = [_block_rows(s0, gh, m_kk) for s0 in s0s]
    bd_v = [_block_rows(v[rs, sl], gh, m_tk) for _, rs, sl in probs]
    grams = [_mm_nt(jnp.concatenate([alpha[rs, sl], rho[rs, sl]], axis=0),
                    jnp.concatenate([_block_rows(beta[rs, sl], gh, m_tk), _block_rows(kappa[rs, sl], gh, m_tk)],
                                    axis=0)) for _, rs, sl in probs]
    a_bs = [jnp.where(strict, g[:C, :tw], 0.0) for g in grams]
    ts = [jnp.where(cin == row, 1.0, 0.0).astype(F32) for _ in probs]
    m = 1
    while m < C:
        lvl = ((row // (2 * m)) == (cin // (2 * m))) & ((row % (2 * m)) >= m) & ((cin % (2 * m)) < m)
        ls = [jnp.where(lvl, a, 0.0) for a in a_bs]
        if m == 1:
            ts = [t - l for t, l in zip(ts, ls)]
        else:
            tl = [_mm(t, _block_rows(l, gh, m_tt)) for t, l in zip(ts, ls)]
            ts = [t - _mm(x, _block_rows(t, gh, m_tt)) for t, x in zip(ts, tl)]
        m *= 2
    rhss = [_mm_nt(alpha[rs, sl], bs) + _mm(jnp.where(strict, g[:C, tw:], 0.0), bv)
            for (_, rs, sl), bs, bv, g in zip(probs, bd_s, bd_v, grams)]
    us = [_mm(t, _block_rows(x, gh, m_tk)) for t, x in zip(ts, rhss)]
    outs = [_mm_nt(rho[rs, sl], bs) + _mm(jnp.where(incl, g[C:, tw:], 0.0), bv)
            - _mm(jnp.where(incl, g[C:, :tw], 0.0), _block_rows(u, gh, m_tk))
            for (_, rs, sl), bs, bv, g, u in zip(probs, bd_s, bd_v, grams, us)]
    for (b, rs, sl), s0, u in zip(probs, s0s, us):
        x = jnp.concatenate([v[rs, sl], -u], axis=0)
        y = jnp.concatenate([kappa_e[rs, sl], beta_e[rs, sl]], axis=0)
        cross = jnp.where(m_kk, _mm_tn(x, y), 0.0)
        st_ref[b, :, sl] = s0 * wc[rs, sl][:1] + sum(cross[h * hd:(h + 1) * hd] for h in range(gh))
    n_g = H_A // gh
    return jnp.concatenate([jnp.concatenate(outs[b * n_g:(b + 1) * n_g], axis=1) for b in range(bb)], axis=0)


def _wkv_heads(alpha, beta, kappa, rho, v, kappa_e, beta_e, wc, st_ref, C, bb):
    row = lax.broadcasted_iota(jnp.int32, (C, C), 0)
    col = lax.broadcasted_iota(jnp.int32, (C, C), 1)
    incl, strict = col <= row, col < row
    probs = [(b, slice(b * C, (b + 1) * C), slice(h * HD_A, (h + 1) * HD_A)) for b in range(bb) for h in range(H_A)]
    s0s = [st_ref[b, :, sl] for b, _, sl in probs]
    bks = [jnp.concatenate([beta[rs, sl], kappa[rs, sl]], axis=0) for _, rs, sl in probs]
    g_as = [_mm_nt(alpha[rs, sl], bk) for (_, rs, sl), bk in zip(probs, bks)]
    g_rs = [_mm_nt(rho[rs, sl], bk) for (_, rs, sl), bk in zip(probs, bks)]
    t_invs = _unit_lower_inverses([jnp.where(strict, g[:, :C], 0.0) for g in g_as], row, col, C)
    rhss = [_mm_nt(alpha[rs, sl], s0) + _mm(jnp.where(strict, g[:, C:], 0.0), v[rs, sl])
            for (_, rs, sl), s0, g in zip(probs, s0s, g_as)]
    us = [_mm(t, x) for t, x in zip(t_invs, rhss)]
    outs = [_mm_nt(rho[rs, sl], s0) + _mm(jnp.where(incl, g[:, C:], 0.0), v[rs, sl])
            - _mm(jnp.where(incl, g[:, :C], 0.0), u) for (_, rs, sl), s0, g, u in zip(probs, s0s, g_rs, us)]
    for (b, rs, sl), s0, u in zip(probs, s0s, us):
        x = jnp.concatenate([v[rs, sl], -u], axis=0)
        y = jnp.concatenate([kappa_e[rs, sl], beta_e[rs, sl]], axis=0)
        st_ref[b, :, sl] = s0 * wc[rs, sl][:1] + _mm_tn(x, y)
    return jnp.concatenate([jnp.concatenate(outs[b * H_A:(b + 1) * H_A], axis=1) for b in range(bb)], axis=0)


def _rwkv_kernel(r_ref, k_ref, v_ref, wa_ref, g_ref, sh0_ref, s0_ref, mu_ref, pv_ref, w2_ref, a2_ref, bones_ref,
                 o_ref, snew_ref, shnew_ref, st_ref, carry_ref, *, chunk, n_valid, bb):
    C = chunk
    R = bb * C
    c = pl.program_id(1)

    @pl.when(c == 0)
    def _():
        carry_ref[...] = sh0_ref[...]
        for b in range(bb):
            for h in range(H_A):
                st_ref[b, :, h * HD_A:(h + 1) * HD_A] = s0_ref[b, h]

    mu = mu_ref[...]
    pv = pv_ref[...]
    w0, a0, k_k, k_a, r_k, ln_g, ln_b = (pv[i:i + 1] for i in range(7))
    row8 = lax.broadcasted_iota(jnp.int32, (8, 1), 0)
    t_in_chunk = lax.broadcasted_iota(jnp.int32, (R, 1), 0) % C

    def tok_shift(x, lo, hi):
        rolled = pltpu.roll(x, 1, 0)
        pieces = []
        for b in range(bb):
            pieces.append(jnp.where(row8 == 0, carry_ref[b, :, lo:hi], rolled[b * C:b * C + 8]))
            if C > 8:
                pieces.append(rolled[b * C + 8:(b + 1) * C])
        return x + (jnp.concatenate(pieces, axis=0) - x) * mu[:, lo:hi]

    pr, pk, pvv, pwa = (ref[...].reshape(R, ref.shape[-1]) for ref in (r_ref, k_ref, v_ref, wa_ref))
    r = tok_shift(pr, 0, D_A)
    k = tok_shift(pk, D_A, 2 * D_A)
    v = tok_shift(pvv, 2 * D_A, 3 * D_A)
    wa = tok_shift(pwa, 3 * D_A, A_SHIFT)
    for b in range(bb):
        last = b * C + n_valid - 1
        new_carry = jnp.concatenate([x[last:last + 1] for x in (pr, pk, pvv, pwa)], axis=1)
        carry_ref[b] = new_carry
        shnew_ref[b] = new_carry

    wl, al = wa[:, :W_LORA], wa[:, W_LORA:]
    w_log = -_softplus(-(w0 + _mm(jnp.tanh(wl), w2_ref[...]))) - 0.5
    logw = -jnp.exp(w_log)
    a = _sigmoid(a0 + _mm(al, a2_ref[...]))
    bones = bones_ref[...]
    kk = k * k_k
    kk = kk / jnp.maximum(jnp.sqrt(_mm_exact_rhs(kk * kk, bones)), 1e-12)
    kp = k * (1.0 + (a - 1.0) * k_a)
    bonus = _mm_exact_rhs(r * kp * r_k, bones) * v
    if n_valid < C:
        ok = t_in_chunk < n_valid
        logw = jnp.where(ok, logw, 0.0)
        kk = jnp.where(ok, kk, 0.0)
        kp = jnp.where(ok, kp, 0.0)

    row = lax.broadcasted_iota(jnp.int32, (R, R), 0)
    col = lax.broadcasted_iota(jnp.int32, (R, R), 1)
    same = (row // C) == (col // C)
    cum = _mm_exact_lhs((same & (col <= row)).astype(BF16), logw)
    tot = _mm_exact_lhs(same.astype(BF16), logw)
    eneg = jnp.exp(-cum)
    alpha = kk * jnp.exp(cum - logw)
    ka = kk * a
    beta = ka * eneg
    kappa = kp * eneg
    rho = r * jnp.exp(cum)
    dec_end = jnp.exp(tot - cum)
    kappa_e = kp * dec_end
    beta_e = ka * dec_end
    wc = jnp.exp(tot)

    solve = _wkv_groups if RWKV_GROUP * C % 128 == 0 else _wkv_heads
    out = solve(alpha, beta, kappa, rho, v, kappa_e, beta_e, wc, st_ref, C, bb)

    mean = _mm_exact_rhs(out, bones) * (1.0 / HD_A)
    d = out - mean
    var = _mm_exact_rhs(d * d, bones) * (1.0 / HD_A)
    y = d * lax.rsqrt(var + RWKV_LN_EPS) * ln_g + ln_b
    gate = g_ref[...].reshape(R, D_A)
    o_ref[...] = ((y + bonus) * (gate * _sigmoid(gate))).reshape(bb, C, D_A)
    for b in range(bb):
        for h in range(H_A):
            snew_ref[b, h] = st_ref[b, :, h * HD_A:(h + 1) * HD_A]


def rwkv7_pallas(proj3d, shift0, wkv0, mu, w0, w2, a0, a2, k_k, k_a, r_k, ln_g, ln_b, n_valid_tokens):
    b, l, _ = proj3d.shape
    chunk = min(RWKV_CHUNK, l)
    n_chunks = l // chunk
    bb = math.gcd(b, RWKV_BATCH)
    assert l % chunk == 0 and chunk % 8 == 0
    n_valid = n_valid_tokens - (n_chunks - 1) * chunk
    assert 0 < n_valid <= chunk and (n_valid == chunk or n_chunks == 1)
    pvec = jnp.stack([w0, a0, k_k, k_a, r_k.reshape(D_A), ln_g, ln_b, jnp.zeros_like(w0)])
    hid = jnp.arange(D_A) // HD_A
    bones = (hid[:, None] == hid[None, :]).astype(BF16)
    blk = lambda w, j: pl.BlockSpec((bb, chunk, w), lambda bi, ci: (bi, ci, j))
    const = lambda shape: pl.BlockSpec(shape, lambda bi, ci: (0,) * len(shape))
    out, wkv_new, shift_new = pl.pallas_call(
        functools.partial(_rwkv_kernel, chunk=chunk, n_valid=n_valid, bb=bb),
        grid=(b // bb, n_chunks),
        in_specs=[
            blk(D_A, EV_R // D_A), blk(D_A, EV_K // D_A), blk(D_A, EV_V // D_A), blk(128, EV_WA // 128),
            blk(D_A, EV_GA // D_A),
            pl.BlockSpec((bb, 1, A_SHIFT), lambda bi, ci: (bi, 0, 0)),
            pl.BlockSpec((bb, H_A, HD_A, HD_A), lambda bi, ci: (bi, 0, 0, 0)),
            const((1, A_SHIFT)), const((8, D_A)), const((W_LORA, D_A)), const((A_LORA, D_A)), const((D_A, D_A)),
        ],
        out_specs=[
            pl.BlockSpec((bb, chunk, D_A), lambda bi, ci: (bi, ci, 0)),
            pl.BlockSpec((bb, H_A, HD_A, HD_A), lambda bi, ci: (bi, 0, 0, 0)),
            pl.BlockSpec((bb, 1, A_SHIFT), lambda bi, ci: (bi, 0, 0)),
        ],
        out_shape=[
            jax.ShapeDtypeStruct((b, l, D_A), F32),
            jax.ShapeDtypeStruct((b, H_A, HD_A, HD_A), F32),
            jax.ShapeDtypeStruct((b, 1, A_SHIFT), F32),
        ],
        scratch_shapes=[pltpu.VMEM((bb, HD_A, D_A), F32), pltpu.VMEM((bb, 1, A_SHIFT), F32)],
        compiler_params=pltpu.CompilerParams(dimension_semantics=("parallel", "arbitrary")),
        name="rwkv7_chunked",
    )(proj3d, proj3d, proj3d, proj3d, proj3d, shift0.reshape(b, 1, A_SHIFT), wkv0,
      mu.reshape(1, A_SHIFT), pvec, w2.astype(BF16), a2.astype(BF16), bones)
    return out, wkv_new, shift_new.reshape(b, A_SHIFT)


OD_U, OD_GC, OD_Q, OD_GD, OD_K, OD_V, OD_QI, OD_KIWI = 0, 512, 1024, 2048, 3072, 3328, 3584, 4096
OD_N = 4608
KV_W = KVH_D * DH_D
QI_W = HI_D * DI_D
INT_MIN = -2 ** 31


def _rope_tables(pos, head_dim):
    rd = head_dim // ROPE_FRAC
    half = rd // 2
    inv_freq = ROPE_THETA ** (-jnp.arange(half, dtype=F32) / half)
    ang = pos.astype(F32)[:, None] * inv_freq[None, :]
    cos, sin = jnp.cos(ang), jnp.sin(ang)
    t = pos.shape[0]
    ones = jnp.ones((t, head_dim - rd), F32)
    cos_h = jnp.concatenate([cos, cos, ones], axis=1)
    sin_h = jnp.concatenate([-sin, sin, 0.0 * ones], axis=1)
    reps = 128 // head_dim
    return jnp.tile(cos_h, (1, reps)), jnp.tile(sin_h, (1, reps))


def _rotate(x, cos_t, sin_t, head_dim):
    w = x.shape[1]
    half = head_dim // ROPE_FRAC // 2
    lane = lax.broadcasted_iota(jnp.int32, x.shape, 1) % head_dim
    if w >= 128:
        cos_f = jnp.tile(cos_t, (1, w // 128))
        sin_f = jnp.tile(sin_t, (1, w // 128))
        partner = jnp.where(lane < half, pltpu.roll(x, w - half, 1), pltpu.roll(x, half, 1))
    else:
        cos_f, sin_f = cos_t[:, :w], sin_t[:, :w]
        partner = jnp.where(lane < half, jnp.concatenate([x[:, half:], x[:, :half]], axis=1),
                            jnp.concatenate([x[:, w - half:], x[:, :w - half]], axis=1))
    return x * cos_f + partner * sin_f


def _rope_kernel(q_ref, k_ref, v_ref, qi_ref, kiwi_ref, c128_ref, s128_ref, c64_ref, s64_ref,
                 qo_ref, qio_ref, ko_ref, vo_ref, kio_ref, kb_ref, vb_ref, kib_ref, *, v_transposed):
    c128, s128, c64, s64 = c128_ref[...], s128_ref[...], c64_ref[...], s64_ref[...]
    qo_ref[0] = _rotate(q_ref[0], c128, s128, DH_D).astype(BF16)
    qio_ref[0] = _rotate(qi_ref[0], c64, s64, DI_D).astype(BF16)
    k_rot = _rotate(k_ref[0], c128, s128, DH_D)
    ko_ref[0] = k_rot
    kb_ref[0] = k_rot.astype(BF16)
    v = v_ref[0]
    vo_ref[0] = v
    vb_ref[0] = (v.T if v_transposed else v).astype(BF16)
    ki_rot = _rotate(kiwi_ref[0][:, :DI_D], c64, s64, DI_D)
    kio_ref[0] = ki_rot
    kib_ref[0] = ki_rot.astype(BF16)


def rope_pallas(proj3d, pos, v_transposed=False):
    b, l, _ = proj3d.shape
    tr = min(512, l)
    assert l % tr == 0
    c128, s128 = _rope_tables(pos, DH_D)
    c64, s64 = _rope_tables(pos, DI_D)
    blk = lambda w, j: pl.BlockSpec((1, tr, w), lambda bi, ti: (bi, ti, j))
    tab = pl.BlockSpec((tr, 128), lambda bi, ti: (ti, 0))
    oblk = lambda w: pl.BlockSpec((1, tr, w), lambda bi, ti: (bi, ti, 0))
    shp = lambda w, dt: jax.ShapeDtypeStruct((b, l, w), dt)
    vb_spec, vb_shape = oblk(KV_W), shp(KV_W, BF16)
    if v_transposed:
        vb_spec = pl.BlockSpec((1, KV_W, tr), lambda bi, ti: (bi, 0, ti))
        vb_shape = jax.ShapeDtypeStruct((b, KV_W, l), BF16)
    return pl.pallas_call(
        functools.partial(_rope_kernel, v_transposed=v_transposed),
        grid=(b, l // tr),
        in_specs=[blk(D_D, OD_Q // D_D), blk(KV_W, OD_K // KV_W), blk(KV_W, OD_V // KV_W), blk(QI_W, OD_QI // QI_W),
                  blk(128, OD_KIWI // 128), tab, tab, tab, tab],
        out_specs=[oblk(D_D), oblk(QI_W), oblk(KV_W), oblk(KV_W), oblk(DI_D), oblk(KV_W), vb_spec, oblk(DI_D)],
        out_shape=[shp(D_D, BF16), shp(QI_W, BF16), shp(KV_W, F32), shp(KV_W, F32), shp(DI_D, F32),
                   shp(KV_W, BF16), vb_shape, shp(DI_D, BF16)],
        compiler_params=pltpu.CompilerParams(dimension_semantics=("parallel", "parallel")),
        name="dsa_rope",
    )(proj3d, proj3d, proj3d, proj3d, proj3d, c128, s128, c64, s64)


DSA_TQ = 128
DSA_TK = 512
NEG_BIG = -1e30
DSA_ONES = 16


def _sortable_key(s):
    bits = pltpu.bitcast(s + 0.0, jnp.int32)
    return bits ^ ((bits >> 31) & 0x7FFFFFFF)


def _dsa_prompt_kernel(q_ref, qi_ref, kiwi_ref, gd_ref, k_ref, vt_ref, ki_ref, o_ref, key_ref, *, n_sel):
    tq, tk = DSA_TQ, DSA_TK
    t0 = pl.program_id(1) * tq
    n_kt = (t0 + tq + tk - 1) // tk
    qi = qi_ref[0]
    wi_t = kiwi_ref[0].T[DI_D:DI_D + HI_D, :]
    pos_q = t0 + lax.broadcasted_iota(jnp.int32, (1, tq), 1)
    row_k = lax.broadcasted_iota(jnp.int32, (tk, tq), 0)

    def key_slice(kt):
        return pl.ds(pl.multiple_of(kt * tk, tk), tk)

    def fold_rows(x):
        x = x.reshape(tk // 8, 8, tq)
        while x.shape[0] > 1:
            half = x.shape[0] // 2
            x = x[:half] + x[half:]
        return x[0]

    def score_body(kt, carry):
        ks = key_slice(kt)
        kit = ki_ref[0, ks, :]
        acc = jnp.zeros((tk, tq), F32)
        for h in range(HI_D):
            d = lax.dot_general(kit, qi[:, h * DI_D:(h + 1) * DI_D], (((1,), (1,)), ((), ())),
                                preferred_element_type=F32)
            acc = acc + wi_t[h:h + 1, :] * jnp.maximum(d, 0.0)
        key = _sortable_key(acc * IDX_SCALE)
        key_ref[ks, :] = jnp.where(kt * tk + row_k <= pos_q, key, INT_MIN)
        return carry

    lax.fori_loop(0, n_kt, score_body, 0)

    def count(pred_fn):
        def body(kt, c):
            return c + fold_rows(jnp.where(pred_fn(key_ref[key_slice(kt), :]), 1.0, 0.0))
        c = lax.fori_loop(0, n_kt, body, jnp.zeros((8, tq), F32))
        return jnp.sum(c, axis=0, keepdims=True)

    def bit_body(i, ans):
        cand = ans | lax.shift_left(jnp.int32(1), 31 - i)
        scand = cand ^ INT_MIN
        cnt = count(lambda key: key >= scand)
        return jnp.where(cnt >= n_sel, cand, ans)

    thr = lax.fori_loop(0, 32, bit_body, jnp.zeros((1, tq), jnp.int32)) ^ INT_MIN
    need = n_sel - count(lambda key: key > thr)

    tri = (lax.broadcasted_iota(jnp.int32, (tk, tk), 1) <= lax.broadcasted_iota(jnp.int32, (tk, tk), 0)).astype(BF16)

    def sel_body(kt, tie_seen):
        ks = key_slice(kt)
        key = key_ref[ks, :]
        tie = (key == thr) & (key != INT_MIN)
        tie_f = jnp.where(tie, 1.0, 0.0)
        rank = tie_seen + jnp.dot(tri, tie_f.astype(BF16), preferred_element_type=F32)
        sel = (key > thr) | (tie & (rank <= need))
        key_ref[ks, :] = jnp.where(sel, 1, 0)
        return tie_seen + jnp.sum(tie_f, axis=0, keepdims=True)

    lax.fori_loop(0, n_kt, sel_body, jnp.zeros((1, tq), F32))

    c_exp = (DH_D ** -0.5) * math.log2(math.e)
    ones_rows = jnp.ones((DSA_ONES, tk), BF16)
    qs = [q_ref[0, :, h * DH_D:(h + 1) * DH_D] for h in range(H_D)]

    def att_body(kt, carry):
        m, accs = carry
        ks = key_slice(kt)
        sel = key_ref[ks, :] != 0
        hs = range(H_D)
        kts = [k_ref[0, ks, j * DH_D:(j + 1) * DH_D] for j in range(KVH_D)]
        vts = [jnp.concatenate([vt_ref[0, j * DH_D:(j + 1) * DH_D, ks], ones_rows], axis=0) for j in range(KVH_D)]
        logits = [lax.dot_general(kts[h // QPK_D], qs[h], (((1,), (1,)), ((), ())), preferred_element_type=F32)
                  for h in hs]
        logits = [jnp.where(sel, x, NEG_BIG) for x in logits]
        m_rows = [jnp.maximum(m[h:h + 1, :], jnp.max(x, axis=0, keepdims=True)) for h, x in zip(hs, logits)]
        ps = [jnp.exp2((x - mn) * c_exp).astype(BF16) for x, mn in zip(logits, m_rows)]
        scales = [jnp.exp2((m[h:h + 1, :] - mn) * c_exp) for h, mn in zip(hs, m_rows)]
        pv = [jnp.dot(vts[h // QPK_D], p, preferred_element_type=F32) for h, p in zip(hs, ps)]
        new_accs = [accs[h] * sc + x for h, sc, x in zip(hs, scales, pv)]
        return jnp.concatenate(m_rows, axis=0), tuple(new_accs)

    init = (jnp.full((H_D, tq), NEG_BIG, F32), tuple(jnp.zeros((DH_D + DSA_ONES, tq), F32) for _ in range(H_D)))
    _, accs = lax.fori_loop(0, n_kt, att_body, init)
    out = jnp.concatenate([(a[:DH_D] / a[DH_D:DH_D + 1]).T for a in accs], axis=1)
    gd = gd_ref[0]
    o_ref[0] = out * (gd * _sigmoid(gd))


def dsa_prompt_pallas(proj3d, q_b, qi_b, k_b, vt_b, ki_b):
    b, l, _ = proj3d.shape
    n_sel = min(TOPK_MAX, l // 4)
    assert l % DSA_TQ == 0 and l % DSA_TK == 0
    qblk = lambda w, j: pl.BlockSpec((1, DSA_TQ, w), lambda bi, ti: (bi, ti, j))
    full = lambda w: pl.BlockSpec((1, l, w), lambda bi, ti: (bi, 0, 0))
    return pl.pallas_call(
        functools.partial(_dsa_prompt_kernel, n_sel=float(n_sel)),
        grid=(b, l // DSA_TQ),
        in_specs=[qblk(D_D, 0), qblk(QI_W, 0), qblk(128, OD_KIWI // 128), qblk(D_D, OD_GD // D_D),
                  full(KV_W), pl.BlockSpec((1, KV_W, l), lambda bi, ti: (bi, 0, 0)), full(DI_D)],
        out_specs=pl.BlockSpec((1, DSA_TQ, D_D), lambda bi, ti: (bi, ti, 0)),
        out_shape=jax.ShapeDtypeStruct((b, l, D_D), F32),
        scratch_shapes=[pltpu.VMEM((l, DSA_TQ), jnp.int32)],
        compiler_params=pltpu.CompilerParams(dimension_semantics=("parallel", "arbitrary")),
        name="dsa_prompt",
    )(q_b, qi_b, proj3d, proj3d, k_b, vt_b, ki_b)


S5_N = G_C * P_C
S5_ROWS = 2 * S5_N // 128
S5_TC = 256


def _s5_kernel(u_ref, g_ref, h0re_ref, h0im_ref, lam_t_ref, lam_r_ref, bd_ref, cd_ref, dsk_ref, gw_ref, gb_ref,
               o_ref, hre_ref, him_ref, scr_ref, h_ref, *, bb, tc, n_valid):
    c = pl.program_id(1)
    rows = bb * tc
    half = S5_ROWS // 2

    @pl.when(c == 0)
    def _():
        for i in range(bb):
            h_ref[i * S5_ROWS:i * S5_ROWS + half] = h0re_ref[i]
            h_ref[i * S5_ROWS + half:(i + 1) * S5_ROWS] = h0im_ref[i]

    def discretize(lam_ref):
        lr, li, dt = lam_ref[0], lam_ref[1], jnp.exp(lam_ref[2])
        mag = jnp.exp(lr * dt)
        br, bi = mag * jnp.cos(li * dt), mag * jnp.sin(li * dt)
        den = lr * lr + li * li
        fr = ((br - 1.0) * lr + bi * li) / den
        fi = (bi * lr - (br - 1.0) * li) / den
        return br, bi, fr, fi

    _, _, fr, fi = discretize(lam_r_ref)
    u = u_ref[...].reshape(rows, D_C)
    hu, hn = D_C // 2, S5_N // 2
    b_re, b_im = (jnp.concatenate([_mm(u[:, i * hu:(i + 1) * hu], bd_ref[i * hu:(i + 1) * hu, o + i * hn:o + (i + 1) * hn])
                                   for i in range(2)], axis=1) for o in (0, S5_N))
    bu = jnp.concatenate([fr * b_re - fi * b_im, fr * b_im + fi * b_re], axis=1)
    for r in range(S5_ROWS):
        scr_ref[pl.ds(r, rows, stride=S5_ROWS), :] = bu[:, r * 128:(r + 1) * 128]

    lbr, lbi, _, _ = discretize(lam_t_ref)
    for i in range(bb):
        def step(t, h):
            hr, hi = h
            idx = pl.multiple_of((i * tc + t) * S5_ROWS, S5_ROWS)
            blk = scr_ref[pl.ds(idx, S5_ROWS), :]
            nr = lbr * hr - lbi * hi + blk[:half]
            ni = lbr * hi + lbi * hr + blk[half:]
            scr_ref[pl.ds(idx, S5_ROWS), :] = jnp.concatenate([nr, ni], axis=0)
            return nr, ni

        h_in = (h_ref[i * S5_ROWS:i * S5_ROWS + half], h_ref[i * S5_ROWS + half:(i + 1) * S5_ROWS])
        hr, hi = lax.fori_loop(0, n_valid, step, h_in, unroll=4)
        h_ref[i * S5_ROWS:i * S5_ROWS + half] = hr
        h_ref[i * S5_ROWS + half:(i + 1) * S5_ROWS] = hi
        hre_ref[i] = hr
        him_ref[i] = hi

    h_tiles = [scr_ref[pl.ds(r, rows, stride=S5_ROWS), :] for r in range(S5_ROWS)]
    n_out = D_C // 128
    per = half // n_out
    y_tiles = []
    for t in range(n_out):
        src = [p * half + t * per + i for p in range(2) for i in range(per)]
        y_tiles.append(_mm(jnp.concatenate([h_tiles[r] for r in src], axis=1),
                           jnp.concatenate([cd_ref[r * 128:(r + 1) * 128, t * 128:(t + 1) * 128] for r in src], axis=0)))
    y = dsk_ref[...] * u + jnp.concatenate(y_tiles, axis=1)
    y = 0.5 * y * (1.0 + jnp.tanh(math.sqrt(2.0 / math.pi) * (y + 0.044715 * (y * y * y))))
    y = y * _sigmoid(_mm(y, gw_ref[...]) + gb_ref[...])
    gate = g_ref[...].reshape(rows, D_C)
    o_ref[...] = (y * (gate * _sigmoid(gate))).reshape(bb, tc, D_C)


def s5_pallas(proj3d, n_valid_tokens, h0_re, h0_im, lam_re, lam_im, log_dt, b_re, b_im, c_re, c_im, d_skip,
              glu_w, glu_b):
    b, l, _ = proj3d.shape
    tc = min(S5_TC, l)
    n_chunks = l // tc
    bb = max(1, min(b, S5_TC // tc))
    assert l % tc == 0 and b % bb == 0 and tc % 8 == 0
    n_valid = n_valid_tokens - (n_chunks - 1) * tc
    assert 0 < n_valid <= tc and (n_valid == tc or n_chunks == 1)
    half = S5_ROWS // 2
    dt_full = jnp.repeat(log_dt, P_C)
    lam_t = jnp.stack([lam_re.reshape(half, 128), lam_im.reshape(half, 128), dt_full.reshape(half, 128)])
    lam_r = jnp.stack([lam_re.reshape(1, S5_N), lam_im.reshape(1, S5_N), dt_full.reshape(1, S5_N)])
    eye = jnp.eye(G_C, dtype=F32)
    bd = jnp.concatenate([jnp.einsum('gpm,gh->gmhp', t, eye).reshape(D_C, S5_N) for t in (b_re, b_im)], axis=1)
    cd = jnp.concatenate([jnp.einsum('gmp,gh->gphm', t, eye).reshape(S5_N, D_C) for t in (c_re, -c_im)], axis=0)
    const = lambda shape: pl.BlockSpec(shape, lambda bi, ci: (0,) * len(shape))
    st_spec = pl.BlockSpec((bb, half, 128), lambda bi, ci: (bi, 0, 0))
    out, h_re, h_im = pl.pallas_call(
        functools.partial(_s5_kernel, bb=bb, tc=tc, n_valid=n_valid),
        grid=(b // bb, n_chunks),
        in_specs=[
            pl.BlockSpec((bb, tc, D_C), lambda bi, ci: (bi, ci, OD_U // D_C)),
            pl.BlockSpec((bb, tc, D_C), lambda bi, ci: (bi, ci, OD_GC // D_C)),
            st_spec, st_spec,
            const((3, half, 128)), const((3, 1, S5_N)), const((D_C, 2 * S5_N)), const((2 * S5_N, D_C)),
            const((1, D_C)), const((D_C, D_C)), const((1, D_C)),
        ],
        out_specs=[pl.BlockSpec((bb, tc, D_C), lambda bi, ci: (bi, ci, 0)), st_spec, st_spec],
        out_shape=[jax.ShapeDtypeStruct((b, l, D_C), F32), jax.ShapeDtypeStruct((b, half, 128), F32),
                   jax.ShapeDtypeStruct((b, half, 128), F32)],
        scratch_shapes=[pltpu.VMEM((bb * tc * S5_ROWS, 128), F32), pltpu.VMEM((bb * S5_ROWS, 128), F32)],
        compiler_params=pltpu.CompilerParams(dimension_semantics=("parallel", "arbitrary")),
        name="s5_scan",
    )(proj3d, proj3d, h0_re.reshape(b, half, 128), h0_im.reshape(b, half, 128), lam_t, lam_r,
      bd.astype(BF16), cd.astype(BF16), d_skip.reshape(1, D_C), glu_w.astype(BF16), glu_b.reshape(1, D_C))
    return out, h_re.reshape(b, G_C, P_C), h_im.reshape(b, G_C, P_C)


def _hi_mid_lo(x):
    hi = x.astype(BF16)
    r1 = x - hi.astype(F32)
    mid = r1.astype(BF16)
    lo = (r1 - mid.astype(F32)).astype(BF16)
    return hi, mid, lo


def _mm3_rhs01(x, b_bf16):
    return sum(jnp.dot(p, b_bf16, preferred_element_type=F32) for p in _hi_mid_lo(x))


def _ssd_kernel(z_ref, xbc_ref, dt_ref, conv0_ref, ssm0_ref, cw_ref, cb_ref, sm_ref, pv_ref, e16_ref, bones_ref,
                o_ref, ssm_ref, convn_ref, buf_ref, st_ref, *, cl, n_valid):
    c = pl.program_id(1)
    keep = CONV_W - 1

    @pl.when(c == 0)
    def _():
        buf_ref[8 - keep:8] = conv0_ref[0]
        st_ref[...] = ssm0_ref[0]

    buf_ref[8:8 + cl] = xbc_ref[0]
    cw = cw_ref[...]
    conv = cb_ref[...] + sum(buf_ref[8 - keep + i:8 - keep + i + cl] * cw[i:i + 1] for i in range(CONV_W))
    convn_ref[0] = buf_ref[8 + n_valid - keep:8 + n_valid]
    buf_ref[8 - keep:8] = buf_ref[8 + cl - keep:8 + cl]
    act = conv * _sigmoid(conv)
    xs, bm, cm = act[:, :D_B], act[:, D_B:D_B + G_B * N_B], act[:, D_B + G_B * N_B:]

    sm = sm_ref[...]
    dt = _softplus(dt_ref[0][:, :H_B] + sm[0:1, :H_B])
    row1 = lax.broadcasted_iota(jnp.int32, (cl, 1), 0)
    if n_valid < cl:
        dt = jnp.where(row1 < n_valid, dt, 0.0)
    a = dt * (-jnp.exp(sm[1:2, :H_B]))
    row = lax.broadcasted_iota(jnp.int32, (cl, cl), 0)
    col = lax.broadcasted_iota(jnp.int32, (cl, cl), 1)
    incl = col <= row
    a_parts = _hi_mid_lo(a)
    a_cum = sum(jnp.dot(incl.astype(BF16), p, preferred_element_type=F32) for p in a_parts)
    a_cum_t = sum(lax.dot_general(p, (row <= col).astype(BF16), (((0,), (0,)), ((), ())),
                                  preferred_element_type=F32) for p in a_parts)
    a_last = a_cum[cl - 1:cl]
    e16 = e16_ref[...]
    xdt = xs * _mm3_rhs01(dt, e16)
    ea_full = jnp.exp(_mm3_rhs01(a_cum, e16))
    xdec = xdt * jnp.exp(_mm3_rhs01(a_last - a_cum, e16))
    chunk_decay = jnp.exp(a_last)

    groups = range(G_B)
    gsl = [slice(g * N_B, (g + 1) * N_B) for g in groups]
    cbs = [_mm_nt(cm[:, s], bm[:, s]) for s in gsl]
    y_off = jnp.concatenate(
        [_mm_nt(cm[:, s], st_ref[g * HPG_B:(g + 1) * HPG_B].reshape(HPG_B * P_B, N_B)) for g, s in zip(groups, gsl)],
        axis=1)
    y_diag = []
    for h in range(H_B):
        hs = slice(h * P_B, (h + 1) * P_B)
        lmat = jnp.exp(jnp.where(incl, a_cum[:, h:h + 1] - a_cum_t[h:h + 1, :], NEG_BIG))
        y_diag.append(_mm(cbs[h // HPG_B] * lmat, xdt[:, hs]))
    for h in range(H_B):
        hs = slice(h * P_B, (h + 1) * P_B)
        st_ref[h] = st_ref[h] * chunk_decay[:, h:h + 1] + _mm_tn(xdec[:, hs], bm[:, gsl[h // HPG_B]])

    pv = pv_ref[...]
    y = jnp.concatenate(y_diag, axis=1) + y_off * ea_full + xs * pv[0:1]
    z = z_ref[0]
    y = y * (z * _sigmoid(z))
    ms = _mm_exact_rhs(y * y, bones_ref[...]) * (1.0 / (D_B // G_B))
    o_ref[0] = y * lax.rsqrt(ms + NORM_EPS) * pv[1:2]

    @pl.when(c == pl.num_programs(1) - 1)
    def _():
        ssm_ref[0] = st_ref[...]


def ssd_pallas(proj3d, n_valid_tokens, conv0, ssm0, conv_w, conv_b, dt_bias, a_log, d_skip, gnorm_g):
    b, l, _ = proj3d.shape
    cl = min(SSD_CHUNK, l)
    n_chunks = l // cl
    assert l % cl == 0 and cl % 8 == 0
    n_valid = n_valid_tokens - (n_chunks - 1) * cl
    assert 0 < n_valid <= cl and (n_valid == cl or n_chunks == 1)
    sm = jnp.zeros((8, 128), F32).at[0, :H_B].set(dt_bias).at[1, :H_B].set(a_log)
    pv = jnp.zeros((8, D_B), F32).at[0].set(jnp.repeat(d_skip, P_B)).at[1].set(gnorm_g)
    e16 = (jnp.arange(D_B)[None, :] // P_B == jnp.arange(H_B)[:, None]).astype(BF16)
    gid = jnp.arange(D_B) // (D_B // G_B)
    bones = (gid[:, None] == gid[None, :]).astype(BF16)
    const = lambda shape: pl.BlockSpec(shape, lambda bi, ci: (0,) * len(shape))
    return pl.pallas_call(
        functools.partial(_ssd_kernel, cl=cl, n_valid=n_valid),
        grid=(b, n_chunks),
        in_specs=[
            pl.BlockSpec((1, cl, D_B), lambda bi, ci: (bi, ci, EV_Z // D_B)),
            pl.BlockSpec((1, cl, CONV_DIM), lambda bi, ci: (bi, ci, EV_XBC // CONV_DIM)),
            pl.BlockSpec((1, cl, 128), lambda bi, ci: (bi, ci, EV_DT // 128)),
            pl.BlockSpec((1, CONV_W - 1, CONV_DIM), lambda bi, ci: (bi, 0, 0)),
            pl.BlockSpec((1, H_B, P_B, N_B), lambda bi, ci: (bi, 0, 0, 0)),
            const((CONV_W, CONV_DIM)), const((1, CONV_DIM)), const((8, 128)), const((8, D_B)),
            const((H_B, D_B)), const((D_B, D_B)),
        ],
        out_specs=[
            pl.BlockSpec((1, cl, D_B), lambda bi, ci: (bi, ci, 0)),
            pl.BlockSpec((1, H_B, P_B, N_B), lambda bi, ci: (bi, 0, 0, 0)),
            pl.BlockSpec((1, CONV_W - 1, CONV_DIM), lambda bi, ci: (bi, 0, 0)),
        ],
        out_shape=[jax.ShapeDtypeStruct((b, l, D_B), F32), jax.ShapeDtypeStruct((b, H_B, P_B, N_B), F32),
                   jax.ShapeDtypeStruct((b, CONV_W - 1, CONV_DIM), F32)],
        scratch_shapes=[pltpu.VMEM((8 + cl, CONV_DIM), F32), pltpu.VMEM((H_B, P_B, N_B), F32)],
        compiler_params=pltpu.CompilerParams(dimension_semantics=("parallel", "arbitrary")),
        name="ssd_chunked",
    )(proj3d, proj3d, proj3d, conv0, ssm0, conv_w, conv_b.reshape(1, CONV_DIM), sm, pv, e16, bones)


RADIX_BITS = 4


def _dsa_sample_kernel(pt_ref, q_ref, qi_ref, kiwi_ref, gd_ref, kn_ref, vn_ref, kin_ref, *rest, n_pages, n_sel, tq):
    kp_refs = rest[:n_pages]
    vp_refs = rest[n_pages:2 * n_pages]
    kip_refs = rest[2 * n_pages:3 * n_pages]
    o_ref = rest[3 * n_pages]
    ps = PAGE_SIZE
    n_tiles = n_pages + 1
    pad_rows = lambda x: jnp.concatenate([x, jnp.zeros((ps - tq, x.shape[1]), x.dtype)], axis=0)

    qi = qi_ref[0]
    qs_i = jnp.concatenate([qi[:, h * DI_D:(h + 1) * DI_D] for h in range(HI_D)], axis=0)
    wi = kiwi_ref[0][:, DI_D:DI_D + HI_D]
    wcol = jnp.concatenate([wi[:, h:h + 1] for h in range(HI_D)], axis=0)
    ki_tiles = [r[0].astype(BF16) for r in kip_refs] + [pad_rows(kin_ref[0])]
    scores = []
    for kit in ki_tiles:
        d = wcol * jnp.maximum(_mm_nt(qs_i, kit), 0.0)
        scores.append(sum(d[h * tq:(h + 1) * tq] for h in range(HI_D)))
    key = _sortable_key(jnp.concatenate(scores, axis=1) * IDX_SCALE)
    n_keys = n_tiles * ps
    qrow = lax.broadcasted_iota(jnp.int32, (tq, n_keys), 0)
    kcol = lax.broadcasted_iota(jnp.int32, (tq, n_keys), 1)
    key = jnp.where(kcol - n_pages * ps <= qrow, key, INT_MIN)

    def count(pred):
        return jnp.sum(jnp.where(pred, 1.0, 0.0), axis=1, keepdims=True)

    ans = jnp.zeros((tq, 1), jnp.int32)
    for shift in range(32 - RADIX_BITS, -1, -RADIX_BITS):
        digit = jnp.zeros((tq, 1), jnp.int32)
        for d in range(1, 2 ** RADIX_BITS):
            cand = ans | (d << shift) if d << shift < 2 ** 31 else ans | (d << shift) - 2 ** 32
            digit = digit + jnp.where(count(key >= (cand ^ INT_MIN)) >= n_sel, 1, 0)
        ans = ans | lax.shift_left(digit, shift)
    thr = ans ^ INT_MIN
    gt = key > thr
    need = n_sel - count(gt)
    tie = (key == thr) & (key != INT_MIN)
    tie_f = jnp.where(tie, 1.0, 0.0)
    tri = (lax.broadcasted_iota(jnp.int32, (ps, ps), 0) <= lax.broadcasted_iota(jnp.int32, (ps, ps), 1)).astype(BF16)
    seen = jnp.zeros((tq, 1), F32)
    ranks = []
    for t in range(n_tiles):
        tf = tie_f[:, t * ps:(t + 1) * ps]
        ranks.append(seen + jnp.dot(tf.astype(BF16), tri, preferred_element_type=F32))
        seen = seen + jnp.sum(tf, axis=1, keepdims=True)
    sel = jnp.where(gt | (tie & (jnp.concatenate(ranks, axis=1) <= need)), 1.0, 0.0).astype(BF16)

    krow = lax.broadcasted_iota(jnp.int32, (ps, KVH_D * ps), 0)
    ccol = lax.broadcasted_iota(jnp.int32, (ps, KVH_D * ps), 1)
    spread = [(ccol == KVH_D * krow + j).astype(BF16) for j in range(KVH_D)]
    masks = []
    for t in range(n_pages):
        st = sel[:, t * ps:(t + 1) * ps]
        per_kv = [jnp.dot(st, spread[j], preferred_element_type=F32) for j in range(KVH_D)]
        masks.append(jnp.concatenate([per_kv[h // QPK_D] for h in range(H_D)], axis=0))
    masks.append(jnp.concatenate([sel[:, n_pages * ps:].astype(F32)] * H_D, axis=0))
    mask = jnp.concatenate(masks, axis=1) > 0.5

    q = q_ref[0]
    qs = jnp.concatenate([q[:, h * DH_D:(h + 1) * DH_D] for h in range(H_D)], axis=0)
    kn, vn = pad_rows(kn_ref[0]), pad_rows(vn_ref[0])
    half = QPK_D * tq
    new_logits = jnp.concatenate([_mm_nt(qs[j * half:(j + 1) * half], kn[:, j * DH_D:(j + 1) * DH_D])
                                  for j in range(KVH_D)], axis=0)
    logits = jnp.concatenate([_mm_nt(qs, r[0].astype(BF16)) for r in kp_refs] + [new_logits], axis=1)
    logits = jnp.where(mask, logits * (DH_D ** -0.5), NEG_BIG)
    m = jnp.max(logits, axis=1, keepdims=True)
    p = jnp.where(mask, jnp.exp(logits - m), 0.0)
    l = jnp.sum(p, axis=1, keepdims=True)
    pb = p.astype(BF16)
    w2 = KVH_D * ps
    acc = sum(jnp.dot(pb[:, t * w2:(t + 1) * w2], r[0].astype(BF16), preferred_element_type=F32)
              for t, r in enumerate(vp_refs))
    p_new = pb[:, n_pages * w2:]
    acc = acc + jnp.concatenate([jnp.dot(p_new[j * half:(j + 1) * half], vn[:, j * DH_D:(j + 1) * DH_D],
                                         preferred_element_type=F32) for j in range(KVH_D)], axis=0)
    o = acc / l
    gd = gd_ref[0]
    o_ref[0] = jnp.concatenate([o[h * tq:(h + 1) * tq] for h in range(H_D)], axis=1) * (gd * _sigmoid(gd))


def dsa_sample_pallas(proj3d, q_b, qi_b, k_b, v_b, ki_b, cache_k, cache_v, cache_ki, page_table, n_valid_tokens):
    b, tq, _ = proj3d.shape
    n_pages = page_table.shape[1]
    n_sel = min(TOPK_MAX, (n_pages * PAGE_SIZE + n_valid_tokens) // 4)
    row = lambda w, j=0: pl.BlockSpec((1, tq, w), lambda bi, pt, j=j: (bi, 0, j))
    kv_page = lambda p: pl.BlockSpec((1, PAGE_SIZE * KVH_D, DH_D), lambda bi, pt, p=p: (pt[bi, p], 0, 0))
    ki_page = lambda p: pl.BlockSpec((1, PAGE_SIZE, DI_D), lambda bi, pt, p=p: (pt[bi, p], 0, 0))
    grid_spec = pltpu.PrefetchScalarGridSpec(
        num_scalar_prefetch=1,
        grid=(b,),
        in_specs=[row(D_D), row(QI_W), row(128, OD_KIWI // 128), row(D_D, OD_GD // D_D), row(KV_W), row(KV_W), row(DI_D)]
        + [kv_page(p) for p in range(n_pages)] + [kv_page(p) for p in range(n_pages)]
        + [ki_page(p) for p in range(n_pages)],
        out_specs=pl.BlockSpec((1, tq, D_D), lambda bi, pt: (bi, 0, 0)),
    )
    return pl.pallas_call(
        functools.partial(_dsa_sample_kernel, n_pages=n_pages, n_sel=float(n_sel), tq=tq),
        grid_spec=grid_spec,
        out_shape=jax.ShapeDtypeStruct((b, tq, D_D), F32),
        compiler_params=pltpu.CompilerParams(dimension_semantics=("parallel",)),
        name="dsa_sample",
    )(page_table, q_b, qi_b, proj3d, proj3d, k_b, v_b, ki_b, *([cache_k] * n_pages), *([cache_v] * n_pages),
      *([cache_ki] * n_pages))


def _split(x, sizes):
    return jnp.split(x, np.cumsum(sizes)[:-1].tolist(), axis=-1)


def even_mixer(x, g, lv, shift0, wkv0, conv0, ssm0, w_in, w_out, mu, w0, w2, a0, a2, k_k, k_a, r_k,
               ln_g, ln_b, conv_w, conv_b, dt_bias, a_log, d_skip, gnorm_g):
    b, l, d = x.shape
    pa_w, ga_w, z_w, xbc_w, dt_w = _split(w_in, [A_SHIFT, D_A, D_B, CONV_DIM, H_B])
    w_ev = jnp.concatenate([ga_w, z_w, xbc_w, pa_w, dt_w, jnp.zeros((d, EV_N - IN_E), w_in.dtype)], axis=1)
    proj = norm_matmul(x.reshape(b * l, d), g, w_ev.astype(BF16)).reshape(b, l, EV_N)
    out_a, wkv_new, shift_new = rwkv7_pallas(proj, shift0, wkv0, mu, w0, w2, a0, a2, k_k, k_a, r_k, ln_g, ln_b, lv)
    out_b, ssm_new, conv_new = ssd_pallas(proj, lv, conv0, ssm0, conv_w, conv_b, dt_bias, a_log, d_skip, gnorm_g)
    x_new = matmul_residual(out_a.reshape(b * l, D_A), out_b.reshape(b * l, D_B), w_out.astype(BF16),
                            x.reshape(b * l, d)).reshape(b, l, d)
    return x_new, (wkv_new, shift_new, ssm_new, conv_new)


def odd_mixer(x, g, lv, pos, c_re0, c_im0, attend, w_in, w_out, lam_re, lam_im, log_dt, b_re, b_im,
              c_re, c_im, d_skip, glu_w, glu_b):
    b, l, d = x.shape
    u_w, gc_w, q_w, k_w, v_w, qi_w, ki_w, wi_w, gd_w = _split(
        w_in, [D_C, D_C, D_D, KV_W, KV_W, QI_W, DI_D, HI_D, D_D])
    w_od = jnp.concatenate([u_w, gc_w, q_w, gd_w, k_w, v_w, qi_w, ki_w, wi_w,
                            jnp.zeros((d, OD_N - IN_O), w_in.dtype)], axis=1)
    proj = norm_matmul(x.reshape(b * l, d), g, w_od.astype(BF16)).reshape(b, l, OD_N)
    out_c, re_last, im_last = s5_pallas(proj, lv, c_re0, c_im0, lam_re, lam_im, log_dt, b_re, b_im,
                                        c_re, c_im, d_skip, glu_w, glu_b)
    q_b, qi_b, k_f, v_f, ki, k_b, v_b, ki_b = rope_pallas(proj, pos, v_transposed=attend is None)
    if attend is None:
        out_d = dsa_prompt_pallas(proj, q_b, qi_b, k_b, v_b, ki_b)
    else:
        out_d = dsa_sample_pallas(proj, q_b, qi_b, k_b, v_b, ki_b, *attend, lv)
    k = k_f[:, :lv].reshape(b, lv, KVH_D, DH_D)
    v = v_f[:, :lv].reshape(b, lv, KVH_D, DH_D)
    ki = ki[:, :lv]
    x_new = matmul_residual(out_c.reshape(b * l, D_C), out_d.reshape(b * l, D_D), w_out.astype(BF16),
                            x.reshape(b * l, d)).reshape(b, l, d)
    return x_new, (re_last, im_last, k, v, ki)


def kernel(x_prompt, x_sample, state_a_wkv, state_a_shift, state_b_ssm, state_b_conv, state_c_re, state_c_im, cache_d_k, cache_d_v, cache_d_kidx, page_table, norm_g, final_norm_g, w_in_e, w_out_e, rwkv_mu, rwkv_w0, rwkv_w2, rwkv_a0, rwkv_a2, rwkv_kk, rwkv_ka, rwkv_rk, rwkv_ln_g, rwkv_ln_b, ssd_conv_w, ssd_conv_b, ssd_dt_bias, ssd_a_log, ssd_d, ssd_norm_g, w_in_o, w_out_o, s5_lam_re, s5_lam_im, s5_log_dt, s5_b_re, s5_b_im, s5_c_re, s5_c_im, s5_d, s5_glu_w, s5_glu_b):
    f32 = jnp.float32
    bp, lp = x_prompt.shape[:2]
    bs, ls = x_sample.shape[:2]
    pos_p = jnp.arange(lp)
    ls_pad = _round_up(ls, 8)
    pos_s = PAST_LEN + jnp.arange(ls_pad)
    xp, xs = x_prompt, jnp.pad(x_sample, ((0, 0), (0, ls_pad - ls), (0, 0)))
    even_p, even_s, odd_p, odd_s = [], [], [], []
    for i in range(DEPTH):
        j = i // 2
        if i % 2 == 0:
            pe = (w_in_e[j], w_out_e[j], rwkv_mu[j], rwkv_w0[j], rwkv_w2[j], rwkv_a0[j], rwkv_a2[j],
                  rwkv_kk[j], rwkv_ka[j], rwkv_rk[j], rwkv_ln_g[j], rwkv_ln_b[j], ssd_conv_w[j],
                  ssd_conv_b[j], ssd_dt_bias[j], ssd_a_log[j], ssd_d[j], ssd_norm_g[j])
            xp, st_p = even_mixer(xp, norm_g[i], lp, jnp.zeros((bp, A_SHIFT), f32), jnp.zeros((bp, H_A, HD_A, HD_A), f32),
                                  jnp.zeros((bp, CONV_W - 1, CONV_DIM), f32),
                                  jnp.zeros((bp, H_B, P_B, N_B), f32), *pe)
            xs, st_s = even_mixer(xs, norm_g[i], ls, state_a_shift[j], state_a_wkv[j], state_b_conv[j], state_b_ssm[j], *pe)
            even_p.append(st_p)
            even_s.append(st_s)
        else:
            po = (w_in_o[j], w_out_o[j], s5_lam_re[j], s5_lam_im[j], s5_log_dt[j], s5_b_re[j], s5_b_im[j],
                  s5_c_re[j], s5_c_im[j], s5_d[j], s5_glu_w[j], s5_glu_b[j])
            zc = jnp.zeros((bp, G_C, P_C), f32)
            xp, st_p = odd_mixer(xp, norm_g[i], lp, pos_p, zc, zc, None, *po)
            n_pool = cache_d_k.shape[1]
            attend_s = (cache_d_k.reshape(-1, PAGE_SIZE * KVH_D, DH_D), cache_d_v.reshape(-1, PAGE_SIZE * KVH_D, DH_D),
                        cache_d_kidx.reshape(-1, PAGE_SIZE, DI_D), page_table + j * n_pool)
            xs, st_s = odd_mixer(xs, norm_g[i], ls, pos_s, state_c_re[j], state_c_im[j], attend_s, *po)
            odd_p.append(st_p)
            odd_s.append(st_s)
    y_prompt = rmsnorm_rows(xp.reshape(bp * lp, D_MODEL), final_norm_g).reshape(bp, lp, D_MODEL)
    y_sample = rmsnorm_rows(xs.reshape(bs * ls_pad, D_MODEL), final_norm_g).reshape(bs, ls_pad, D_MODEL)[:, :ls]
    new_a_wkv_p, new_a_shift_p, new_b_ssm_p, new_b_conv_p = [jnp.stack(t) for t in zip(*even_p)]
    new_a_wkv_s, new_a_shift_s, new_b_ssm_s, new_b_conv_s = [jnp.stack(t) for t in zip(*even_s)]
    new_c_re_p, new_c_im_p, new_d_k_p, new_d_v_p, new_d_kidx_p = [jnp.stack(t) for t in zip(*odd_p)]
    new_c_re_s, new_c_im_s, new_d_k_s, new_d_v_s, new_d_kidx_s = [jnp.stack(t) for t in zip(*odd_s)]
    return (y_prompt, y_sample,
            new_a_wkv_p, new_a_shift_p, new_b_ssm_p, new_b_conv_p,
            new_c_re_p, new_c_im_p, new_d_k_p, new_d_v_p, new_d_kidx_p,
            new_a_wkv_s, new_a_shift_s, new_b_ssm_s, new_b_conv_s,
            new_c_re_s, new_c_im_s, new_d_k_s, new_d_v_s, new_d_kidx_s)
```

```python
import functools
import math

import jax
import jax.numpy as jnp
import numpy as np
from jax import lax
from jax.experimental import pallas as pl
from jax.experimental.pallas import tpu as pltpu

D_MODEL = 1024
DEPTH = 4
PAST_LEN = 2048
PAGE_SIZE = 128
NORM_EPS = 1e-6

D_A = D_MODEL
HD_A = 64
H_A = D_A // HD_A
W_LORA = 64
A_LORA = 64
A_SHIFT = 3 * D_A + W_LORA + A_LORA
RWKV_LN_EPS = 64e-5
D_B = D_MODEL
P_B = 64
H_B = D_B // P_B
N_B = 128
G_B = 4
HPG_B = H_B // G_B
CONV_W = 4
CONV_DIM = D_B + 2 * G_B * N_B
SSD_CHUNK = 128
D_C = D_MODEL // 2
CH_C = 16
G_C = D_C // CH_C
P_C = 64
H_D = 8
DH_D = 128
KVH_D = 2
QPK_D = H_D // KVH_D
D_D = H_D * DH_D
HI_D = 8
DI_D = 64
IDX_SCALE = (DI_D ** -0.5) * (HI_D ** -0.5)
TOPK_MAX = 256
QBLK = 128
ROPE_THETA = 500000.0
ROPE_FRAC = 4

IN_E = A_SHIFT + D_A + D_B + CONV_DIM + H_B
OUT_E = D_A + D_B
IN_O = 2 * D_C + D_D + 2 * KVH_D * DH_D + HI_D * DI_D + DI_D + HI_D + D_D
OUT_O = D_C + D_D

F32 = jnp.float32
BF16 = jnp.bfloat16

TILE_M = 512
TILE_N = 512
PROJ_TILE_M = 1024


def _round_up(n, m):
    return (n + m - 1) // m * m


def _norm_matmul_kernel(x_ref, g_ref, w_ref, o_ref, h_ref):
    @pl.when(pl.program_id(1) == 0)
    def _():
        x = x_ref[...]
        ms = jnp.mean(x * x, axis=-1, keepdims=True)
        h_ref[...] = (x * lax.rsqrt(ms + NORM_EPS) * g_ref[...]).astype(BF16)

    o_ref[...] = jnp.dot(h_ref[...], w_ref[...], preferred_element_type=F32)


def norm_matmul(x2d, g, w_bf16, tn=TILE_N):
    m, d = x2d.shape
    n = w_bf16.shape[1]
    tm = min(PROJ_TILE_M, m)
    assert m % tm == 0 and n % tn == 0
    return pl.pallas_call(
        _norm_matmul_kernel,
        grid=(m // tm, n // tn),
        in_specs=[
            pl.BlockSpec((tm, d), lambda i, j: (i, 0)),
            pl.BlockSpec((1, d), lambda i, j: (0, 0)),
            pl.BlockSpec((d, tn), lambda i, j: (0, j)),
        ],
        out_specs=pl.BlockSpec((tm, tn), lambda i, j: (i, j)),
        out_shape=jax.ShapeDtypeStruct((m, n), F32),
        scratch_shapes=[pltpu.VMEM((tm, d), BF16)],
        compiler_params=pltpu.CompilerParams(dimension_semantics=("parallel", "arbitrary")),
        name="norm_matmul",
    )(x2d, g.reshape(1, d), w_bf16)


def _matmul_res_kernel(a1_ref, a2_ref, w1_ref, w2_ref, r_ref, o_ref):
    o_ref[...] = (r_ref[...] + jnp.dot(a1_ref[...].astype(BF16), w1_ref[...], preferred_element_type=F32)
                  + jnp.dot(a2_ref[...].astype(BF16), w2_ref[...], preferred_element_type=F32))


def matmul_residual(a1, a2, w_bf16, res2d):
    m, k1 = a1.shape
    k2 = a2.shape[1]
    n = w_bf16.shape[1]
    tm = min(TILE_M, m)
    assert m % tm == 0 and w_bf16.shape[0] == k1 + k2
    return pl.pallas_call(
        _matmul_res_kernel,
        grid=(m // tm,),
        in_specs=[
            pl.BlockSpec((tm, k1), lambda i: (i, 0)),
            pl.BlockSpec((tm, k2), lambda i: (i, 0)),
            pl.BlockSpec((k1, n), lambda i: (0, 0)),
            pl.BlockSpec((k2, n), lambda i: (0, 0)),
            pl.BlockSpec((tm, n), lambda i: (i, 0)),
        ],
        out_specs=pl.BlockSpec((tm, n), lambda i: (i, 0)),
        out_shape=jax.ShapeDtypeStruct((m, n), F32),
        compiler_params=pltpu.CompilerParams(dimension_semantics=("parallel",)),
        name="matmul_residual",
    )(a1, a2, w_bf16[:k1], w_bf16[k1:], res2d)


def _rmsnorm_kernel(x_ref, g_ref, o_ref):
    x = x_ref[...]
    ms = jnp.mean(x * x, axis=-1, keepdims=True)
    o_ref[...] = x * lax.rsqrt(ms + NORM_EPS) * g_ref[...]


def rmsnorm_rows(x2d, g):
    m, d = x2d.shape
    tm = min(TILE_M, m)
    return pl.pallas_call(
        _rmsnorm_kernel,
        grid=(m // tm,),
        in_specs=[pl.BlockSpec((tm, d), lambda i: (i, 0)), pl.BlockSpec((1, d), lambda i: (0, 0))],
        out_specs=pl.BlockSpec((tm, d), lambda i: (i, 0)),
        out_shape=jax.ShapeDtypeStruct((m, d), F32),
        compiler_params=pltpu.CompilerParams(dimension_semantics=("parallel",)),
        name="final_rmsnorm",
    )(x2d, g.reshape(1, d))


def _mm(a, b):
    return jnp.dot(a.astype(BF16), b.astype(BF16), preferred_element_type=F32)


def _mm_nt(a, b):
    return lax.dot_general(a.astype(BF16), b.astype(BF16), (((1,), (1,)), ((), ())), preferred_element_type=F32)


def _mm_tn(a, b):
    return lax.dot_general(a.astype(BF16), b.astype(BF16), (((0,), (0,)), ((), ())), preferred_element_type=F32)


def _hi_lo(x):
    hi = x.astype(BF16)
    lo = (x - hi.astype(F32)).astype(BF16)
    return hi, lo


def _mm_exact_lhs(a_bf16, x):
    hi, lo = _hi_lo(x)
    return (jnp.dot(a_bf16, hi, preferred_element_type=F32) + jnp.dot(a_bf16, lo, preferred_element_type=F32))


def _mm_exact_rhs(x, b_bf16):
    hi, lo = _hi_lo(x)
    return (jnp.dot(hi, b_bf16, preferred_element_type=F32) + jnp.dot(lo, b_bf16, preferred_element_type=F32))


def _softplus(x):
    return jnp.maximum(x, 0.0) + jnp.log1p(jnp.exp(-jnp.abs(x)))


def _sigmoid(x):
    return 1.0 / (1.0 + jnp.exp(-x))


def _unit_lower_inverses(a_list, row, col, n):
    eye = jnp.where(row == col, 1.0, 0.0).astype(F32)
    ts = [eye for _ in a_list]
    m = 1
    while m < n:
        in_pair = (row // (2 * m)) == (col // (2 * m))
        lvl = in_pair & ((row % (2 * m)) >= m) & ((col % (2 * m)) < m)
        ls = [jnp.where(lvl, a, 0.0) for a in a_list]
        if m == 1:
            ts = [t - l for t, l in zip(ts, ls)]
        else:
            tl = [_mm(t, l) for t, l in zip(ts, ls)]
            ts = [t - _mm(x, t) for t, x in zip(ts, tl)]
        m *= 2
    return ts


EV_GA, EV_Z, EV_XBC, EV_R, EV_K, EV_V, EV_WA, EV_DT = 0, 1024, 2048, 4096, 5120, 6144, 7168, 7296
EV_N = 7680
RWKV_CHUNK = 64


RWKV_GROUP = 4
RWKV_BATCH = 4


def _block_rows(x, n, mask):
    return jnp.where(mask, jnp.concatenate([x] * n, axis=0), 0.0)


def _wkv_groups(alpha, beta, kappa, rho, v, kappa_e, beta_e, wc, st_ref, C, bb):
    gh, hd = RWKV_GROUP, HD_A
    gw, tw = gh * hd, gh * C
    iota = lambda shape, d: lax.broadcasted_iota(jnp.int32, shape, d)
    row, cin = iota((C, tw), 0), iota((C, tw), 1) % C
    strict, incl = cin < row, cin <= row
    m_tt = iota((tw, tw), 0) // C == iota((tw, tw), 1) // C
    m_tk = iota((tw, gw), 0) // C == iota((tw, gw), 1) // hd
    m_kk = iota((gw, gw), 0) // hd == iota((gw, gw), 1) // hd
    probs = [(b, slice(b * C, (b + 1) * C), slice(g * gw, (g + 1) * gw)) for b in range(bb) for g in range(H_A // gh)]
    s0s = [st_ref[b, :, sl] for b, _, sl in probs]
    bd_s = [_block_rows(s0, gh, m_kk) for s0 in s0s]
    bd_v = [_block_rows(v[rs, sl], gh, m_tk) for _, rs, sl in probs]
    grams = [_mm_nt(jnp.concatenate([alpha[rs, sl], rho[rs, sl]], axis=0),
                    jnp.concatenate([_block_rows(beta[rs, sl], gh, m_tk), _block_rows(kappa[rs, sl], gh, m_tk)],
                                    axis=0)) for _, rs, sl in probs]
    a_bs = [jnp.where(strict, g[:C, :tw], 0.0) for g in grams]
    ts = [jnp.where(cin == row, 1.0, 0.0).astype(F32) for _ in probs]
    m = 1
    while m < C:
        lvl = ((row // (2 * m)) == (cin // (2 * m))) & ((row % (2 * m)) >= m) & ((cin % (2 * m)) < m)
        ls = [jnp.where(lvl, a, 0.0) for a in a_bs]
        if m == 1:
            ts = [t - l for t, l in zip(ts, ls)]
        else:
            tl = [_mm(t, _block_rows(l, gh, m_tt)) for t, l in zip(ts, ls)]
            ts = [t - _mm(x, _block_rows(t, gh, m_tt)) for t, x in zip(ts, tl)]
        m *= 2
    rhss = [_mm_nt(alpha[rs, sl], bs) + _mm(jnp.where(strict, g[:C, tw:], 0.0), bv)
            for (_, rs, sl), bs, bv, g in zip(probs, bd_s, bd_v, grams)]
    us = [_mm(t, _block_rows(x, gh, m_tk)) for t, x in zip(ts, rhss)]
    outs = [_mm_nt(rho[rs, sl], bs) + _mm(jnp.where(incl, g[C:, tw:], 0.0), bv)
            - _mm(jnp.where(incl, g[C:, :tw], 0.0), _block_rows(u, gh, m_tk))
            for (_, rs, sl), bs, bv, g, u in zip(probs, bd_s, bd_v, grams, us)]
    for (b, rs, sl), s0, u in zip(probs, s0s, us):
        x = jnp.concatenate([v[rs, sl], -u], axis=0)
        y = jnp.concatenate([kappa_e[rs, sl], beta_e[rs, sl]], axis=0)
        cross = jnp.where(m_kk, _mm_tn(x, y), 0.0)
        st_ref[b, :, sl] = s0 * wc[rs, sl][:1] + sum(cross[h * hd:(h + 1) * hd] for h in range(gh))
    n_g = H_A // gh
    return jnp.concatenate([jnp.concatenate(outs[b * n_g:(b + 1) * n_g], axis=1) for b in range(bb)], axis=0)


def _wkv_heads(alpha, beta, kappa, rho, v, kappa_e, beta_e, wc, st_ref, C, bb):
    row = lax.broadcasted_iota(jnp.int32, (C, C), 0)
    col = lax.broadcasted_iota(jnp.int32, (C, C), 1)
    incl, strict = col <= row, col < row
    probs = [(b, slice(b * C, (b + 1) * C), slice(h * HD_A, (h + 1) * HD_A)) for b in range(bb) for h in range(H_A)]
    s0s = [st_ref[b, :, sl] for b, _, sl in probs]
    bks = [jnp.concatenate([beta[rs, sl], kappa[rs, sl]], axis=0) for _, rs, sl in probs]
    g_as = [_mm_nt(alpha[rs, sl], bk) for (_, rs, sl), bk in zip(probs, bks)]
    g_rs = [_mm_nt(rho[rs, sl], bk) for (_, rs, sl), bk in zip(probs, bks)]
    t_invs = _unit_lower_inverses([jnp.where(strict, g[:, :C], 0.0) for g in g_as], row, col, C)
    rhss = [_mm_nt(alpha[rs, sl], s0) + _mm(jnp.where(strict, g[:, C:], 0.0), v[rs, sl])
            for (_, rs, sl), s0, g in zip(probs, s0s, g_as)]
    us = [_mm(t, x) for t, x in zip(t_invs, rhss)]
    outs = [_mm_nt(rho[rs, sl], s0) + _mm(jnp.where(incl, g[:, C:], 0.0), v[rs, sl])
            - _mm(jnp.where(incl, g[:, :C], 0.0), u) for (_, rs, sl), s0, g, u in zip(probs, s0s, g_rs, us)]
    for (b, rs, sl), s0, u in zip(probs, s0s, us):
        x = jnp.concatenate([v[rs, sl], -u], axis=0)
        y = jnp.concatenate([kappa_e[rs, sl], beta_e[rs, sl]], axis=0)
        st_ref[b, :, sl] = s0 * wc[rs, sl][:1] + _mm_tn(x, y)
    return jnp.concatenate([jnp.concatenate(outs[b * H_A:(b + 1) * H_A], axis=1) for b in range(bb)], axis=0)


def _rwkv_kernel(r_ref, k_ref, v_ref, wa_ref, g_ref, sh0_ref, s0_ref, mu_ref, pv_ref, w2_ref, a2_ref, bones_ref,
                 o_ref, snew_ref, shnew_ref, st_ref, carry_ref, *, chunk, n_valid, bb):
    C = chunk
    R = bb * C
    c = pl.program_id(1)

    @pl.when(c == 0)
    def _():
        carry_ref[...] = sh0_ref[...]
        for b in range(bb):
            for h in range(H_A):
                st_ref[b, :, h * HD_A:(h + 1) * HD_A] = s0_ref[b, h]

    mu = mu_ref[...]
    pv = pv_ref[...]
    w0, a0, k_k, k_a, r_k, ln_g, ln_b = (pv[i:i + 1] for i in range(7))
    row8 = lax.broadcasted_iota(jnp.int32, (8, 1), 0)
    t_in_chunk = lax.broadcasted_iota(jnp.int32, (R, 1), 0) % C

    def tok_shift(x, lo, hi):
        rolled = pltpu.roll(x, 1, 0)
        pieces = []
        for b in range(bb):
            pieces.append(jnp.where(row8 == 0, carry_ref[b, :, lo:hi], rolled[b * C:b * C + 8]))
            if C > 8:
                pieces.append(rolled[b * C + 8:(b + 1) * C])
        return x + (jnp.concatenate(pieces, axis=0) - x) * mu[:, lo:hi]

    pr, pk, pvv, pwa = (ref[...].reshape(R, ref.shape[-1]) for ref in (r_ref, k_ref, v_ref, wa_ref))
    r = tok_shift(pr, 0, D_A)
    k = tok_shift(pk, D_A, 2 * D_A)
    v = tok_shift(pvv, 2 * D_A, 3 * D_A)
    wa = tok_shift(pwa, 3 * D_A, A_SHIFT)
    for b in range(bb):
        last = b * C + n_valid - 1
        new_carry = jnp.concatenate([x[last:last + 1] for x in (pr, pk, pvv, pwa)], axis=1)
        carry_ref[b] = new_carry
        shnew_ref[b] = new_carry

    wl, al = wa[:, :W_LORA], wa[:, W_LORA:]
    w_log = -_softplus(-(w0 + _mm(jnp.tanh(wl), w2_ref[...]))) - 0.5
    logw = -jnp.exp(w_log)
    a = _sigmoid(a0 + _mm(al, a2_ref[...]))
    bones = bones_ref[...]
    kk = k * k_k
    kk = kk / jnp.maximum(jnp.sqrt(_mm_exact_rhs(kk * kk, bones)), 1e-12)
    kp = k * (1.0 + (a - 1.0) * k_a)
    bonus = _mm_exact_rhs(r * kp * r_k, bones) * v
    if n_valid < C:
        ok = t_in_chunk < n_valid
        logw = jnp.where(ok, logw, 0.0)
        kk = jnp.where(ok, kk, 0.0)
        kp = jnp.where(ok, kp, 0.0)

    row = lax.broadcasted_iota(jnp.int32, (R, R), 0)
    col = lax.broadcasted_iota(jnp.int32, (R, R), 1)
    same = (row // C) == (col // C)
    cum = _mm_exact_lhs((same & (col <= row)).astype(BF16), logw)
    tot = _mm_exact_lhs(same.astype(BF16), logw)
    eneg = jnp.exp(-cum)
    alpha = kk * jnp.exp(cum - logw)
    ka = kk * a
    beta = ka * eneg
    kappa = kp * eneg
    rho = r * jnp.exp(cum)
    dec_end = jnp.exp(tot - cum)
    kappa_e = kp * dec_end
    beta_e = ka * dec_end
    wc = jnp.exp(tot)

    solve = _wkv_groups if RWKV_GROUP * C % 128 == 0 else _wkv_heads
    out = solve(alpha, beta, kappa, rho, v, kappa_e, beta_e, wc, st_ref, C, bb)

    mean = _mm_exact_rhs(out, bones) * (1.0 / HD_A)
    d = out - mean
    var = _mm_exact_rhs(d * d, bones) * (1.0 / HD_A)
    y = d * lax.rsqrt(var + RWKV_LN_EPS) * ln_g + ln_b
    gate = g_ref[...].reshape(R, D_A)
    o_ref[...] = ((y + bonus) * (gate * _sigmoid(gate))).reshape(bb, C, D_A)
    for b in range(bb):
        for h in range(H_A):
            snew_ref[b, h] = st_ref[b, :, h * HD_A:(h + 1) * HD_A]


def rwkv7_pallas(proj3d, shift0, wkv0, mu, w0, w2, a0, a2, k_k, k_a, r_k, ln_g, ln_b, n_valid_tokens):
    b, l, _ = proj3d.shape
    chunk = min(RWKV_CHUNK, l)
    n_chunks = l // chunk
    bb = math.gcd(b, RWKV_BATCH)
    assert l % chunk == 0 and chunk % 8 == 0
    n_valid = n_valid_tokens - (n_chunks - 1) * chunk
    assert 0 < n_valid <= chunk and (n_valid == chunk or n_chunks == 1)
    pvec = jnp.stack([w0, a0, k_k, k_a, r_k.reshape(D_A), ln_g, ln_b, jnp.zeros_like(w0)])
    hid = jnp.arange(D_A) // HD_A
    bones = (hid[:, None] == hid[None, :]).astype(BF16)
    blk = lambda w, j: pl.BlockSpec((bb, chunk, w), lambda bi, ci: (bi, ci, j))
    const = lambda shape: pl.BlockSpec(shape, lambda bi, ci: (0,) * len(shape))
    out, wkv_new, shift_new = pl.pallas_call(
        functools.partial(_rwkv_kernel, chunk=chunk, n_valid=n_valid, bb=bb),
        grid=(b // bb, n_chunks),
        in_specs=[
            blk(D_A, EV_R // D_A), blk(D_A, EV_K // D_A), blk(D_A, EV_V // D_A), blk(128, EV_WA // 128),
            blk(D_A, EV_GA // D_A),
            pl.BlockSpec((bb, 1, A_SHIFT), lambda bi, ci: (bi, 0, 0)),
            pl.BlockSpec((bb, H_A, HD_A, HD_A), lambda bi, ci: (bi, 0, 0, 0)),
            const((1, A_SHIFT)), const((8, D_A)), const((W_LORA, D_A)), const((A_LORA, D_A)), const((D_A, D_A)),
        ],
        out_specs=[
            pl.BlockSpec((bb, chunk, D_A), lambda bi, ci: (bi, ci, 0)),
            pl.BlockSpec((bb, H_A, HD_A, HD_A), lambda bi, ci: (bi, 0, 0, 0)),
            pl.BlockSpec((bb, 1, A_SHIFT), lambda bi, ci: (bi, 0, 0)),
        ],
        out_shape=[
            jax.ShapeDtypeStruct((b, l, D_A), F32),
            jax.ShapeDtypeStruct((b, H_A, HD_A, HD_A), F32),
            jax.ShapeDtypeStruct((b, 1, A_SHIFT), F32),
        ],
        scratch_shapes=[pltpu.VMEM((bb, HD_A, D_A), F32), pltpu.VMEM((bb, 1, A_SHIFT), F32)],
        compiler_params=pltpu.CompilerParams(dimension_semantics=("parallel", "arbitrary")),
        name="rwkv7_chunked",
    )(proj3d, proj3d, proj3d, proj3d, proj3d, shift0.reshape(b, 1, A_SHIFT), wkv0,
      mu.reshape(1, A_SHIFT), pvec, w2.astype(BF16), a2.astype(BF16), bones)
    return out, wkv_new, shift_new.reshape(b, A_SHIFT)


OD_U, OD_GC, OD_Q, OD_GD, OD_K, OD_V, OD_QI, OD_KIWI = 0, 512, 1024, 2048, 3072, 3328, 3584, 4096
OD_N = 4608
KV_W = KVH_D * DH_D
QI_W = HI_D * DI_D
INT_MIN = -2 ** 31


def _rope_tables(pos, head_dim):
    rd = head_dim // ROPE_FRAC
    half = rd // 2
    inv_freq = ROPE_THETA ** (-jnp.arange(half, dtype=F32) / half)
    ang = pos.astype(F32)[:, None] * inv_freq[None, :]
    cos, sin = jnp.cos(ang), jnp.sin(ang)
    t = pos.shape[0]
    ones = jnp.ones((t, head_dim - rd), F32)
    cos_h = jnp.concatenate([cos, cos, ones], axis=1)
    sin_h = jnp.concatenate([-sin, sin, 0.0 * ones], axis=1)
    reps = 128 // head_dim
    return jnp.tile(cos_h, (1, reps)), jnp.tile(sin_h, (1, reps))


def _rotate(x, cos_t, sin_t, head_dim):
    w = x.shape[1]
    half = head_dim // ROPE_FRAC // 2
    lane = lax.broadcasted_iota(jnp.int32, x.shape, 1) % head_dim
    if w >= 128:
        cos_f = jnp.tile(cos_t, (1, w // 128))
        sin_f = jnp.tile(sin_t, (1, w // 128))
        partner = jnp.where(lane < half, pltpu.roll(x, w - half, 1), pltpu.roll(x, half, 1))
    else:
        cos_f, sin_f = cos_t[:, :w], sin_t[:, :w]
        partner = jnp.where(lane < half, jnp.concatenate([x[:, half:], x[:, :half]], axis=1),
                            jnp.concatenate([x[:, w - half:], x[:, :w - half]], axis=1))
    return x * cos_f + partner * sin_f


def _rope_kernel(q_ref, k_ref, v_ref, qi_ref, kiwi_ref, c128_ref, s128_ref, c64_ref, s64_ref,
                 qo_ref, qio_ref, ko_ref, vo_ref, kio_ref, kb_ref, vb_ref, kib_ref, *, v_transposed):
    c128, s128, c64, s64 = c128_ref[...], s128_ref[...], c64_ref[...], s64_ref[...]
    qo_ref[0] = _rotate(q_ref[0], c128, s128, DH_D).astype(BF16)
    qio_ref[0] = _rotate(qi_ref[0], c64, s64, DI_D).astype(BF16)
    k_rot = _rotate(k_ref[0], c128, s128, DH_D)
    ko_ref[0] = k_rot
    kb_ref[0] = k_rot.astype(BF16)
    v = v_ref[0]
    vo_ref[0] = v
    vb_ref[0] = (v.T if v_transposed else v).astype(BF16)
    ki_rot = _rotate(kiwi_ref[0][:, :DI_D], c64, s64, DI_D)
    kio_ref[0] = ki_rot
    kib_ref[0] = ki_rot.astype(BF16)


def rope_pallas(proj3d, pos, v_transposed=False):
    b, l, _ = proj3d.shape
    tr = min(512, l)
    assert l % tr == 0
    c128, s128 = _rope_tables(pos, DH_D)
    c64, s64 = _rope_tables(pos, DI_D)
    blk = lambda w, j: pl.BlockSpec((1, tr, w), lambda bi, ti: (bi, ti, j))
    tab = pl.BlockSpec((tr, 128), lambda bi, ti: (ti, 0))
    oblk = lambda w: pl.BlockSpec((1, tr, w), lambda bi, ti: (bi, ti, 0))
    shp = lambda w, dt: jax.ShapeDtypeStruct((b, l, w), dt)
    vb_spec, vb_shape = oblk(KV_W), shp(KV_W, BF16)
    if v_transposed:
        vb_spec = pl.BlockSpec((1, KV_W, tr), lambda bi, ti: (bi, 0, ti))
        vb_shape = jax.ShapeDtypeStruct((b, KV_W, l), BF16)
    return pl.pallas_call(
        functools.partial(_rope_kernel, v_transposed=v_transposed),
        grid=(b, l // tr),
        in_specs=[blk(D_D, OD_Q // D_D), blk(KV_W, OD_K // KV_W), blk(KV_W, OD_V // KV_W), blk(QI_W, OD_QI // QI_W),
                  blk(128, OD_KIWI // 128), tab, tab, tab, tab],
        out_specs=[oblk(D_D), oblk(QI_W), oblk(KV_W), oblk(KV_W), oblk(DI_D), oblk(KV_W), vb_spec, oblk(DI_D)],
        out_shape=[shp(D_D, BF16), shp(QI_W, BF16), shp(KV_W, F32), shp(KV_W, F32), shp(DI_D, F32),
                   shp(KV_W, BF16), vb_shape, shp(DI_D, BF16)],
        compiler_params=pltpu.CompilerParams(dimension_semantics=("parallel", "parallel")),
        name="dsa_rope",
    )(proj3d, proj3d, proj3d, proj3d, proj3d, c128, s128, c64, s64)


DSA_TQ = 128
DSA_TK = 512
NEG_BIG = -1e30
DSA_ONES = 16


def _sortable_key(s):
    bits = pltpu.bitcast(s + 0.0, jnp.int32)
    return bits ^ ((bits >> 31) & 0x7FFFFFFF)


COARSE_BITS = 14


def _coarse_code(c):
    return pltpu.bitcast(lax.shift_left(c + 0x3F80, 16), F32).astype(BF16)


def _dsa_prompt_kernel(q_ref, qi_ref, kiwi_ref, gd_ref, k_ref, vt_ref, ki_ref, o_ref, key_ref, coarse_ref, *, n_sel):
    tq, tk = DSA_TQ, DSA_TK
    t0 = pl.program_id(1) * tq
    n_kt = (t0 + tq + tk - 1) // tk
    qi = qi_ref[0]
    wi_t = kiwi_ref[0].T[DI_D:DI_D + HI_D, :]
    pos_q = t0 + lax.broadcasted_iota(jnp.int32, (1, tq), 1)
    row_k = lax.broadcasted_iota(jnp.int32, (tk, tq), 0)

    def key_slice(kt):
        return pl.ds(pl.multiple_of(kt * tk, tk), tk)

    def fold_rows(x):
        x = x.reshape(tk // 8, 8, tq)
        while x.shape[0] > 1:
            half = x.shape[0] // 2
            x = x[:half] + x[half:]
        return x[0]

    qi_rows = jnp.concatenate([qi[:, h * DI_D:(h + 1) * DI_D] for h in range(HI_D)], axis=0)

    def score_body(kt, carry):
        ks = key_slice(kt)
        d = lax.dot_general(ki_ref[0, ks, :], qi_rows, (((1,), (1,)), ((), ())), preferred_element_type=F32)
        acc = sum(wi_t[h:h + 1, :] * jnp.maximum(d[:, h * tq:(h + 1) * tq], 0.0) for h in range(HI_D))
        key = _sortable_key(acc * IDX_SCALE)
        key = jnp.where(kt * tk + row_k <= pos_q, key, INT_MIN)
        key_ref[ks, :] = key
        coarse_ref[ks, :] = _coarse_code(lax.shift_right_logical(key ^ INT_MIN, 32 - COARSE_BITS))
        return carry

    lax.fori_loop(0, n_kt, score_body, 0)

    def count(pred_fn):
        def body(kt, c):
            return c + fold_rows(jnp.where(pred_fn(key_ref[key_slice(kt), :]), 1.0, 0.0))
        c = lax.fori_loop(0, n_kt, body, jnp.zeros((8, tq), F32))
        return jnp.sum(c, axis=0, keepdims=True)

    def coarse_body(i, ans):
        cand = ans | lax.shift_left(jnp.int32(1), COARSE_BITS - 1 - i)
        code = _coarse_code(cand)
        one, zero = jnp.ones((tk, tq), BF16), jnp.zeros((tk, tq), BF16)

        def body(kt, c):
            hit = jnp.where(coarse_ref[key_slice(kt), :] >= code, one, zero).reshape(tk // 16, 16, tq)
            while hit.shape[0] > 1:
                half = hit.shape[0] // 2
                hit = hit[:half] + hit[half:]
            return c + hit[0]

        c = lax.fori_loop(0, n_kt, body, jnp.zeros((16, tq), BF16))
        cnt = jnp.sum(c.astype(F32), axis=0, keepdims=True)
        return jnp.where(cnt >= n_sel, cand, ans)

    def bit_body(i, ans):
        cand = ans | lax.shift_left(jnp.int32(1), 31 - COARSE_BITS - i)
        scand = cand ^ INT_MIN
        cnt = count(lambda key: key >= scand)
        return jnp.where(cnt >= n_sel, cand, ans)

    top = lax.fori_loop(0, COARSE_BITS, coarse_body, jnp.zeros((1, tq), jnp.int32))
    thr = lax.fori_loop(0, 32 - COARSE_BITS, bit_body, lax.shift_left(top, 32 - COARSE_BITS)) ^ INT_MIN

    def rank_ties():
        need = n_sel - count(lambda key: key > thr)
        tri = (lax.broadcasted_iota(jnp.int32, (tk, tk), 1)
               <= lax.broadcasted_iota(jnp.int32, (tk, tk), 0)).astype(BF16)

        def sel_body(kt, tie_seen):
            ks = key_slice(kt)
            key = key_ref[ks, :]
            tie = (key == thr) & (key != INT_MIN)
            tie_f = jnp.where(tie, 1.0, 0.0)
            rank = tie_seen + jnp.dot(tri, tie_f.astype(BF16), preferred_element_type=F32)
            key_ref[ks, :] = jnp.where((key > thr) | (tie & (rank <= need)), 1, 0)
            return tie_seen + jnp.sum(tie_f, axis=0, keepdims=True)

        lax.fori_loop(0, n_kt, sel_body, jnp.zeros((1, tq), F32))

    def take_all():
        def sel_body(kt, carry):
            ks = key_slice(kt)
            key = key_ref[ks, :]
            key_ref[ks, :] = jnp.where((key >= thr) & (key != INT_MIN), 1, 0)
            return carry

        lax.fori_loop(0, n_kt, sel_body, 0)

    lax.cond(jnp.max(count(lambda key: key >= thr)) > n_sel, rank_ties, take_all)

    c_exp = (DH_D ** -0.5) * math.log2(math.e)
    ones_rows = jnp.ones((DSA_ONES, tk), BF16)
    qs = [q_ref[0, :, h * DH_D:(h + 1) * DH_D] for h in range(H_D)]

    def att_body(kt, carry):
        m, accs = carry
        ks = key_slice(kt)
        sel = key_ref[ks, :] != 0
        hs = range(H_D)
        kts = [k_ref[0, ks, j * DH_D:(j + 1) * DH_D] for j in range(KVH_D)]
        vts = [jnp.concatenate([vt_ref[0, j * DH_D:(j + 1) * DH_D, ks], ones_rows], axis=0) for j in range(KVH_D)]
        logits = [lax.dot_general(kts[h // QPK_D], qs[h], (((1,), (1,)), ((), ())), preferred_element_type=F32)
                  for h in hs]
        logits = [jnp.where(sel, x, NEG_BIG) for x in logits]
        m_rows = [jnp.maximum(m[h:h + 1, :], jnp.max(x, axis=0, keepdims=True)) for h, x in zip(hs, logits)]
        ps = [jnp.exp2((x - mn) * c_exp).astype(BF16) for x, mn in zip(logits, m_rows)]
        scales = [jnp.exp2((m[h:h + 1, :] - mn) * c_exp) for h, mn in zip(hs, m_rows)]
        pv = [jnp.dot(vts[h // QPK_D], p, preferred_element_type=F32) for h, p in zip(hs, ps)]
        new_accs = [accs[h] * sc + x for h, sc, x in zip(hs, scales, pv)]
        return jnp.concatenate(m_rows, axis=0), tuple(new_accs)

    init = (jnp.full((H_D, tq), NEG_BIG, F32), tuple(jnp.zeros((DH_D + DSA_ONES, tq), F32) for _ in range(H_D)))
    _, accs = lax.fori_loop(0, n_kt, att_body, init)
    out = jnp.concatenate([(a[:DH_D] / a[DH_D:DH_D + 1]).T for a in accs], axis=1)
    gd = gd_ref[0]
    o_ref[0] = out * (gd * _sigmoid(gd))


def dsa_prompt_pallas(proj3d, q_b, qi_b, k_b, vt_b, ki_b):
    b, l, _ = proj3d.shape
    n_sel = min(TOPK_MAX, l // 4)
    assert l % DSA_TQ == 0 and l % DSA_TK == 0 and l // 16 <= 256
    qblk = lambda w, j: pl.BlockSpec((1, DSA_TQ, w), lambda bi, ti: (bi, ti, j))
    full = lambda w: pl.BlockSpec((1, l, w), lambda bi, ti: (bi, 0, 0))
    return pl.pallas_call(
        functools.partial(_dsa_prompt_kernel, n_sel=float(n_sel)),
        grid=(b, l // DSA_TQ),
        in_specs=[qblk(D_D, 0), qblk(QI_W, 0), qblk(128, OD_KIWI // 128), qblk(D_D, OD_GD // D_D),
                  full(KV_W), pl.BlockSpec((1, KV_W, l), lambda bi, ti: (bi, 0, 0)), full(DI_D)],
        out_specs=pl.BlockSpec((1, DSA_TQ, D_D), lambda bi, ti: (bi, ti, 0)),
        out_shape=jax.ShapeDtypeStruct((b, l, D_D), F32),
        scratch_shapes=[pltpu.VMEM((l, DSA_TQ), jnp.int32), pltpu.VMEM((l, DSA_TQ), BF16)],
        compiler_params=pltpu.CompilerParams(dimension_semantics=("parallel", "arbitrary")),
        name="dsa_prompt",
    )(q_b, qi_b, proj3d, proj3d, k_b, vt_b, ki_b)


S5_N = G_C * P_C
S5_ROWS = 2 * S5_N // 128
S5_TC = 256


def _s5_kernel(u_ref, g_ref, h0re_ref, h0im_ref, lam_t_ref, lam_r_ref, bd_ref, cd_ref, dsk_ref, gw_ref, gb_ref,
               o_ref, hre_ref, him_ref, scr_ref, h_ref, *, bb, tc, n_valid):
    c = pl.program_id(1)
    rows = bb * tc
    half = S5_ROWS // 2

    @pl.when(c == 0)
    def _():
        for i in range(bb):
            h_ref[i * S5_ROWS:i * S5_ROWS + half] = h0re_ref[i]
            h_ref[i * S5_ROWS + half:(i + 1) * S5_ROWS] = h0im_ref[i]

    def discretize(lam_ref):
        lr, li, dt = lam_ref[0], lam_ref[1], jnp.exp(lam_ref[2])
        mag = jnp.exp(lr * dt)
        br, bi = mag * jnp.cos(li * dt), mag * jnp.sin(li * dt)
        den = lr * lr + li * li
        fr = ((br - 1.0) * lr + bi * li) / den
        fi = (bi * lr - (br - 1.0) * li) / den
        return br, bi, fr, fi

    _, _, fr, fi = discretize(lam_r_ref)
    u = u_ref[...].reshape(rows, D_C)
    hu, hn = D_C // 2, S5_N // 2
    b_re, b_im = (jnp.concatenate([_mm(u[:, i * hu:(i + 1) * hu], bd_ref[i * hu:(i + 1) * hu, o + i * hn:o + (i + 1) * hn])
                                   for i in range(2)], axis=1) for o in (0, S5_N))
    bu = jnp.concatenate([fr * b_re - fi * b_im, fr * b_im + fi * b_re], axis=1)
    for r in range(S5_ROWS):
        scr_ref[pl.ds(r, rows, stride=S5_ROWS), :] = bu[:, r * 128:(r + 1) * 128]

    lbr, lbi, _, _ = discretize(lam_t_ref)
    for i in range(bb):
        def step(t, h):
            hr, hi = h
            idx = pl.multiple_of((i * tc + t) * S5_ROWS, S5_ROWS)
            blk = scr_ref[pl.ds(idx, S5_ROWS), :]
            nr = lbr * hr - lbi * hi + blk[:half]
            ni = lbr * hi + lbi * hr + blk[half:]
            scr_ref[pl.ds(idx, S5_ROWS), :] = jnp.concatenate([nr, ni], axis=0)
            return nr, ni

        h_in = (h_ref[i * S5_ROWS:i * S5_ROWS + half], h_ref[i * S5_ROWS + half:(i + 1) * S5_ROWS])
        hr, hi = lax.fori_loop(0, n_valid, step, h_in, unroll=4)
        h_ref[i * S5_ROWS:i * S5_ROWS + half] = hr
        h_ref[i * S5_ROWS + half:(i + 1) * S5_ROWS] = hi
        hre_ref[i] = hr
        him_ref[i] = hi

    h_tiles = [scr_ref[pl.ds(r, rows, stride=S5_ROWS), :] for r in range(S5_ROWS)]
    n_out = D_C // 128
    per = half // n_out
    y_tiles = []
    for t in range(n_out):
        src = [p * half + t * per + i for p in range(2) for i in range(per)]
        y_tiles.append(_mm(jnp.concatenate([h_tiles[r] for r in src], axis=1),
                           jnp.concatenate([cd_ref[r * 128:(r + 1) * 128, t * 128:(t + 1) * 128] for r in src], axis=0)))
    y = dsk_ref[...] * u + jnp.concatenate(y_tiles, axis=1)
    y = 0.5 * y * (1.0 + jnp.tanh(math.sqrt(2.0 / math.pi) * (y + 0.044715 * (y * y * y))))
    y = y * _sigmoid(_mm(y, gw_ref[...]) + gb_ref[...])
    gate = g_ref[...].reshape(rows, D_C)
    o_ref[...] = (y * (gate * _sigmoid(gate))).reshape(bb, tc, D_C)


def s5_pallas(proj3d, n_valid_tokens, h0_re, h0_im, lam_re, lam_im, log_dt, b_re, b_im, c_re, c_im, d_skip,
              glu_w, glu_b):
    b, l, _ = proj3d.shape
    tc = min(S5_TC, l)
    n_chunks = l // tc
    bb = max(1, min(b, S5_TC // tc))
    assert l % tc == 0 and b % bb == 0 and tc % 8 == 0
    n_valid = n_valid_tokens - (n_chunks - 1) * tc
    assert 0 < n_valid <= tc and (n_valid == tc or n_chunks == 1)
    half = S5_ROWS // 2
    dt_full = jnp.repeat(log_dt, P_C)
    lam_t = jnp.stack([lam_re.reshape(half, 128), lam_im.reshape(half, 128), dt_full.reshape(half, 128)])
    lam_r = jnp.stack([lam_re.reshape(1, S5_N), lam_im.reshape(1, S5_N), dt_full.reshape(1, S5_N)])
    eye = jnp.eye(G_C, dtype=F32)
    bd = jnp.concatenate([jnp.einsum('gpm,gh->gmhp', t, eye).reshape(D_C, S5_N) for t in (b_re, b_im)], axis=1)
    cd = jnp.concatenate([jnp.einsum('gmp,gh->gphm', t, eye).reshape(S5_N, D_C) for t in (c_re, -c_im)], axis=0)
    const = lambda shape: pl.BlockSpec(shape, lambda bi, ci: (0,) * len(shape))
    st_spec = pl.BlockSpec((bb, half, 128), lambda bi, ci: (bi, 0, 0))
    out, h_re, h_im = pl.pallas_call(
        functools.partial(_s5_kernel, bb=bb, tc=tc, n_valid=n_valid),
        grid=(b // bb, n_chunks),
        in_specs=[
            pl.BlockSpec((bb, tc, D_C), lambda bi, ci: (bi, ci, OD_U // D_C)),
            pl.BlockSpec((bb, tc, D_C), lambda bi, ci: (bi, ci, OD_GC // D_C)),
            st_spec, st_spec,
            const((3, half, 128)), const((3, 1, S5_N)), const((D_C, 2 * S5_N)), const((2 * S5_N, D_C)),
            const((1, D_C)), const((D_C, D_C)), const((1, D_C)),
        ],
        out_specs=[pl.BlockSpec((bb, tc, D_C), lambda bi, ci: (bi, ci, 0)), st_spec, st_spec],
        out_shape=[jax.ShapeDtypeStruct((b, l, D_C), F32), jax.ShapeDtypeStruct((b, half, 128), F32),
                   jax.ShapeDtypeStruct((b, half, 128), F32)],
        scratch_shapes=[pltpu.VMEM((bb * tc * S5_ROWS, 128), F32), pltpu.VMEM((bb * S5_ROWS, 128), F32)],
        compiler_params=pltpu.CompilerParams(dimension_semantics=("parallel", "arbitrary")),
        name="s5_scan",
    )(proj3d, proj3d, h0_re.reshape(b, half, 128), h0_im.reshape(b, half, 128), lam_t, lam_r,
      bd.astype(BF16), cd.astype(BF16), d_skip.reshape(1, D_C), glu_w.astype(BF16), glu_b.reshape(1, D_C))
    return out, h_re.reshape(b, G_C, P_C), h_im.reshape(b, G_C, P_C)


def _hi_mid_lo(x):
    hi = x.astype(BF16)
    r1 = x - hi.astype(F32)
    mid = r1.astype(BF16)
    lo = (r1 - mid.astype(F32)).astype(BF16)
    return hi, mid, lo


def _mm3_rhs01(x, b_bf16):
    return sum(jnp.dot(p, b_bf16, preferred_element_type=F32) for p in _hi_mid_lo(x))


def _ssd_kernel(z_ref, xbc_ref, dt_ref, conv0_ref, ssm0_ref, cw_ref, cb_ref, sm_ref, pv_ref, e16_ref, bones_ref,
                o_ref, ssm_ref, convn_ref, buf_ref, st_ref, *, cl, n_valid):
    c = pl.program_id(1)
    keep = CONV_W - 1

    @pl.when(c == 0)
    def _():
        buf_ref[8 - keep:8] = conv0_ref[0]
        st_ref[...] = ssm0_ref[0]

    buf_ref[8:8 + cl] = xbc_ref[0]
    cw = cw_ref[...]
    conv = cb_ref[...] + sum(buf_ref[8 - keep + i:8 - keep + i + cl] * cw[i:i + 1] for i in range(CONV_W))
    convn_ref[0] = buf_ref[8 + n_valid - keep:8 + n_valid]
    buf_ref[8 - keep:8] = buf_ref[8 + cl - keep:8 + cl]
    act = conv * _sigmoid(conv)
    xs, bm, cm = act[:, :D_B], act[:, D_B:D_B + G_B * N_B], act[:, D_B + G_B * N_B:]

    sm = sm_ref[...]
    dt = _softplus(dt_ref[0][:, :H_B] + sm[0:1, :H_B])
    row1 = lax.broadcasted_iota(jnp.int32, (cl, 1), 0)
    if n_valid < cl:
        dt = jnp.where(row1 < n_valid, dt, 0.0)
    a = dt * (-jnp.exp(sm[1:2, :H_B]))
    row = lax.broadcasted_iota(jnp.int32, (cl, cl), 0)
    col = lax.broadcasted_iota(jnp.int32, (cl, cl), 1)
    incl = col <= row
    a_parts = _hi_mid_lo(a)
    a_cum = sum(jnp.dot(incl.astype(BF16), p, preferred_element_type=F32) for p in a_parts)
    a_cum_t = sum(lax.dot_general(p, (row <= col).astype(BF16), (((0,), (0,)), ((), ())),
                                  preferred_element_type=F32) for p in a_parts)
    a_last = a_cum[cl - 1:cl]
    e16 = e16_ref[...]
    xdt = xs * _mm3_rhs01(dt, e16)
    ea_full = jnp.exp(_mm3_rhs01(a_cum, e16))
    xdec = xdt * jnp.exp(_mm3_rhs01(a_last - a_cum, e16))
    chunk_decay = jnp.exp(a_last)

    groups = range(G_B)
    gsl = [slice(g * N_B, (g + 1) * N_B) for g in groups]
    cbs = [_mm_nt(cm[:, s], bm[:, s]) for s in gsl]
    y_off = jnp.concatenate(
        [_mm_nt(cm[:, s], st_ref[g * HPG_B:(g + 1) * HPG_B].reshape(HPG_B * P_B, N_B)) for g, s in zip(groups, gsl)],
        axis=1)
    y_diag = []
    for h in range(H_B):
        hs = slice(h * P_B, (h + 1) * P_B)
        lmat = jnp.exp(jnp.where(incl, a_cum[:, h:h + 1] - a_cum_t[h:h + 1, :], NEG_BIG))
        y_diag.append(_mm(cbs[h // HPG_B] * lmat, xdt[:, hs]))
    for h in range(H_B):
        hs = slice(h * P_B, (h + 1) * P_B)
        st_ref[h] = st_ref[h] * chunk_decay[:, h:h + 1] + _mm_tn(xdec[:, hs], bm[:, gsl[h // HPG_B]])

    pv = pv_ref[...]
    y = jnp.concatenate(y_diag, axis=1) + y_off * ea_full + xs * pv[0:1]
    z = z_ref[0]
    y = y * (z * _sigmoid(z))
    ms = _mm_exact_rhs(y * y, bones_ref[...]) * (1.0 / (D_B // G_B))
    o_ref[0] = y * lax.rsqrt(ms + NORM_EPS) * pv[1:2]

    @pl.when(c == pl.num_programs(1) - 1)
    def _():
        ssm_ref[0] = st_ref[...]


def ssd_pallas(proj3d, n_valid_tokens, conv0, ssm0, conv_w, conv_b, dt_bias, a_log, d_skip, gnorm_g):
    b, l, _ = proj3d.shape
    cl = min(SSD_CHUNK, l)
    n_chunks = l // cl
    assert l % cl == 0 and cl % 8 == 0
    n_valid = n_valid_tokens - (n_chunks - 1) * cl
    assert 0 < n_valid <= cl and (n_valid == cl or n_chunks == 1)
    sm = jnp.zeros((8, 128), F32).at[0, :H_B].set(dt_bias).at[1, :H_B].set(a_log)
    pv = jnp.zeros((8, D_B), F32).at[0].set(jnp.repeat(d_skip, P_B)).at[1].set(gnorm_g)
    e16 = (jnp.arange(D_B)[None, :] // P_B == jnp.arange(H_B)[:, None]).astype(BF16)
    gid = jnp.arange(D_B) // (D_B // G_B)
    bones = (gid[:, None] == gid[None, :]).astype(BF16)
    const = lambda shape: pl.BlockSpec(shape, lambda bi, ci: (0,) * len(shape))
    return pl.pallas_call(
        functools.partial(_ssd_kernel, cl=cl, n_valid=n_valid),
        grid=(b, n_chunks),
        in_specs=[
            pl.BlockSpec((1, cl, D_B), lambda bi, ci: (bi, ci, EV_Z // D_B)),
            pl.BlockSpec((1, cl, CONV_DIM), lambda bi, ci: (bi, ci, EV_XBC // CONV_DIM)),
            pl.BlockSpec((1, cl, 128), lambda bi, ci: (bi, ci, EV_DT // 128)),
            pl.BlockSpec((1, CONV_W - 1, CONV_DIM), lambda bi, ci: (bi, 0, 0)),
            pl.BlockSpec((1, H_B, P_B, N_B), lambda bi, ci: (bi, 0, 0, 0)),
            const((CONV_W, CONV_DIM)), const((1, CONV_DIM)), const((8, 128)), const((8, D_B)),
            const((H_B, D_B)), const((D_B, D_B)),
        ],
        out_specs=[
            pl.BlockSpec((1, cl, D_B), lambda bi, ci: (bi, ci, 0)),
            pl.BlockSpec((1, H_B, P_B, N_B), lambda bi, ci: (bi, 0, 0, 0)),
            pl.BlockSpec((1, CONV_W - 1, CONV_DIM), lambda bi, ci: (bi, 0, 0)),
        ],
        out_shape=[jax.ShapeDtypeStruct((b, l, D_B), F32), jax.ShapeDtypeStruct((b, H_B, P_B, N_B), F32),
                   jax.ShapeDtypeStruct((b, CONV_W - 1, CONV_DIM), F32)],
        scratch_shapes=[pltpu.VMEM((8 + cl, CONV_DIM), F32), pltpu.VMEM((H_B, P_B, N_B), F32)],
        compiler_params=pltpu.CompilerParams(dimension_semantics=("parallel", "arbitrary")),
        name="ssd_chunked",
    )(proj3d, proj3d, proj3d, conv0, ssm0, conv_w, conv_b.reshape(1, CONV_DIM), sm, pv, e16, bones)


RADIX_BITS = 4


def _dsa_sample_kernel(pt_ref, q_ref, qi_ref, kiwi_ref, gd_ref, kn_ref, vn_ref, kin_ref, *rest, n_pages, n_sel, tq):
    kp_refs = rest[:n_pages]
    vp_refs = rest[n_pages:2 * n_pages]
    kip_refs = rest[2 * n_pages:3 * n_pages]
    o_ref = rest[3 * n_pages]
    ps = PAGE_SIZE
    n_tiles = n_pages + 1
    pad_rows = lambda x: jnp.concatenate([x, jnp.zeros((ps - tq, x.shape[1]), x.dtype)], axis=0)

    qi = qi_ref[0]
    qs_i = jnp.concatenate([qi[:, h * DI_D:(h + 1) * DI_D] for h in range(HI_D)], axis=0)
    wi = kiwi_ref[0][:, DI_D:DI_D + HI_D]
    wcol = jnp.concatenate([wi[:, h:h + 1] for h in range(HI_D)], axis=0)
    dots = [_mm(qs_i, r[0]) for r in kip_refs] + [_mm_nt(qs_i, pad_rows(kin_ref[0]))]
    scores = []
    for d in dots:
        d = wcol * jnp.maximum(d, 0.0)
        scores.append(sum(d[h * tq:(h + 1) * tq] for h in range(HI_D)))
    key = _sortable_key(jnp.concatenate(scores, axis=1) * IDX_SCALE)
    n_keys = n_tiles * ps
    qrow = lax.broadcasted_iota(jnp.int32, (tq, n_keys), 0)
    kcol = lax.broadcasted_iota(jnp.int32, (tq, n_keys), 1)
    key = jnp.where(kcol - n_pages * ps <= qrow, key, INT_MIN)

    def count(pred):
        return jnp.sum(jnp.where(pred, 1.0, 0.0), axis=1, keepdims=True)

    ans = jnp.zeros((tq, 1), jnp.int32)
    for shift in range(32 - RADIX_BITS, -1, -RADIX_BITS):
        digit = jnp.zeros((tq, 1), jnp.int32)
        for d in range(1, 2 ** RADIX_BITS):
            cand = ans | (d << shift) if d << shift < 2 ** 31 else ans | (d << shift) - 2 ** 32
            digit = digit + jnp.where(count(key >= (cand ^ INT_MIN)) >= n_sel, 1, 0)
        ans = ans | lax.shift_left(digit, shift)
    thr = ans ^ INT_MIN
    gt = key > thr
    need = n_sel - count(gt)
    tie = (key == thr) & (key != INT_MIN)
    tie_f = jnp.where(tie, 1.0, 0.0)
    tri = (lax.broadcasted_iota(jnp.int32, (ps, ps), 0) <= lax.broadcasted_iota(jnp.int32, (ps, ps), 1)).astype(BF16)
    seen = jnp.zeros((tq, 1), F32)
    ranks = []
    for t in range(n_tiles):
        tf = tie_f[:, t * ps:(t + 1) * ps]
        ranks.append(seen + jnp.dot(tf.astype(BF16), tri, preferred_element_type=F32))
        seen = seen + jnp.sum(tf, axis=1, keepdims=True)
    sel = jnp.where(gt | (tie & (jnp.concatenate(ranks, axis=1) <= need)), 1.0, 0.0).astype(BF16)

    krow = lax.broadcasted_iota(jnp.int32, (ps, KVH_D * ps), 0)
    ccol = lax.broadcasted_iota(jnp.int32, (ps, KVH_D * ps), 1)
    spread = [(ccol == KVH_D * krow + j).astype(BF16) for j in range(KVH_D)]
    masks = []
    for t in range(n_pages):
        st = sel[:, t * ps:(t + 1) * ps]
        per_kv = [jnp.dot(st, spread[j], preferred_element_type=F32) for j in range(KVH_D)]
        masks.append(jnp.concatenate([per_kv[h // QPK_D] for h in range(H_D)], axis=0))
    masks.append(jnp.concatenate([sel[:, n_pages * ps:].astype(F32)] * H_D, axis=0))
    mask = jnp.concatenate(masks, axis=1) > 0.5

    q = q_ref[0]
    qs = jnp.concatenate([q[:, h * DH_D:(h + 1) * DH_D] for h in range(H_D)], axis=0)
    kn, vn = pad_rows(kn_ref[0]), pad_rows(vn_ref[0])
    half = QPK_D * tq
    new_logits = jnp.concatenate([_mm_nt(qs[j * half:(j + 1) * half], kn[:, j * DH_D:(j + 1) * DH_D])
                                  for j in range(KVH_D)], axis=0)
    logits = jnp.concatenate([_mm_nt(qs, r[0].astype(BF16)) for r in kp_refs] + [new_logits], axis=1)
    logits = jnp.where(mask, logits * (DH_D ** -0.5), NEG_BIG)
    m = jnp.max(logits, axis=1, keepdims=True)
    p = jnp.where(mask, jnp.exp(logits - m), 0.0)
    l = jnp.sum(p, axis=1, keepdims=True)
    pb = p.astype(BF16)
    w2 = KVH_D * ps
    acc = sum(jnp.dot(pb[:, t * w2:(t + 1) * w2], r[0].astype(BF16), preferred_element_type=F32)
              for t, r in enumerate(vp_refs))
    p_new = pb[:, n_pages * w2:]
    acc = acc + jnp.concatenate([jnp.dot(p_new[j * half:(j + 1) * half], vn[:, j * DH_D:(j + 1) * DH_D],
                                         preferred_element_type=F32) for j in range(KVH_D)], axis=0)
    o = acc / l
    gd = gd_ref[0]
    o_ref[0] = jnp.concatenate([o[h * tq:(h + 1) * tq] for h in range(H_D)], axis=1) * (gd * _sigmoid(gd))


def dsa_sample_pallas(proj3d, q_b, qi_b, k_b, v_b, ki_b, cache_k, cache_v, cache_ki, page_table, n_valid_tokens):
    b, tq, _ = proj3d.shape
    n_pages = page_table.shape[1]
    n_sel = min(TOPK_MAX, (n_pages * PAGE_SIZE + n_valid_tokens) // 4)
    row = lambda w, j=0: pl.BlockSpec((1, tq, w), lambda bi, pt, j=j: (bi, 0, j))
    kv_page = lambda p: pl.BlockSpec((1, PAGE_SIZE * KVH_D, DH_D), lambda bi, pt, p=p: (pt[bi, p], 0, 0))
    ki_page = lambda p: pl.BlockSpec((1, DI_D, PAGE_SIZE), lambda bi, pt, p=p: (pt[bi, p], 0, 0))
    grid_spec = pltpu.PrefetchScalarGridSpec(
        num_scalar_prefetch=1,
        grid=(b,),
        in_specs=[row(D_D), row(QI_W), row(128, OD_KIWI // 128), row(D_D, OD_GD // D_D), row(KV_W), row(KV_W), row(DI_D)]
        + [kv_page(p) for p in range(n_pages)] + [kv_page(p) for p in range(n_pages)]
        + [ki_page(p) for p in range(n_pages)],
        out_specs=pl.BlockSpec((1, tq, D_D), lambda bi, pt: (bi, 0, 0)),
    )
    return pl.pallas_call(
        functools.partial(_dsa_sample_kernel, n_pages=n_pages, n_sel=float(n_sel), tq=tq),
        grid_spec=grid_spec,
        out_shape=jax.ShapeDtypeStruct((b, tq, D_D), F32),
        compiler_params=pltpu.CompilerParams(dimension_semantics=("parallel",)),
        name="dsa_sample",
    )(page_table, q_b, qi_b, proj3d, proj3d, k_b, v_b, ki_b, *([cache_k] * n_pages), *([cache_v] * n_pages),
      *([cache_ki] * n_pages))


def _split(x, sizes):
    return jnp.split(x, np.cumsum(sizes)[:-1].tolist(), axis=-1)


def even_mixer(x, g, lv, shift0, wkv0, conv0, ssm0, w_in, w_out, mu, w0, w2, a0, a2, k_k, k_a, r_k,
               ln_g, ln_b, conv_w, conv_b, dt_bias, a_log, d_skip, gnorm_g):
    b, l, d = x.shape
    pa_w, ga_w, z_w, xbc_w, dt_w = _split(w_in, [A_SHIFT, D_A, D_B, CONV_DIM, H_B])
    w_ev = jnp.concatenate([ga_w, z_w, xbc_w, pa_w, dt_w, jnp.zeros((d, EV_N - IN_E), w_in.dtype)], axis=1)
    proj = norm_matmul(x.reshape(b * l, d), g, w_ev.astype(BF16)).reshape(b, l, EV_N)
    out_a, wkv_new, shift_new = rwkv7_pallas(proj, shift0, wkv0, mu, w0, w2, a0, a2, k_k, k_a, r_k, ln_g, ln_b, lv)
    out_b, ssm_new, conv_new = ssd_pallas(proj, lv, conv0, ssm0, conv_w, conv_b, dt_bias, a_log, d_skip, gnorm_g)
    x_new = matmul_residual(out_a.reshape(b * l, D_A), out_b.reshape(b * l, D_B), w_out.astype(BF16),
                            x.reshape(b * l, d)).reshape(b, l, d)
    return x_new, (wkv_new, shift_new, ssm_new, conv_new)


def odd_mixer(x, g, lv, pos, c_re0, c_im0, attend, w_in, w_out, lam_re, lam_im, log_dt, b_re, b_im,
              c_re, c_im, d_skip, glu_w, glu_b):
    b, l, d = x.shape
    u_w, gc_w, q_w, k_w, v_w, qi_w, ki_w, wi_w, gd_w = _split(
        w_in, [D_C, D_C, D_D, KV_W, KV_W, QI_W, DI_D, HI_D, D_D])
    w_od = jnp.concatenate([u_w, gc_w, q_w, gd_w, k_w, v_w, qi_w, ki_w, wi_w,
                            jnp.zeros((d, OD_N - IN_O), w_in.dtype)], axis=1)
    proj = norm_matmul(x.reshape(b * l, d), g, w_od.astype(BF16)).reshape(b, l, OD_N)
    out_c, re_last, im_last = s5_pallas(proj, lv, c_re0, c_im0, lam_re, lam_im, log_dt, b_re, b_im,
                                        c_re, c_im, d_skip, glu_w, glu_b)
    q_b, qi_b, k_f, v_f, ki, k_b, v_b, ki_b = rope_pallas(proj, pos, v_transposed=attend is None)
    if attend is None:
        out_d = dsa_prompt_pallas(proj, q_b, qi_b, k_b, v_b, ki_b)
    else:
        out_d = dsa_sample_pallas(proj, q_b, qi_b, k_b, v_b, ki_b, *attend, lv)
    k = k_f[:, :lv].reshape(b, lv, KVH_D, DH_D)
    v = v_f[:, :lv].reshape(b, lv, KVH_D, DH_D)
    ki = ki[:, :lv]
    x_new = matmul_residual(out_c.reshape(b * l, D_C), out_d.reshape(b * l, D_D), w_out.astype(BF16),
                            x.reshape(b * l, d)).reshape(b, l, d)
    return x_new, (re_last, im_last, k, v, ki)


def kernel(x_prompt, x_sample, state_a_wkv, state_a_shift, state_b_ssm, state_b_conv, state_c_re, state_c_im, cache_d_k, cache_d_v, cache_d_kidx, page_table, norm_g, final_norm_g, w_in_e, w_out_e, rwkv_mu, rwkv_w0, rwkv_w2, rwkv_a0, rwkv_a2, rwkv_kk, rwkv_ka, rwkv_rk, rwkv_ln_g, rwkv_ln_b, ssd_conv_w, ssd_conv_b, ssd_dt_bias, ssd_a_log, ssd_d, ssd_norm_g, w_in_o, w_out_o, s5_lam_re, s5_lam_im, s5_log_dt, s5_b_re, s5_b_im, s5_c_re, s5_c_im, s5_d, s5_glu_w, s5_glu_b):
    f32 = jnp.float32
    bp, lp = x_prompt.shape[:2]
    bs, ls = x_sample.shape[:2]
    pos_p = jnp.arange(lp)
    ls_pad = _round_up(ls, 8)
    pos_s = PAST_LEN + jnp.arange(ls_pad)
    xp, xs = x_prompt, jnp.pad(x_sample, ((0, 0), (0, ls_pad - ls), (0, 0)))
    even_p, even_s, odd_p, odd_s = [], [], [], []
    for i in range(DEPTH):
        j = i // 2
        if i % 2 == 0:
            pe = (w_in_e[j], w_out_e[j], rwkv_mu[j], rwkv_w0[j], rwkv_w2[j], rwkv_a0[j], rwkv_a2[j],
                  rwkv_kk[j], rwkv_ka[j], rwkv_rk[j], rwkv_ln_g[j], rwkv_ln_b[j], ssd_conv_w[j],
                  ssd_conv_b[j], ssd_dt_bias[j], ssd_a_log[j], ssd_d[j], ssd_norm_g[j])
            xp, st_p = even_mixer(xp, norm_g[i], lp, jnp.zeros((bp, A_SHIFT), f32), jnp.zeros((bp, H_A, HD_A, HD_A), f32),
                                  jnp.zeros((bp, CONV_W - 1, CONV_DIM), f32),
                                  jnp.zeros((bp, H_B, P_B, N_B), f32), *pe)
            xs, st_s = even_mixer(xs, norm_g[i], ls, state_a_shift[j], state_a_wkv[j], state_b_conv[j], state_b_ssm[j], *pe)
            even_p.append(st_p)
            even_s.append(st_s)
        else:
            po = (w_in_o[j], w_out_o[j], s5_lam_re[j], s5_lam_im[j], s5_log_dt[j], s5_b_re[j], s5_b_im[j],
                  s5_c_re[j], s5_c_im[j], s5_d[j], s5_glu_w[j], s5_glu_b[j])
            zc = jnp.zeros((bp, G_C, P_C), f32)
            xp, st_p = odd_mixer(xp, norm_g[i], lp, pos_p, zc, zc, None, *po)
            n_pool = cache_d_k.shape[1]
            attend_s = (cache_d_k.reshape(-1, PAGE_SIZE * KVH_D, DH_D), cache_d_v.reshape(-1, PAGE_SIZE * KVH_D, DH_D),
                        jnp.swapaxes(cache_d_kidx, 2, 3).reshape(-1, DI_D, PAGE_SIZE), page_table + j * n_pool)
            xs, st_s = odd_mixer(xs, norm_g[i], ls, pos_s, state_c_re[j], state_c_im[j], attend_s, *po)
            odd_p.append(st_p)
            odd_s.append(st_s)
    y_prompt = rmsnorm_rows(xp.reshape(bp * lp, D_MODEL), final_norm_g).reshape(bp, lp, D_MODEL)
    y_sample = rmsnorm_rows(xs.reshape(bs * ls_pad, D_MODEL), final_norm_g).reshape(bs, ls_pad, D_MODEL)[:, :ls]
    new_a_wkv_p, new_a_shift_p, new_b_ssm_p, new_b_conv_p = [jnp.stack(t) for t in zip(*even_p)]
    new_a_wkv_s, new_a_shift_s, new_b_ssm_s, new_b_conv_s = [jnp.stack(t) for t in zip(*even_s)]
    new_c_re_p, new_c_im_p, new_d_k_p, new_d_v_p, new_d_kidx_p = [jnp.stack(t) for t in zip(*odd_p)]
    new_c_re_s, new_c_im_s, new_d_k_s, new_d_v_s, new_d_kidx_s = [jnp.stack(t) for t in zip(*odd_s)]
    return (y_prompt, y_sample,
            new_a_wkv_p, new_a_shift_p, new_b_ssm_p, new_b_conv_p,
            new_c_re_p, new_c_im_p, new_d_k_p, new_d_v_p, new_d_kidx_p,
            new_a_wkv_s, new_a_shift_s, new_b_ssm_s, new_b_conv_s,
            new_c_re_s, new_c_im_s, new_d_k_s, new_d_v_s, new_d_kidx_s)
```

```python
import functools
import math

import jax
import jax.numpy as jnp
import numpy as np
from jax import lax
from jax.experimental import pallas as pl
from jax.experimental.pallas import tpu as pltpu

D_MODEL = 1024
DEPTH = 4
PAST_LEN = 2048
PAGE_SIZE = 128
NORM_EPS = 1e-6

D_A = D_MODEL
HD_A = 64
H_A = D_A // HD_A
W_LORA = 64
A_LORA = 64
A_SHIFT = 3 * D_A + W_LORA + A_LORA
RWKV_LN_EPS = 64e-5
D_B = D_MODEL
P_B = 64
H_B = D_B // P_B
N_B = 128
G_B = 4
HPG_B = H_B // G_B
CONV_W = 4
CONV_DIM = D_B + 2 * G_B * N_B
SSD_CHUNK = 128
D_C = D_MODEL // 2
CH_C = 16
G_C = D_C // CH_C
P_C = 64
H_D = 8
DH_D = 128
KVH_D = 2
QPK_D = H_D // KVH_D
D_D = H_D * DH_D
HI_D = 8
DI_D = 64
IDX_SCALE = (DI_D ** -0.5) * (HI_D ** -0.5)
TOPK_MAX = 256
QBLK = 128
ROPE_THETA = 500000.0
ROPE_FRAC = 4

IN_E = A_SHIFT + D_A + D_B + CONV_DIM + H_B
OUT_E = D_A + D_B
IN_O = 2 * D_C + D_D + 2 * KVH_D * DH_D + HI_D * DI_D + DI_D + HI_D + D_D
OUT_O = D_C + D_D

F32 = jnp.float32
BF16 = jnp.bfloat16

TILE_M = 512
TILE_N = 512
PROJ_TILE_M = 1024


def _round_up(n, m):
    return (n + m - 1) // m * m


def _norm_matmul_kernel(x_ref, g_ref, w_ref, o_ref, h_ref):
    @pl.when(pl.program_id(1) == 0)
    def _():
        x = x_ref[...]
        ms = jnp.mean(x * x, axis=-1, keepdims=True)
        h_ref[...] = (x * lax.rsqrt(ms + NORM_EPS) * g_ref[...]).astype(BF16)

    o_ref[...] = jnp.dot(h_ref[...], w_ref[...], preferred_element_type=F32)


def norm_matmul(x2d, g, w_bf16, tn=TILE_N):
    m, d = x2d.shape
    n = w_bf16.shape[1]
    tm = min(PROJ_TILE_M, m)
    assert m % tm == 0 and n % tn == 0
    return pl.pallas_call(
        _norm_matmul_kernel,
        grid=(m // tm, n // tn),
        in_specs=[
            pl.BlockSpec((tm, d), lambda i, j: (i, 0)),
            pl.BlockSpec((1, d), lambda i, j: (0, 0)),
            pl.BlockSpec((d, tn), lambda i, j: (0, j)),
        ],
        out_specs=pl.BlockSpec((tm, tn), lambda i, j: (i, j)),
        out_shape=jax.ShapeDtypeStruct((m, n), F32),
        scratch_shapes=[pltpu.VMEM((tm, d), BF16)],
        compiler_params=pltpu.CompilerParams(dimension_semantics=("parallel", "arbitrary")),
        name="norm_matmul",
    )(x2d, g.reshape(1, d), w_bf16)


def _matmul_res_kernel(a1_ref, a2_ref, w1_ref, w2_ref, r_ref, o_ref):
    o_ref[...] = (r_ref[...] + jnp.dot(a1_ref[...].astype(BF16), w1_ref[...], preferred_element_type=F32)
                  + jnp.dot(a2_ref[...].astype(BF16), w2_ref[...], preferred_element_type=F32))


def matmul_residual(a1, a2, w_bf16, res2d):
    m, k1 = a1.shape
    k2 = a2.shape[1]
    n = w_bf16.shape[1]
    tm = min(TILE_M, m)
    assert m % tm == 0 and w_bf16.shape[0] == k1 + k2
    return pl.pallas_call(
        _matmul_res_kernel,
        grid=(m // tm,),
        in_specs=[
            pl.BlockSpec((tm, k1), lambda i: (i, 0)),
            pl.BlockSpec((tm, k2), lambda i: (i, 0)),
            pl.BlockSpec((k1, n), lambda i: (0, 0)),
            pl.BlockSpec((k2, n), lambda i: (0, 0)),
            pl.BlockSpec((tm, n), lambda i: (i, 0)),
        ],
        out_specs=pl.BlockSpec((tm, n), lambda i: (i, 0)),
        out_shape=jax.ShapeDtypeStruct((m, n), F32),
        compiler_params=pltpu.CompilerParams(dimension_semantics=("parallel",)),
        name="matmul_residual",
    )(a1, a2, w_bf16[:k1], w_bf16[k1:], res2d)


def _rmsnorm_kernel(x_ref, g_ref, o_ref):
    x = x_ref[...]
    ms = jnp.mean(x * x, axis=-1, keepdims=True)
    o_ref[...] = x * lax.rsqrt(ms + NORM_EPS) * g_ref[...]


def rmsnorm_rows(x2d, g):
    m, d = x2d.shape
    tm = min(TILE_M, m)
    return pl.pallas_call(
        _rmsnorm_kernel,
        grid=(m // tm,),
        in_specs=[pl.BlockSpec((tm, d), lambda i: (i, 0)), pl.BlockSpec((1, d), lambda i: (0, 0))],
        out_specs=pl.BlockSpec((tm, d), lambda i: (i, 0)),
        out_shape=jax.ShapeDtypeStruct((m, d), F32),
        compiler_params=pltpu.CompilerParams(dimension_semantics=("parallel",)),
        name="final_rmsnorm",
    )(x2d, g.reshape(1, d))


def _mm(a, b):
    return jnp.dot(a.astype(BF16), b.astype(BF16), preferred_element_type=F32)


def _mm_nt(a, b):
    return lax.dot_general(a.astype(BF16), b.astype(BF16), (((1,), (1,)), ((), ())), preferred_element_type=F32)


def _mm_tn(a, b):
    return lax.dot_general(a.astype(BF16), b.astype(BF16), (((0,), (0,)), ((), ())), preferred_element_type=F32)


def _hi_lo(x):
    hi = x.astype(BF16)
    lo = (x - hi.astype(F32)).astype(BF16)
    return hi, lo


def _mm_exact_lhs(a_bf16, x):
    hi, lo = _hi_lo(x)
    return (jnp.dot(a_bf16, hi, preferred_element_type=F32) + jnp.dot(a_bf16, lo, preferred_element_type=F32))


def _mm_exact_rhs(x, b_bf16):
    hi, lo = _hi_lo(x)
    return (jnp.dot(hi, b_bf16, preferred_element_type=F32) + jnp.dot(lo, b_bf16, preferred_element_type=F32))


def _softplus(x):
    return jnp.maximum(x, 0.0) + jnp.log1p(jnp.exp(-jnp.abs(x)))


def _sigmoid(x):
    return 1.0 / (1.0 + jnp.exp(-x))


def _unit_lower_inverses(a_list, row, col, n):
    eye = jnp.where(row == col, 1.0, 0.0).astype(F32)
    ts = [eye for _ in a_list]
    m = 1
    while m < n:
        in_pair = (row // (2 * m)) == (col // (2 * m))
        lvl = in_pair & ((row % (2 * m)) >= m) & ((col % (2 * m)) < m)
        ls = [jnp.where(lvl, a, 0.0) for a in a_list]
        if m == 1:
            ts = [t - l for t, l in zip(ts, ls)]
        else:
            tl = [_mm(t, l) for t, l in zip(ts, ls)]
            ts = [t - _mm(x, t) for t, x in zip(ts, tl)]
        m *= 2
    return ts


EV_GA, EV_Z, EV_XBC, EV_R, EV_K, EV_V, EV_WA, EV_DT = 0, 1024, 2048, 4096, 5120, 6144, 7168, 7296
EV_N = 7680
RWKV_CHUNK = 64


RWKV_GROUP = 4
RWKV_BATCH = 8


def _block_rows(x, n, mask):
    return jnp.where(mask, jnp.concatenate([x] * n, axis=0), 0.0)


def _wkv_groups(alpha, beta, kappa, rho, v, kappa_e, beta_e, wc, st_ref, C, bb):
    gh, hd = RWKV_GROUP, HD_A
    gw, tw = gh * hd, gh * C
    iota = lambda shape, d: lax.broadcasted_iota(jnp.int32, shape, d)
    row, cin = iota((C, tw), 0), iota((C, tw), 1) % C
    strict, incl = cin < row, cin <= row
    m_tt = iota((tw, tw), 0) // C == iota((tw, tw), 1) // C
    m_tk = iota((tw, gw), 0) // C == iota((tw, gw), 1) // hd
    m_kk = iota((gw, gw), 0) // hd == iota((gw, gw), 1) // hd
    probs = [(b, slice(b * C, (b + 1) * C), slice(g * gw, (g + 1) * gw)) for b in range(bb) for g in range(H_A // gh)]
    s0s = [st_ref[b, :, sl] for b, _, sl in probs]
    bd_s = [_block_rows(s0, gh, m_kk) for s0 in s0s]
    bd_v = [_block_rows(v[rs, sl], gh, m_tk) for _, rs, sl in probs]
    grams = [_mm_nt(jnp.concatenate([alpha[rs, sl], rho[rs, sl]], axis=0),
                    jnp.concatenate([_block_rows(beta[rs, sl], gh, m_tk), _block_rows(kappa[rs, sl], gh, m_tk)],
                                    axis=0)) for _, rs, sl in probs]
    a_bs = [jnp.where(strict, g[:C, :tw], 0.0) for g in grams]
    ts = [jnp.where(cin == row, 1.0, 0.0).astype(F32) for _ in probs]
    m = 1
    while m < C:
        lvl = ((row // (2 * m)) == (cin // (2 * m))) & ((row % (2 * m)) >= m) & ((cin % (2 * m)) < m)
        ls = [jnp.where(lvl, a, 0.0) for a in a_bs]
        if m == 1:
            ts = [t - l for t, l in zip(ts, ls)]
        else:
            tl = [_mm(t, _block_rows(l, gh, m_tt)) for t, l in zip(ts, ls)]
            ts = [t - _mm(x, _block_rows(t, gh, m_tt)) for t, x in zip(ts, tl)]
        m *= 2
    rhss = [_mm_nt(alpha[rs, sl], bs) + _mm(jnp.where(strict, g[:C, tw:], 0.0), bv)
            for (_, rs, sl), bs, bv, g in zip(probs, bd_s, bd_v, grams)]
    us = [_mm(t, _block_rows(x, gh, m_tk)) for t, x in zip(ts, rhss)]
    outs = [_mm_nt(rho[rs, sl], bs) + _mm(jnp.where(incl, g[C:, tw:], 0.0), bv)
            - _mm(jnp.where(incl, g[C:, :tw], 0.0), _block_rows(u, gh, m_tk))
            for (_, rs, sl), bs, bv, g, u in zip(probs, bd_s, bd_v, grams, us)]
    for (b, rs, sl), s0, u in zip(probs, s0s, us):
        x = jnp.concatenate([v[rs, sl], -u], axis=0)
        y = jnp.concatenate([kappa_e[rs, sl], beta_e[rs, sl]], axis=0)
        cross = jnp.where(m_kk, _mm_tn(x, y), 0.0)
        st_ref[b, :, sl] = s0 * wc[rs, sl][:1] + sum(cross[h * hd:(h + 1) * hd] for h in range(gh))
    n_g = H_A // gh
    return jnp.concatenate([jnp.concatenate(outs[b * n_g:(b + 1) * n_g], axis=1) for b in range(bb)], axis=0)


def _wkv_heads(alpha, beta, kappa, rho, v, kappa_e, beta_e, wc, st_ref, C, bb):
    row = lax.broadcasted_iota(jnp.int32, (C, C), 0)
    col = lax.broadcasted_iota(jnp.int32, (C, C), 1)
    incl, strict = col <= row, col < row
    probs = [(b, slice(b * C, (b + 1) * C), slice(h * HD_A, (h + 1) * HD_A)) for b in range(bb) for h in range(H_A)]
    s0s = [st_ref[b, :, sl] for b, _, sl in probs]
    bks = [jnp.concatenate([beta[rs, sl], kappa[rs, sl]], axis=0) for _, rs, sl in probs]
    g_as = [_mm_nt(alpha[rs, sl], bk) for (_, rs, sl), bk in zip(probs, bks)]
    g_rs = [_mm_nt(rho[rs, sl], bk) for (_, rs, sl), bk in zip(probs, bks)]
    t_invs = _unit_lower_inverses([jnp.where(strict, g[:, :C], 0.0) for g in g_as], row, col, C)
    rhss = [_mm_nt(alpha[rs, sl], s0) + _mm(jnp.where(strict, g[:, C:], 0.0), v[rs, sl])
            for (_, rs, sl), s0, g in zip(probs, s0s, g_as)]
    us = [_mm(t, x) for t, x in zip(t_invs, rhss)]
    outs = [_mm_nt(rho[rs, sl], s0) + _mm(jnp.where(incl, g[:, C:], 0.0), v[rs, sl])
            - _mm(jnp.where(incl, g[:, :C], 0.0), u) for (_, rs, sl), s0, g, u in zip(probs, s0s, g_rs, us)]
    for (b, rs, sl), s0, u in zip(probs, s0s, us):
        x = jnp.concatenate([v[rs, sl], -u], axis=0)
        y = jnp.concatenate([kappa_e[rs, sl], beta_e[rs, sl]], axis=0)
        st_ref[b, :, sl] = s0 * wc[rs, sl][:1] + _mm_tn(x, y)
    return jnp.concatenate([jnp.concatenate(outs[b * H_A:(b + 1) * H_A], axis=1) for b in range(bb)], axis=0)


def _rwkv_kernel(r_ref, k_ref, v_ref, wa_ref, g_ref, sh0_ref, s0_ref, mu_ref, pv_ref, w2_ref, a2_ref, bones_ref,
                 o_ref, snew_ref, shnew_ref, st_ref, carry_ref, *, chunk, n_valid, bb):
    C = chunk
    R = bb * C
    c = pl.program_id(1)

    @pl.when(c == 0)
    def _():
        carry_ref[...] = sh0_ref[...]
        for b in range(bb):
            for h in range(H_A):
                st_ref[b, :, h * HD_A:(h + 1) * HD_A] = s0_ref[b, h]

    mu = mu_ref[...]
    pv = pv_ref[...]
    w0, a0, k_k, k_a, r_k, ln_g, ln_b = (pv[i:i + 1] for i in range(7))
    row8 = lax.broadcasted_iota(jnp.int32, (8, 1), 0)
    t_in_chunk = lax.broadcasted_iota(jnp.int32, (R, 1), 0) % C

    def tok_shift(x, lo, hi):
        rolled = pltpu.roll(x, 1, 0)
        pieces = []
        for b in range(bb):
            pieces.append(jnp.where(row8 == 0, carry_ref[b, :, lo:hi], rolled[b * C:b * C + 8]))
            if C > 8:
                pieces.append(rolled[b * C + 8:(b + 1) * C])
        return x + (jnp.concatenate(pieces, axis=0) - x) * mu[:, lo:hi]

    pr, pk, pvv, pwa = (ref[...].reshape(R, ref.shape[-1]) for ref in (r_ref, k_ref, v_ref, wa_ref))
    r = tok_shift(pr, 0, D_A)
    k = tok_shift(pk, D_A, 2 * D_A)
    v = tok_shift(pvv, 2 * D_A, 3 * D_A)
    wa = tok_shift(pwa, 3 * D_A, A_SHIFT)
    for b in range(bb):
        last = b * C + n_valid - 1
        new_carry = jnp.concatenate([x[last:last + 1] for x in (pr, pk, pvv, pwa)], axis=1)
        carry_ref[b] = new_carry
        shnew_ref[b] = new_carry

    wl, al = wa[:, :W_LORA], wa[:, W_LORA:]
    w_log = -_softplus(-(w0 + _mm(jnp.tanh(wl), w2_ref[...]))) - 0.5
    logw = -jnp.exp(w_log)
    a = _sigmoid(a0 + _mm(al, a2_ref[...]))
    bones = bones_ref[...]
    kk = k * k_k
    kk = kk / jnp.maximum(jnp.sqrt(_mm_exact_rhs(kk * kk, bones)), 1e-12)
    kp = k * (1.0 + (a - 1.0) * k_a)
    bonus = _mm_exact_rhs(r * kp * r_k, bones) * v
    if n_valid < C:
        ok = t_in_chunk < n_valid
        logw = jnp.where(ok, logw, 0.0)
        kk = jnp.where(ok, kk, 0.0)
        kp = jnp.where(ok, kp, 0.0)

    row = lax.broadcasted_iota(jnp.int32, (R, R), 0)
    col = lax.broadcasted_iota(jnp.int32, (R, R), 1)
    same = (row // C) == (col // C)
    cum = _mm_exact_lhs((same & (col <= row)).astype(BF16), logw)
    tot = _mm_exact_lhs(same.astype(BF16), logw)
    eneg = jnp.exp(-cum)
    alpha = kk * jnp.exp(cum - logw)
    ka = kk * a
    beta = ka * eneg
    kappa = kp * eneg
    rho = r * jnp.exp(cum)
    dec_end = jnp.exp(tot - cum)
    kappa_e = kp * dec_end
    beta_e = ka * dec_end
    wc = jnp.exp(tot)

    solve = _wkv_groups if RWKV_GROUP * C % 128 == 0 else _wkv_heads
    out = solve(alpha, beta, kappa, rho, v, kappa_e, beta_e, wc, st_ref, C, bb)

    mean = _mm_exact_rhs(out, bones) * (1.0 / HD_A)
    d = out - mean
    var = _mm_exact_rhs(d * d, bones) * (1.0 / HD_A)
    y = d * lax.rsqrt(var + RWKV_LN_EPS) * ln_g + ln_b
    gate = g_ref[...].reshape(R, D_A)
    o_ref[...] = ((y + bonus) * (gate * _sigmoid(gate))).reshape(bb, C, D_A)
    for b in range(bb):
        for h in range(H_A):
            snew_ref[b, h] = st_ref[b, :, h * HD_A:(h + 1) * HD_A]


def rwkv7_pallas(proj3d, shift0, wkv0, mu, w0, w2, a0, a2, k_k, k_a, r_k, ln_g, ln_b, n_valid_tokens):
    b, l, _ = proj3d.shape
    chunk = min(RWKV_CHUNK, l)
    n_chunks = l // chunk
    bb = math.gcd(b, RWKV_BATCH)
    assert l % chunk == 0 and chunk % 8 == 0
    n_valid = n_valid_tokens - (n_chunks - 1) * chunk
    assert 0 < n_valid <= chunk and (n_valid == chunk or n_chunks == 1)
    pvec = jnp.stack([w0, a0, k_k, k_a, r_k.reshape(D_A), ln_g, ln_b, jnp.zeros_like(w0)])
    hid = jnp.arange(D_A) // HD_A
    bones = (hid[:, None] == hid[None, :]).astype(BF16)
    blk = lambda w, j: pl.BlockSpec((bb, chunk, w), lambda bi, ci: (bi, ci, j))
    const = lambda shape: pl.BlockSpec(shape, lambda bi, ci: (0,) * len(shape))
    out, wkv_new, shift_new = pl.pallas_call(
        functools.partial(_rwkv_kernel, chunk=chunk, n_valid=n_valid, bb=bb),
        grid=(b // bb, n_chunks),
        in_specs=[
            blk(D_A, EV_R // D_A), blk(D_A, EV_K // D_A), blk(D_A, EV_V // D_A), blk(128, EV_WA // 128),
            blk(D_A, EV_GA // D_A),
            pl.BlockSpec((bb, 1, A_SHIFT), lambda bi, ci: (bi, 0, 0)),
            pl.BlockSpec((bb, H_A, HD_A, HD_A), lambda bi, ci: (bi, 0, 0, 0)),
            const((1, A_SHIFT)), const((8, D_A)), const((W_LORA, D_A)), const((A_LORA, D_A)), const((D_A, D_A)),
        ],
        out_specs=[
            pl.BlockSpec((bb, chunk, D_A), lambda bi, ci: (bi, ci, 0)),
            pl.BlockSpec((bb, H_A, HD_A, HD_A), lambda bi, ci: (bi, 0, 0, 0)),
            pl.BlockSpec((bb, 1, A_SHIFT), lambda bi, ci: (bi, 0, 0)),
        ],
        out_shape=[
            jax.ShapeDtypeStruct((b, l, D_A), F32),
            jax.ShapeDtypeStruct((b, H_A, HD_A, HD_A), F32),
            jax.ShapeDtypeStruct((b, 1, A_SHIFT), F32),
        ],
        scratch_shapes=[pltpu.VMEM((bb, HD_A, D_A), F32), pltpu.VMEM((bb, 1, A_SHIFT), F32)],
        compiler_params=pltpu.CompilerParams(dimension_semantics=("parallel", "arbitrary")),
        name="rwkv7_chunked",
    )(proj3d, proj3d, proj3d, proj3d, proj3d, shift0.reshape(b, 1, A_SHIFT), wkv0,
      mu.reshape(1, A_SHIFT), pvec, w2.astype(BF16), a2.astype(BF16), bones)
    return out, wkv_new, shift_new.reshape(b, A_SHIFT)


OD_U, OD_GC, OD_Q, OD_GD, OD_K, OD_V, OD_QI, OD_KIWI = 0, 512, 1024, 2048, 3072, 3328, 3584, 4096
OD_N = 4608
KV_W = KVH_D * DH_D
QI_W = HI_D * DI_D
INT_MIN = -2 ** 31


def _rope_tables(pos, head_dim):
    rd = head_dim // ROPE_FRAC
    half = rd // 2
    inv_freq = ROPE_THETA ** (-jnp.arange(half, dtype=F32) / half)
    ang = pos.astype(F32)[:, None] * inv_freq[None, :]
    cos, sin = jnp.cos(ang), jnp.sin(ang)
    t = pos.shape[0]
    ones = jnp.ones((t, head_dim - rd), F32)
    cos_h = jnp.concatenate([cos, cos, ones], axis=1)
    sin_h = jnp.concatenate([-sin, sin, 0.0 * ones], axis=1)
    reps = 128 // head_dim
    return jnp.tile(cos_h, (1, reps)), jnp.tile(sin_h, (1, reps))


def _rotate(x, cos_t, sin_t, head_dim):
    w = x.shape[1]
    half = head_dim // ROPE_FRAC // 2
    lane = lax.broadcasted_iota(jnp.int32, x.shape, 1) % head_dim
    if w >= 128:
        cos_f = jnp.tile(cos_t, (1, w // 128))
        sin_f = jnp.tile(sin_t, (1, w // 128))
        partner = jnp.where(lane < half, pltpu.roll(x, w - half, 1), pltpu.roll(x, half, 1))
    else:
        cos_f, sin_f = cos_t[:, :w], sin_t[:, :w]
        partner = jnp.where(lane < half, jnp.concatenate([x[:, half:], x[:, :half]], axis=1),
                            jnp.concatenate([x[:, w - half:], x[:, :w - half]], axis=1))
    return x * cos_f + partner * sin_f


def _rope_kernel(q_ref, k_ref, v_ref, qi_ref, kiwi_ref, c128_ref, s128_ref, c64_ref, s64_ref,
                 qo_ref, qio_ref, ko_ref, vo_ref, kio_ref, kb_ref, vb_ref, kib_ref, *, v_transposed):
    c128, s128, c64, s64 = c128_ref[...], s128_ref[...], c64_ref[...], s64_ref[...]
    qo_ref[0] = _rotate(q_ref[0], c128, s128, DH_D).astype(BF16)
    qio_ref[0] = _rotate(qi_ref[0], c64, s64, DI_D).astype(BF16)
    k_rot = _rotate(k_ref[0], c128, s128, DH_D)
    ko_ref[0] = k_rot
    kb_ref[0] = k_rot.astype(BF16)
    v = v_ref[0]
    vo_ref[0] = v
    vb_ref[0] = (v.T if v_transposed else v).astype(BF16)
    ki_rot = _rotate(kiwi_ref[0][:, :DI_D], c64, s64, DI_D)
    kio_ref[0] = ki_rot
    kib_ref[0] = ki_rot.astype(BF16)


def rope_pallas(proj3d, pos, v_transposed=False):
    b, l, _ = proj3d.shape
    tr = min(512, l)
    assert l % tr == 0
    c128, s128 = _rope_tables(pos, DH_D)
    c64, s64 = _rope_tables(pos, DI_D)
    blk = lambda w, j: pl.BlockSpec((1, tr, w), lambda bi, ti: (bi, ti, j))
    tab = pl.BlockSpec((tr, 128), lambda bi, ti: (ti, 0))
    oblk = lambda w: pl.BlockSpec((1, tr, w), lambda bi, ti: (bi, ti, 0))
    shp = lambda w, dt: jax.ShapeDtypeStruct((b, l, w), dt)
    vb_spec, vb_shape = oblk(KV_W), shp(KV_W, BF16)
    if v_transposed:
        vb_spec = pl.BlockSpec((1, KV_W, tr), lambda bi, ti: (bi, 0, ti))
        vb_shape = jax.ShapeDtypeStruct((b, KV_W, l), BF16)
    return pl.pallas_call(
        functools.partial(_rope_kernel, v_transposed=v_transposed),
        grid=(b, l // tr),
        in_specs=[blk(D_D, OD_Q // D_D), blk(KV_W, OD_K // KV_W), blk(KV_W, OD_V // KV_W), blk(QI_W, OD_QI // QI_W),
                  blk(128, OD_KIWI // 128), tab, tab, tab, tab],
        out_specs=[oblk(D_D), oblk(QI_W), oblk(KV_W), oblk(KV_W), oblk(DI_D), oblk(KV_W), vb_spec, oblk(DI_D)],
        out_shape=[shp(D_D, BF16), shp(QI_W, BF16), shp(KV_W, F32), shp(KV_W, F32), shp(DI_D, F32),
                   shp(KV_W, BF16), vb_shape, shp(DI_D, BF16)],
        compiler_params=pltpu.CompilerParams(dimension_semantics=("parallel", "parallel")),
        name="dsa_rope",
    )(proj3d, proj3d, proj3d, proj3d, proj3d, c128, s128, c64, s64)


DSA_TQ = 128
DSA_TK = 512
NEG_BIG = -1e30
DSA_ONES = 16


def _sortable_key(s):
    bits = pltpu.bitcast(s + 0.0, jnp.int32)
    return bits ^ ((bits >> 31) & 0x7FFFFFFF)


COARSE_BITS = 14


def _coarse_code(c):
    return pltpu.bitcast(lax.shift_left(c + 0x3F80, 16), F32).astype(BF16)


def _dsa_prompt_kernel(q_ref, qi_ref, kiwi_ref, gd_ref, k_ref, vt_ref, ki_ref, o_ref, key_ref, coarse_ref, *, n_sel):
    tq, tk = DSA_TQ, DSA_TK
    t0 = pl.program_id(1) * tq
    n_kt = (t0 + tq + tk - 1) // tk
    qi = qi_ref[0]
    wi_t = kiwi_ref[0].T[DI_D:DI_D + HI_D, :]
    pos_q = t0 + lax.broadcasted_iota(jnp.int32, (1, tq), 1)
    row_k = lax.broadcasted_iota(jnp.int32, (tk, tq), 0)

    def key_slice(kt):
        return pl.ds(pl.multiple_of(kt * tk, tk), tk)

    def fold_rows(x):
        x = x.reshape(tk // 8, 8, tq)
        while x.shape[0] > 1:
            half = x.shape[0] // 2
            x = x[:half] + x[half:]
        return x[0]

    qi_rows = jnp.concatenate([qi[:, h * DI_D:(h + 1) * DI_D] for h in range(HI_D)], axis=0)

    def score_body(kt, carry):
        ks = key_slice(kt)
        d = lax.dot_general(ki_ref[0, ks, :], qi_rows, (((1,), (1,)), ((), ())), preferred_element_type=F32)
        acc = sum(wi_t[h:h + 1, :] * jnp.maximum(d[:, h * tq:(h + 1) * tq], 0.0) for h in range(HI_D))
        key = _sortable_key(acc * IDX_SCALE)
        key = jnp.where(kt * tk + row_k <= pos_q, key, INT_MIN)
        key_ref[ks, :] = key
        coarse_ref[ks, :] = _coarse_code(lax.shift_right_logical(key ^ INT_MIN, 32 - COARSE_BITS))
        return carry

    lax.fori_loop(0, n_kt, score_body, 0)

    def count(pred_fn):
        def body(kt, c):
            return c + fold_rows(jnp.where(pred_fn(key_ref[key_slice(kt), :]), 1.0, 0.0))
        c = lax.fori_loop(0, n_kt, body, jnp.zeros((8, tq), F32))
        return jnp.sum(c, axis=0, keepdims=True)

    def coarse_body(i, ans):
        cand = ans | lax.shift_left(jnp.int32(1), COARSE_BITS - 1 - i)
        code = _coarse_code(cand)
        one, zero = jnp.ones((tk, tq), BF16), jnp.zeros((tk, tq), BF16)

        def body(kt, c):
            hit = jnp.where(coarse_ref[key_slice(kt), :] >= code, one, zero).reshape(tk // 16, 16, tq)
            while hit.shape[0] > 1:
                half = hit.shape[0] // 2
                hit = hit[:half] + hit[half:]
            return c + hit[0]

        c = lax.fori_loop(0, n_kt, body, jnp.zeros((16, tq), BF16))
        cnt = jnp.sum(c.astype(F32), axis=0, keepdims=True)
        return jnp.where(cnt >= n_sel, cand, ans)

    def bit_body(i, ans):
        cand = ans | lax.shift_left(jnp.int32(1), 31 - COARSE_BITS - i)
        scand = cand ^ INT_MIN
        cnt = count(lambda key: key >= scand)
        return jnp.where(cnt >= n_sel, cand, ans)

    top = lax.fori_loop(0, COARSE_BITS, coarse_body, jnp.zeros((1, tq), jnp.int32))
    thr = lax.fori_loop(0, 32 - COARSE_BITS, bit_body, lax.shift_left(top, 32 - COARSE_BITS)) ^ INT_MIN

    def rank_ties():
        need = n_sel - count(lambda key: key > thr)
        tri = (lax.broadcasted_iota(jnp.int32, (tk, tk), 1)
               <= lax.broadcasted_iota(jnp.int32, (tk, tk), 0)).astype(BF16)

        def sel_body(kt, tie_seen):
            ks = key_slice(kt)
            key = key_ref[ks, :]
            tie = (key == thr) & (key != INT_MIN)
            tie_f = jnp.where(tie, 1.0, 0.0)
            rank = tie_seen + jnp.dot(tri, tie_f.astype(BF16), preferred_element_type=F32)
            key_ref[ks, :] = jnp.where((key > thr) | (tie & (rank <= need)), 1, 0)
            return tie_seen + jnp.sum(tie_f, axis=0, keepdims=True)

        lax.fori_loop(0, n_kt, sel_body, jnp.zeros((1, tq), F32))

    def take_all():
        def sel_body(kt, carry):
            ks = key_slice(kt)
            key = key_ref[ks, :]
            key_ref[ks, :] = jnp.where((key >= thr) & (key != INT_MIN), 1, 0)
            return carry

        lax.fori_loop(0, n_kt, sel_body, 0)

    lax.cond(jnp.max(count(lambda key: key >= thr)) > n_sel, rank_ties, take_all)

    c_exp = (DH_D ** -0.5) * math.log2(math.e)
    ones_rows = jnp.ones((DSA_ONES, tk), BF16)
    qs = [q_ref[0, :, h * DH_D:(h + 1) * DH_D] for h in range(H_D)]

    def att_body(kt, carry):
        m, accs = carry
        ks = key_slice(kt)
        sel = key_ref[ks, :] != 0
        hs = range(H_D)
        kts = [k_ref[0, ks, j * DH_D:(j + 1) * DH_D] for j in range(KVH_D)]
        vts = [jnp.concatenate([vt_ref[0, j * DH_D:(j + 1) * DH_D, ks], ones_rows], axis=0) for j in range(KVH_D)]
        logits = [lax.dot_general(kts[h // QPK_D], qs[h], (((1,), (1,)), ((), ())), preferred_element_type=F32)
                  for h in hs]
        logits = [jnp.where(sel, x, NEG_BIG) for x in logits]
        m_rows = [jnp.maximum(m[h:h + 1, :], jnp.max(x, axis=0, keepdims=True)) for h, x in zip(hs, logits)]
        ps = [jnp.exp2((x - mn) * c_exp).astype(BF16) for x, mn in zip(logits, m_rows)]
        scales = [jnp.exp2((m[h:h + 1, :] - mn) * c_exp) for h, mn in zip(hs, m_rows)]
        pv = [jnp.dot(vts[h // QPK_D], p, preferred_element_type=F32) for h, p in zip(hs, ps)]
        new_accs = [accs[h] * sc + x for h, sc, x in zip(hs, scales, pv)]
        return jnp.concatenate(m_rows, axis=0), tuple(new_accs)

    init = (jnp.full((H_D, tq), NEG_BIG, F32), tuple(jnp.zeros((DH_D + DSA_ONES, tq), F32) for _ in range(H_D)))
    _, accs = lax.fori_loop(0, n_kt, att_body, init)
    out = jnp.concatenate([(a[:DH_D] / a[DH_D:DH_D + 1]).T for a in accs], axis=1)
    gd = gd_ref[0]
    o_ref[0] = out * (gd * _sigmoid(gd))


def dsa_prompt_pallas(proj3d, q_b, qi_b, k_b, vt_b, ki_b):
    b, l, _ = proj3d.shape
    n_sel = min(TOPK_MAX, l // 4)
    assert l % DSA_TQ == 0 and l % DSA_TK == 0 and l // 16 <= 256
    qblk = lambda w, j: pl.BlockSpec((1, DSA_TQ, w), lambda bi, ti: (bi, ti, j))
    full = lambda w: pl.BlockSpec((1, l, w), lambda bi, ti: (bi, 0, 0))
    return pl.pallas_call(
        functools.partial(_dsa_prompt_kernel, n_sel=float(n_sel)),
        grid=(b, l // DSA_TQ),
        in_specs=[qblk(D_D, 0), qblk(QI_W, 0), qblk(128, OD_KIWI // 128), qblk(D_D, OD_GD // D_D),
                  full(KV_W), pl.BlockSpec((1, KV_W, l), lambda bi, ti: (bi, 0, 0)), full(DI_D)],
        out_specs=pl.BlockSpec((1, DSA_TQ, D_D), lambda bi, ti: (bi, ti, 0)),
        out_shape=jax.ShapeDtypeStruct((b, l, D_D), F32),
        scratch_shapes=[pltpu.VMEM((l, DSA_TQ), jnp.int32), pltpu.VMEM((l, DSA_TQ), BF16)],
        compiler_params=pltpu.CompilerParams(dimension_semantics=("parallel", "arbitrary")),
        name="dsa_prompt",
    )(q_b, qi_b, proj3d, proj3d, k_b, vt_b, ki_b)


S5_N = G_C * P_C
S5_ROWS = 2 * S5_N // 128
S5_TC = 256


def _s5_kernel(u_ref, g_ref, h0re_ref, h0im_ref, lam_t_ref, lam_r_ref, bd_ref, cd_ref, dsk_ref, gw_ref, gb_ref,
               o_ref, hre_ref, him_ref, scr_ref, h_ref, *, bb, tc, n_valid):
    c = pl.program_id(1)
    rows = bb * tc
    half = S5_ROWS // 2

    @pl.when(c == 0)
    def _():
        for i in range(bb):
            h_ref[i * S5_ROWS:i * S5_ROWS + half] = h0re_ref[i]
            h_ref[i * S5_ROWS + half:(i + 1) * S5_ROWS] = h0im_ref[i]

    def discretize(lam_ref):
        lr, li, dt = lam_ref[0], lam_ref[1], jnp.exp(lam_ref[2])
        mag = jnp.exp(lr * dt)
        br, bi = mag * jnp.cos(li * dt), mag * jnp.sin(li * dt)
        den = lr * lr + li * li
        fr = ((br - 1.0) * lr + bi * li) / den
        fi = (bi * lr - (br - 1.0) * li) / den
        return br, bi, fr, fi

    _, _, fr, fi = discretize(lam_r_ref)
    u = u_ref[...].reshape(rows, D_C)
    hu, hn = D_C // 2, S5_N // 2
    b_re, b_im = (jnp.concatenate([_mm(u[:, i * hu:(i + 1) * hu], bd_ref[i * hu:(i + 1) * hu, o + i * hn:o + (i + 1) * hn])
                                   for i in range(2)], axis=1) for o in (0, S5_N))
    bu = jnp.concatenate([fr * b_re - fi * b_im, fr * b_im + fi * b_re], axis=1)
    for r in range(S5_ROWS):
        scr_ref[pl.ds(r, rows, stride=S5_ROWS), :] = bu[:, r * 128:(r + 1) * 128]

    lbr, lbi, _, _ = discretize(lam_t_ref)
    for i in range(bb):
        def step(t, h):
            hr, hi = h
            idx = pl.multiple_of((i * tc + t) * S5_ROWS, S5_ROWS)
            blk = scr_ref[pl.ds(idx, S5_ROWS), :]
            nr = lbr * hr - lbi * hi + blk[:half]
            ni = lbr * hi + lbi * hr + blk[half:]
            scr_ref[pl.ds(idx, S5_ROWS), :] = jnp.concatenate([nr, ni], axis=0)
            return nr, ni

        h_in = (h_ref[i * S5_ROWS:i * S5_ROWS + half], h_ref[i * S5_ROWS + half:(i + 1) * S5_ROWS])
        hr, hi = lax.fori_loop(0, n_valid, step, h_in, unroll=4)
        h_ref[i * S5_ROWS:i * S5_ROWS + half] = hr
        h_ref[i * S5_ROWS + half:(i + 1) * S5_ROWS] = hi
        hre_ref[i] = hr
        him_ref[i] = hi

    h_tiles = [scr_ref[pl.ds(r, rows, stride=S5_ROWS), :] for r in range(S5_ROWS)]
    n_out = D_C // 128
    per = half // n_out
    y_tiles = []
    for t in range(n_out):
        src = [p * half + t * per + i for p in range(2) for i in range(per)]
        y_tiles.append(_mm(jnp.concatenate([h_tiles[r] for r in src], axis=1),
                           jnp.concatenate([cd_ref[r * 128:(r + 1) * 128, t * 128:(t + 1) * 128] for r in src], axis=0)))
    y = dsk_ref[...] * u + jnp.concatenate(y_tiles, axis=1)
    y = 0.5 * y * (1.0 + jnp.tanh(math.sqrt(2.0 / math.pi) * (y + 0.044715 * (y * y * y))))
    y = y * _sigmoid(_mm(y, gw_ref[...]) + gb_ref[...])
    gate = g_ref[...].reshape(rows, D_C)
    o_ref[...] = (y * (gate * _sigmoid(gate))).reshape(bb, tc, D_C)


def s5_pallas(proj3d, n_valid_tokens, h0_re, h0_im, lam_re, lam_im, log_dt, b_re, b_im, c_re, c_im, d_skip,
              glu_w, glu_b):
    b, l, _ = proj3d.shape
    tc = min(S5_TC, l)
    n_chunks = l // tc
    bb = max(1, min(b, S5_TC // tc))
    assert l % tc == 0 and b % bb == 0 and tc % 8 == 0
    n_valid = n_valid_tokens - (n_chunks - 1) * tc
    assert 0 < n_valid <= tc and (n_valid == tc or n_chunks == 1)
    half = S5_ROWS // 2
    dt_full = jnp.repeat(log_dt, P_C)
    lam_t = jnp.stack([lam_re.reshape(half, 128), lam_im.reshape(half, 128), dt_full.reshape(half, 128)])
    lam_r = jnp.stack([lam_re.reshape(1, S5_N), lam_im.reshape(1, S5_N), dt_full.reshape(1, S5_N)])
    eye = jnp.eye(G_C, dtype=F32)
    bd = jnp.concatenate([jnp.einsum('gpm,gh->gmhp', t, eye).reshape(D_C, S5_N) for t in (b_re, b_im)], axis=1)
    cd = jnp.concatenate([jnp.einsum('gmp,gh->gphm', t, eye).reshape(S5_N, D_C) for t in (c_re, -c_im)], axis=0)
    const = lambda shape: pl.BlockSpec(shape, lambda bi, ci: (0,) * len(shape))
    st_spec = pl.BlockSpec((bb, half, 128), lambda bi, ci: (bi, 0, 0))
    out, h_re, h_im = pl.pallas_call(
        functools.partial(_s5_kernel, bb=bb, tc=tc, n_valid=n_valid),
        grid=(b // bb, n_chunks),
        in_specs=[
            pl.BlockSpec((bb, tc, D_C), lambda bi, ci: (bi, ci, OD_U // D_C)),
            pl.BlockSpec((bb, tc, D_C), lambda bi, ci: (bi, ci, OD_GC // D_C)),
            st_spec, st_spec,
            const((3, half, 128)), const((3, 1, S5_N)), const((D_C, 2 * S5_N)), const((2 * S5_N, D_C)),
            const((1, D_C)), const((D_C, D_C)), const((1, D_C)),
        ],
        out_specs=[pl.BlockSpec((bb, tc, D_C), lambda bi, ci: (bi, ci, 0)), st_spec, st_spec],
        out_shape=[jax.ShapeDtypeStruct((b, l, D_C), F32), jax.ShapeDtypeStruct((b, half, 128), F32),
                   jax.ShapeDtypeStruct((b, half, 128), F32)],
        scratch_shapes=[pltpu.VMEM((bb * tc * S5_ROWS, 128), F32), pltpu.VMEM((bb * S5_ROWS, 128), F32)],
        compiler_params=pltpu.CompilerParams(dimension_semantics=("parallel", "arbitrary")),
        name="s5_scan",
    )(proj3d, proj3d, h0_re.reshape(b, half, 128), h0_im.reshape(b, half, 128), lam_t, lam_r,
      bd.astype(BF16), cd.astype(BF16), d_skip.reshape(1, D_C), glu_w.astype(BF16), glu_b.reshape(1, D_C))
    return out, h_re.reshape(b, G_C, P_C), h_im.reshape(b, G_C, P_C)


def _hi_mid_lo(x):
    hi = x.astype(BF16)
    r1 = x - hi.astype(F32)
    mid = r1.astype(BF16)
    lo = (r1 - mid.astype(F32)).astype(BF16)
    return hi, mid, lo


def _mm3_rhs01(x, b_bf16):
    return sum(jnp.dot(p, b_bf16, preferred_element_type=F32) for p in _hi_mid_lo(x))


def _ssd_kernel(z_ref, xbc_ref, dt_ref, conv0_ref, ssm0_ref, cw_ref, cb_ref, sm_ref, pv_ref, e16_ref, bones_ref,
                o_ref, ssm_ref, convn_ref, buf_ref, st_ref, *, cl, n_valid):
    c = pl.program_id(1)
    keep = CONV_W - 1

    @pl.when(c == 0)
    def _():
        buf_ref[8 - keep:8] = conv0_ref[0]
        st_ref[...] = ssm0_ref[0]

    buf_ref[8:8 + cl] = xbc_ref[0]
    cw = cw_ref[...]
    conv = cb_ref[...] + sum(buf_ref[8 - keep + i:8 - keep + i + cl] * cw[i:i + 1] for i in range(CONV_W))
    convn_ref[0] = buf_ref[8 + n_valid - keep:8 + n_valid]
    buf_ref[8 - keep:8] = buf_ref[8 + cl - keep:8 + cl]
    act = conv * _sigmoid(conv)
    xs, bm, cm = act[:, :D_B], act[:, D_B:D_B + G_B * N_B], act[:, D_B + G_B * N_B:]

    sm = sm_ref[...]
    dt = _softplus(dt_ref[0][:, :H_B] + sm[0:1, :H_B])
    row1 = lax.broadcasted_iota(jnp.int32, (cl, 1), 0)
    if n_valid < cl:
        dt = jnp.where(row1 < n_valid, dt, 0.0)
    a = dt * (-jnp.exp(sm[1:2, :H_B]))
    row = lax.broadcasted_iota(jnp.int32, (cl, cl), 0)
    col = lax.broadcasted_iota(jnp.int32, (cl, cl), 1)
    incl = col <= row
    a_parts = _hi_mid_lo(a)
    a_cum = sum(jnp.dot(incl.astype(BF16), p, preferred_element_type=F32) for p in a_parts)
    a_cum_t = sum(lax.dot_general(p, (row <= col).astype(BF16), (((0,), (0,)), ((), ())),
                                  preferred_element_type=F32) for p in a_parts)
    a_last = a_cum[cl - 1:cl]
    e16 = e16_ref[...]
    xdt = xs * _mm3_rhs01(dt, e16)
    ea_full = jnp.exp(_mm3_rhs01(a_cum, e16))
    xdec = xdt * jnp.exp(_mm3_rhs01(a_last - a_cum, e16))
    chunk_decay = jnp.exp(a_last)

    groups = range(G_B)
    gsl = [slice(g * N_B, (g + 1) * N_B) for g in groups]
    cbs = [_mm_nt(cm[:, s], bm[:, s]) for s in gsl]
    y_off = jnp.concatenate(
        [_mm_nt(cm[:, s], st_ref[g * HPG_B:(g + 1) * HPG_B].reshape(HPG_B * P_B, N_B)) for g, s in zip(groups, gsl)],
        axis=1)
    y_diag = []
    for h in range(H_B):
        hs = slice(h * P_B, (h + 1) * P_B)
        lmat = jnp.exp(jnp.where(incl, a_cum[:, h:h + 1] - a_cum_t[h:h + 1, :], NEG_BIG))
        y_diag.append(_mm(cbs[h // HPG_B] * lmat, xdt[:, hs]))
    for h in range(H_B):
        hs = slice(h * P_B, (h + 1) * P_B)
        st_ref[h] = st_ref[h] * chunk_decay[:, h:h + 1] + _mm_tn(xdec[:, hs], bm[:, gsl[h // HPG_B]])

    pv = pv_ref[...]
    y = jnp.concatenate(y_diag, axis=1) + y_off * ea_full + xs * pv[0:1]
    z = z_ref[0]
    y = y * (z * _sigmoid(z))
    ms = _mm_exact_rhs(y * y, bones_ref[...]) * (1.0 / (D_B // G_B))
    o_ref[0] = y * lax.rsqrt(ms + NORM_EPS) * pv[1:2]

    @pl.when(c == pl.num_programs(1) - 1)
    def _():
        ssm_ref[0] = st_ref[...]


def ssd_pallas(proj3d, n_valid_tokens, conv0, ssm0, conv_w, conv_b, dt_bias, a_log, d_skip, gnorm_g):
    b, l, _ = proj3d.shape
    cl = min(SSD_CHUNK, l)
    n_chunks = l // cl
    assert l % cl == 0 and cl % 8 == 0
    n_valid = n_valid_tokens - (n_chunks - 1) * cl
    assert 0 < n_valid <= cl and (n_valid == cl or n_chunks == 1)
    sm = jnp.zeros((8, 128), F32).at[0, :H_B].set(dt_bias).at[1, :H_B].set(a_log)
    pv = jnp.zeros((8, D_B), F32).at[0].set(jnp.repeat(d_skip, P_B)).at[1].set(gnorm_g)
    e16 = (jnp.arange(D_B)[None, :] // P_B == jnp.arange(H_B)[:, None]).astype(BF16)
    gid = jnp.arange(D_B) // (D_B // G_B)
    bones = (gid[:, None] == gid[None, :]).astype(BF16)
    const = lambda shape: pl.BlockSpec(shape, lambda bi, ci: (0,) * len(shape))
    return pl.pallas_call(
        functools.partial(_ssd_kernel, cl=cl, n_valid=n_valid),
        grid=(b, n_chunks),
        in_specs=[
            pl.BlockSpec((1, cl, D_B), lambda bi, ci: (bi, ci, EV_Z // D_B)),
            pl.BlockSpec((1, cl, CONV_DIM), lambda bi, ci: (bi, ci, EV_XBC // CONV_DIM)),
            pl.BlockSpec((1, cl, 128), lambda bi, ci: (bi, ci, EV_DT // 128)),
            pl.BlockSpec((1, CONV_W - 1, CONV_DIM), lambda bi, ci: (bi, 0, 0)),
            pl.BlockSpec((1, H_B, P_B, N_B), lambda bi, ci: (bi, 0, 0, 0)),
            const((CONV_W, CONV_DIM)), const((1, CONV_DIM)), const((8, 128)), const((8, D_B)),
            const((H_B, D_B)), const((D_B, D_B)),
        ],
        out_specs=[
            pl.BlockSpec((1, cl, D_B), lambda bi, ci: (bi, ci, 0)),
            pl.BlockSpec((1, H_B, P_B, N_B), lambda bi, ci: (bi, 0, 0, 0)),
            pl.BlockSpec((1, CONV_W - 1, CONV_DIM), lambda bi, ci: (bi, 0, 0)),
        ],
        out_shape=[jax.ShapeDtypeStruct((b, l, D_B), F32), jax.ShapeDtypeStruct((b, H_B, P_B, N_B), F32),
                   jax.ShapeDtypeStruct((b, CONV_W - 1, CONV_DIM), F32)],
        scratch_shapes=[pltpu.VMEM((8 + cl, CONV_DIM), F32), pltpu.VMEM((H_B, P_B, N_B), F32)],
        compiler_params=pltpu.CompilerParams(dimension_semantics=("parallel", "arbitrary")),
        name="ssd_chunked",
    )(proj3d, proj3d, proj3d, conv0, ssm0, conv_w, conv_b.reshape(1, CONV_DIM), sm, pv, e16, bones)


RADIX_BITS = 4
DSA_SAMPLE_BATCH = 2


def _dsa_sample_kernel(pt_ref, q_ref, qi_ref, kiwi_ref, gd_ref, kn_ref, vn_ref, kin_ref, *rest, n_pages, n_sel, tq, bb):
    per = bb * n_pages
    kp_refs = [rest[i * n_pages:(i + 1) * n_pages] for i in range(bb)]
    vp_refs = [rest[per + i * n_pages:per + (i + 1) * n_pages] for i in range(bb)]
    kip_refs = [rest[2 * per + i * n_pages:2 * per + (i + 1) * n_pages] for i in range(bb)]
    o_ref = rest[3 * per]
    ps = PAGE_SIZE
    n_tiles = n_pages + 1
    bs = range(bb)
    pad_rows = lambda x: jnp.concatenate([x, jnp.zeros((ps - tq, x.shape[1]), x.dtype)], axis=0)

    qs_i = [jnp.concatenate([qi_ref[i][:, h * DI_D:(h + 1) * DI_D] for h in range(HI_D)], axis=0) for i in bs]
    wcol = [jnp.concatenate([kiwi_ref[i][:, DI_D + h:DI_D + h + 1] for h in range(HI_D)], axis=0) for i in bs]
    dots = [[_mm(qs_i[i], r[0]) for r in kip_refs[i]] + [_mm_nt(qs_i[i], pad_rows(kin_ref[i]))] for i in bs]
    n_keys = n_tiles * ps
    qrow = lax.broadcasted_iota(jnp.int32, (tq, n_keys), 0)
    kcol = lax.broadcasted_iota(jnp.int32, (tq, n_keys), 1)
    admissible = kcol - n_pages * ps <= qrow
    keys = []
    for i in bs:
        scores = []
        for d in dots[i]:
            d = wcol[i] * jnp.maximum(d, 0.0)
            scores.append(sum(d[h * tq:(h + 1) * tq] for h in range(HI_D)))
        keys.append(jnp.where(admissible, _sortable_key(jnp.concatenate(scores, axis=1) * IDX_SCALE), INT_MIN))

    def count(pred):
        return jnp.sum(jnp.where(pred, 1.0, 0.0), axis=1, keepdims=True)

    ans = [jnp.zeros((tq, 1), jnp.int32) for _ in bs]
    for shift in range(32 - RADIX_BITS, -1, -RADIX_BITS):
        digits = [jnp.zeros((tq, 1), jnp.int32) for _ in bs]
        for d in range(1, 2 ** RADIX_BITS):
            step = d << shift if d << shift < 2 ** 31 else (d << shift) - 2 ** 32
            digits = [dg + jnp.where(count(k >= ((a | step) ^ INT_MIN)) >= n_sel, 1, 0)
                      for dg, k, a in zip(digits, keys, ans)]
        ans = [a | lax.shift_left(dg, shift) for a, dg in zip(ans, digits)]
    thrs = [a ^ INT_MIN for a in ans]
    gts = [k > t for k, t in zip(keys, thrs)]
    needs = [n_sel - count(g) for g in gts]
    ties = [(k == t) & (k != INT_MIN) for k, t in zip(keys, thrs)]
    tie_fs = [jnp.where(t, 1.0, 0.0) for t in ties]
    tri = (lax.broadcasted_iota(jnp.int32, (ps, ps), 0) <= lax.broadcasted_iota(jnp.int32, (ps, ps), 1)).astype(BF16)
    sels = []
    for i in bs:
        seen = jnp.zeros((tq, 1), F32)
        ranks = []
        for t in range(n_tiles):
            tf = tie_fs[i][:, t * ps:(t + 1) * ps]
            ranks.append(seen + jnp.dot(tf.astype(BF16), tri, preferred_element_type=F32))
            seen = seen + jnp.sum(tf, axis=1, keepdims=True)
        sels.append(jnp.where(gts[i] | (ties[i] & (jnp.concatenate(ranks, axis=1) <= needs[i])), 1.0, 0.0).astype(BF16))

    krow = lax.broadcasted_iota(jnp.int32, (ps, KVH_D * ps), 0)
    ccol = lax.broadcasted_iota(jnp.int32, (ps, KVH_D * ps), 1)
    spread = [(ccol == KVH_D * krow + j).astype(BF16) for j in range(KVH_D)]
    masks = []
    for i in bs:
        parts = []
        for t in range(n_pages):
            st = sels[i][:, t * ps:(t + 1) * ps]
            per_kv = [jnp.dot(st, spread[j], preferred_element_type=F32) for j in range(KVH_D)]
            parts.append(jnp.concatenate([per_kv[h // QPK_D] for h in range(H_D)], axis=0))
        parts.append(jnp.concatenate([sels[i][:, n_pages * ps:].astype(F32)] * H_D, axis=0))
        masks.append(jnp.concatenate(parts, axis=1) > 0.5)

    half = QPK_D * tq
    w2 = KVH_D * ps
    qs = [jnp.concatenate([q_ref[i][:, h * DH_D:(h + 1) * DH_D] for h in range(H_D)], axis=0) for i in bs]
    kns = [pad_rows(kn_ref[i]) for i in bs]
    vns = [pad_rows(vn_ref[i]) for i in bs]
    logits = []
    for i in bs:
        new_logits = jnp.concatenate([_mm_nt(qs[i][j * half:(j + 1) * half], kns[i][:, j * DH_D:(j + 1) * DH_D])
                                      for j in range(KVH_D)], axis=0)
        lg = jnp.concatenate([_mm_nt(qs[i], r[0].astype(BF16)) for r in kp_refs[i]] + [new_logits], axis=1)
        logits.append(jnp.where(masks[i], lg * (DH_D ** -0.5), NEG_BIG))
    ms = [jnp.max(x, axis=1, keepdims=True) for x in logits]
    ps_ = [jnp.where(mk, jnp.exp(x - m), 0.0) for mk, x, m in zip(masks, logits, ms)]
    ls = [jnp.sum(p, axis=1, keepdims=True) for p in ps_]
    for i in bs:
        pb = ps_[i].astype(BF16)
        acc = sum(jnp.dot(pb[:, t * w2:(t + 1) * w2], r[0].astype(BF16), preferred_element_type=F32)
                  for t, r in enumerate(vp_refs[i]))
        p_new = pb[:, n_pages * w2:]
        acc = acc + jnp.concatenate([jnp.dot(p_new[j * half:(j + 1) * half], vns[i][:, j * DH_D:(j + 1) * DH_D],
                                             preferred_element_type=F32) for j in range(KVH_D)], axis=0)
        o = acc / ls[i]
        gd = gd_ref[i]
        o_ref[i] = jnp.concatenate([o[h * tq:(h + 1) * tq] for h in range(H_D)], axis=1) * (gd * _sigmoid(gd))


def dsa_sample_pallas(proj3d, q_b, qi_b, k_b, v_b, ki_b, cache_k, cache_v, cache_ki, page_table, n_valid_tokens):
    b, tq, _ = proj3d.shape
    n_pages = page_table.shape[1]
    bb = math.gcd(b, DSA_SAMPLE_BATCH)
    n_sel = min(TOPK_MAX, (n_pages * PAGE_SIZE + n_valid_tokens) // 4)
    row = lambda w, j=0: pl.BlockSpec((bb, tq, w), lambda bi, pt, j=j: (bi, 0, j))
    pages = lambda shape: [pl.BlockSpec((1,) + shape, lambda bi, pt, i=i, p=p: (pt[bi * bb + i, p], 0, 0))
                           for i in range(bb) for p in range(n_pages)]
    grid_spec = pltpu.PrefetchScalarGridSpec(
        num_scalar_prefetch=1,
        grid=(b // bb,),
        in_specs=[row(D_D), row(QI_W), row(128, OD_KIWI // 128), row(D_D, OD_GD // D_D), row(KV_W), row(KV_W), row(DI_D)]
        + pages((PAGE_SIZE * KVH_D, DH_D)) + pages((PAGE_SIZE * KVH_D, DH_D)) + pages((DI_D, PAGE_SIZE)),
        out_specs=pl.BlockSpec((bb, tq, D_D), lambda bi, pt: (bi, 0, 0)),
    )
    n_ops = bb * n_pages
    return pl.pallas_call(
        functools.partial(_dsa_sample_kernel, n_pages=n_pages, n_sel=float(n_sel), tq=tq, bb=bb),
        grid_spec=grid_spec,
        out_shape=jax.ShapeDtypeStruct((b, tq, D_D), F32),
        compiler_params=pltpu.CompilerParams(dimension_semantics=("parallel",)),
        name="dsa_sample",
    )(page_table, q_b, qi_b, proj3d, proj3d, k_b, v_b, ki_b, *([cache_k] * n_ops), *([cache_v] * n_ops),
      *([cache_ki] * n_ops))


def _split(x, sizes):
    return jnp.split(x, np.cumsum(sizes)[:-1].tolist(), axis=-1)


def even_mixer(x, g, lv, shift0, wkv0, conv0, ssm0, w_in, w_out, mu, w0, w2, a0, a2, k_k, k_a, r_k,
               ln_g, ln_b, conv_w, conv_b, dt_bias, a_log, d_skip, gnorm_g):
    b, l, d = x.shape
    pa_w, ga_w, z_w, xbc_w, dt_w = _split(w_in, [A_SHIFT, D_A, D_B, CONV_DIM, H_B])
    w_ev = jnp.concatenate([ga_w, z_w, xbc_w, pa_w, dt_w, jnp.zeros((d, EV_N - IN_E), w_in.dtype)], axis=1)
    proj = norm_matmul(x.reshape(b * l, d), g, w_ev.astype(BF16)).reshape(b, l, EV_N)
    out_a, wkv_new, shift_new = rwkv7_pallas(proj, shift0, wkv0, mu, w0, w2, a0, a2, k_k, k_a, r_k, ln_g, ln_b, lv)
    out_b, ssm_new, conv_new = ssd_pallas(proj, lv, conv0, ssm0, conv_w, conv_b, dt_bias, a_log, d_skip, gnorm_g)
    x_new = matmul_residual(out_a.reshape(b * l, D_A), out_b.reshape(b * l, D_B), w_out.astype(BF16),
                            x.reshape(b * l, d)).reshape(b, l, d)
    return x_new, (wkv_new, shift_new, ssm_new, conv_new)


def odd_mixer(x, g, lv, pos, c_re0, c_im0, attend, w_in, w_out, lam_re, lam_im, log_dt, b_re, b_im,
              c_re, c_im, d_skip, glu_w, glu_b):
    b, l, d = x.shape
    u_w, gc_w, q_w, k_w, v_w, qi_w, ki_w, wi_w, gd_w = _split(
        w_in, [D_C, D_C, D_D, KV_W, KV_W, QI_W, DI_D, HI_D, D_D])
    w_od = jnp.concatenate([u_w, gc_w, q_w, gd_w, k_w, v_w, qi_w, ki_w, wi_w,
                            jnp.zeros((d, OD_N - IN_O), w_in.dtype)], axis=1)
    proj = norm_matmul(x.reshape(b * l, d), g, w_od.astype(BF16)).reshape(b, l, OD_N)
    out_c, re_last, im_last = s5_pallas(proj, lv, c_re0, c_im0, lam_re, lam_im, log_dt, b_re, b_im,
                                        c_re, c_im, d_skip, glu_w, glu_b)
    q_b, qi_b, k_f, v_f, ki, k_b, v_b, ki_b = rope_pallas(proj, pos, v_transposed=attend is None)
    if attend is None:
        out_d = dsa_prompt_pallas(proj, q_b, qi_b, k_b, v_b, ki_b)
    else:
        out_d = dsa_sample_pallas(proj, q_b, qi_b, k_b, v_b, ki_b, *attend, lv)
    k = k_f[:, :lv].reshape(b, lv, KVH_D, DH_D)
    v = v_f[:, :lv].reshape(b, lv, KVH_D, DH_D)
    ki = ki[:, :lv]
    x_new = matmul_residual(out_c.reshape(b * l, D_C), out_d.reshape(b * l, D_D), w_out.astype(BF16),
                            x.reshape(b * l, d)).reshape(b, l, d)
    return x_new, (re_last, im_last, k, v, ki)


def kernel(x_prompt, x_sample, state_a_wkv, state_a_shift, state_b_ssm, state_b_conv, state_c_re, state_c_im, cache_d_k, cache_d_v, cache_d_kidx, page_table, norm_g, final_norm_g, w_in_e, w_out_e, rwkv_mu, rwkv_w0, rwkv_w2, rwkv_a0, rwkv_a2, rwkv_kk, rwkv_ka, rwkv_rk, rwkv_ln_g, rwkv_ln_b, ssd_conv_w, ssd_conv_b, ssd_dt_bias, ssd_a_log, ssd_d, ssd_norm_g, w_in_o, w_out_o, s5_lam_re, s5_lam_im, s5_log_dt, s5_b_re, s5_b_im, s5_c_re, s5_c_im, s5_d, s5_glu_w, s5_glu_b):
    f32 = jnp.float32
    bp, lp = x_prompt.shape[:2]
    bs, ls = x_sample.shape[:2]
    pos_p = jnp.arange(lp)
    ls_pad = _round_up(ls, 8)
    pos_s = PAST_LEN + jnp.arange(ls_pad)
    xp, xs = x_prompt, jnp.pad(x_sample, ((0, 0), (0, ls_pad - ls), (0, 0)))
    even_p, even_s, odd_p, odd_s = [], [], [], []
    for i in range(DEPTH):
        j = i // 2
        if i % 2 == 0:
            pe = (w_in_e[j], w_out_e[j], rwkv_mu[j], rwkv_w0[j], rwkv_w2[j], rwkv_a0[j], rwkv_a2[j],
                  rwkv_kk[j], rwkv_ka[j], rwkv_rk[j], rwkv_ln_g[j], rwkv_ln_b[j], ssd_conv_w[j],
                  ssd_conv_b[j], ssd_dt_bias[j], ssd_a_log[j], ssd_d[j], ssd_norm_g[j])
            xp, st_p = even_mixer(xp, norm_g[i], lp, jnp.zeros((bp, A_SHIFT), f32), jnp.zeros((bp, H_A, HD_A, HD_A), f32),
                                  jnp.zeros((bp, CONV_W - 1, CONV_DIM), f32),
                                  jnp.zeros((bp, H_B, P_B, N_B), f32), *pe)
            xs, st_s = even_mixer(xs, norm_g[i], ls, state_a_shift[j], state_a_wkv[j], state_b_conv[j], state_b_ssm[j], *pe)
            even_p.append(st_p)
            even_s.append(st_s)
        else:
            po = (w_in_o[j], w_out_o[j], s5_lam_re[j], s5_lam_im[j], s5_log_dt[j], s5_b_re[j], s5_b_im[j],
                  s5_c_re[j], s5_c_im[j], s5_d[j], s5_glu_w[j], s5_glu_b[j])
            zc = jnp.zeros((bp, G_C, P_C), f32)
            xp, st_p = odd_mixer(xp, norm_g[i], lp, pos_p, zc, zc, None, *po)
            n_pool = cache_d_k.shape[1]
            attend_s = (cache_d_k.reshape(-1, PAGE_SIZE * KVH_D, DH_D), cache_d_v.reshape(-1, PAGE_SIZE * KVH_D, DH_D),
                        jnp.swapaxes(cache_d_kidx, 2, 3).reshape(-1, DI_D, PAGE_SIZE), page_table + j * n_pool)
            xs, st_s = odd_mixer(xs, norm_g[i], ls, pos_s, state_c_re[j], state_c_im[j], attend_s, *po)
            odd_p.append(st_p)
            odd_s.append(st_s)
    y_prompt = rmsnorm_rows(xp.reshape(bp * lp, D_MODEL), final_norm_g).reshape(bp, lp, D_MODEL)
    y_sample = rmsnorm_rows(xs.reshape(bs * ls_pad, D_MODEL), final_norm_g).reshape(bs, ls_pad, D_MODEL)[:, :ls]
    new_a_wkv_p, new_a_shift_p, new_b_ssm_p, new_b_conv_p = [jnp.stack(t) for t in zip(*even_p)]
    new_a_wkv_s, new_a_shift_s, new_b_ssm_s, new_b_conv_s = [jnp.stack(t) for t in zip(*even_s)]
    new_c_re_p, new_c_im_p, new_d_k_p, new_d_v_p, new_d_kidx_p = [jnp.stack(t) for t in zip(*odd_p)]
    new_c_re_s, new_c_im_s, new_d_k_s, new_d_v_s, new_d_kidx_s = [jnp.stack(t) for t in zip(*odd_s)]
    return (y_prompt, y_sample,
            new_a_wkv_p, new_a_shift_p, new_b_ssm_p, new_b_conv_p,
            new_c_re_p, new_c_im_p, new_d_k_p, new_d_v_p, new_d_kidx_p,
            new_a_wkv_s, new_a_shift_s, new_b_ssm_s, new_b_conv_s,
            new_c_re_s, new_c_im_s, new_d_k_s, new_d_v_s, new_d_kidx_s)
```

```python
import functools
import math

import jax
import jax.numpy as jnp
import numpy as np
from jax import lax
from jax.experimental import pallas as pl
from jax.experimental.pallas import tpu as pltpu

D_MODEL = 1024
DEPTH = 4
PAST_LEN = 2048
PAGE_SIZE = 128
NORM_EPS = 1e-6

D_A = D_MODEL
HD_A = 64
H_A = D_A // HD_A
W_LORA = 64
A_LORA = 64
A_SHIFT = 3 * D_A + W_LORA + A_LORA
RWKV_LN_EPS = 64e-5
D_B = D_MODEL
P_B = 64
H_B = D_B // P_B
N_B = 128
G_B = 4
HPG_B = H_B // G_B
CONV_W = 4
CONV_DIM = D_B + 2 * G_B * N_B
SSD_CHUNK = 128
D_C = D_MODEL // 2
CH_C = 16
G_C = D_C // CH_C
P_C = 64
H_D = 8
DH_D = 128
KVH_D = 2
QPK_D = H_D // KVH_D
D_D = H_D * DH_D
HI_D = 8
DI_D = 64
IDX_SCALE = (DI_D ** -0.5) * (HI_D ** -0.5)
TOPK_MAX = 256
QBLK = 128
ROPE_THETA = 500000.0
ROPE_FRAC = 4

IN_E = A_SHIFT + D_A + D_B + CONV_DIM + H_B
OUT_E = D_A + D_B
IN_O = 2 * D_C + D_D + 2 * KVH_D * DH_D + HI_D * DI_D + DI_D + HI_D + D_D
OUT_O = D_C + D_D

F32 = jnp.float32
BF16 = jnp.bfloat16

LANES = 128
SUBLANES = 8

TILE_M = 512
TILE_N = 512
PROJ_TILE_M = 1024


def _round_up(n, m):
    return (n + m - 1) // m * m


def _norm_matmul_kernel(x_ref, g_ref, w_ref, o_ref, h_ref):
    @pl.when(pl.program_id(1) == 0)
    def _():
        x = x_ref[...]
        ms = jnp.mean(x * x, axis=-1, keepdims=True)
        h_ref[...] = (x * lax.rsqrt(ms + NORM_EPS) * g_ref[...]).astype(BF16)

    o_ref[...] = jnp.dot(h_ref[...], w_ref[...], preferred_element_type=F32)


def norm_matmul(x2d, g, w_bf16, tn=TILE_N):
    m, d = x2d.shape
    n = w_bf16.shape[1]
    tm = min(PROJ_TILE_M, m)
    assert m % tm == 0 and n % tn == 0
    return pl.pallas_call(
        _norm_matmul_kernel,
        grid=(m // tm, n // tn),
        in_specs=[
            pl.BlockSpec((tm, d), lambda i, j: (i, 0)),
            pl.BlockSpec((1, d), lambda i, j: (0, 0)),
            pl.BlockSpec((d, tn), lambda i, j: (0, j)),
        ],
        out_specs=pl.BlockSpec((tm, tn), lambda i, j: (i, j)),
        out_shape=jax.ShapeDtypeStruct((m, n), F32),
        scratch_shapes=[pltpu.VMEM((tm, d), BF16)],
        compiler_params=pltpu.CompilerParams(dimension_semantics=("parallel", "arbitrary")),
        name="norm_matmul",
    )(x2d, g.reshape(1, d), w_bf16)


def _matmul_res_kernel(a1_ref, a2_ref, w1_ref, w2_ref, r_ref, o_ref):
    o_ref[...] = (r_ref[...] + jnp.dot(a1_ref[...].astype(BF16), w1_ref[...], preferred_element_type=F32)
                  + jnp.dot(a2_ref[...].astype(BF16), w2_ref[...], preferred_element_type=F32))


def matmul_residual(a1, a2, w_bf16, res2d):
    m, k1 = a1.shape
    k2 = a2.shape[1]
    n = w_bf16.shape[1]
    tm = min(TILE_M, m)
    assert m % tm == 0 and w_bf16.shape[0] == k1 + k2
    return pl.pallas_call(
        _matmul_res_kernel,
        grid=(m // tm,),
        in_specs=[
            pl.BlockSpec((tm, k1), lambda i: (i, 0)),
            pl.BlockSpec((tm, k2), lambda i: (i, 0)),
            pl.BlockSpec((k1, n), lambda i: (0, 0)),
            pl.BlockSpec((k2, n), lambda i: (0, 0)),
            pl.BlockSpec((tm, n), lambda i: (i, 0)),
        ],
        out_specs=pl.BlockSpec((tm, n), lambda i: (i, 0)),
        out_shape=jax.ShapeDtypeStruct((m, n), F32),
        compiler_params=pltpu.CompilerParams(dimension_semantics=("parallel",)),
        name="matmul_residual",
    )(a1, a2, w_bf16[:k1], w_bf16[k1:], res2d)


def _rmsnorm_kernel(x_ref, g_ref, o_ref):
    x = x_ref[...]
    ms = jnp.mean(x * x, axis=-1, keepdims=True)
    o_ref[...] = x * lax.rsqrt(ms + NORM_EPS) * g_ref[...]


def rmsnorm_rows(x2d, g):
    m, d = x2d.shape
    tm = min(TILE_M, m)
    return pl.pallas_call(
        _rmsnorm_kernel,
        grid=(m // tm,),
        in_specs=[pl.BlockSpec((tm, d), lambda i: (i, 0)), pl.BlockSpec((1, d), lambda i: (0, 0))],
        out_specs=pl.BlockSpec((tm, d), lambda i: (i, 0)),
        out_shape=jax.ShapeDtypeStruct((m, d), F32),
        compiler_params=pltpu.CompilerParams(dimension_semantics=("parallel",)),
        name="final_rmsnorm",
    )(x2d, g.reshape(1, d))


def _mm(a, b):
    return jnp.dot(a.astype(BF16), b.astype(BF16), preferred_element_type=F32)


def _mm_nt(a, b):
    return lax.dot_general(a.astype(BF16), b.astype(BF16), (((1,), (1,)), ((), ())), preferred_element_type=F32)


def _mm_tn(a, b):
    return lax.dot_general(a.astype(BF16), b.astype(BF16), (((0,), (0,)), ((), ())), preferred_element_type=F32)


def _hi_lo(x):
    hi = x.astype(BF16)
    lo = (x - hi.astype(F32)).astype(BF16)
    return hi, lo


def _mm_exact_lhs(a_bf16, x):
    hi, lo = _hi_lo(x)
    return (jnp.dot(a_bf16, hi, preferred_element_type=F32) + jnp.dot(a_bf16, lo, preferred_element_type=F32))


def _mm_exact_rhs(x, b_bf16):
    hi, lo = _hi_lo(x)
    return (jnp.dot(hi, b_bf16, preferred_element_type=F32) + jnp.dot(lo, b_bf16, preferred_element_type=F32))


def _softplus(x):
    return jnp.maximum(x, 0.0) + jnp.log1p(jnp.exp(-jnp.abs(x)))


def _sigmoid(x):
    return 1.0 / (1.0 + jnp.exp(-x))


def _unit_lower_inverses(a_list, row, col, n):
    eye = jnp.where(row == col, 1.0, 0.0).astype(F32)
    ts = [eye for _ in a_list]
    m = 1
    while m < n:
        in_pair = (row // (2 * m)) == (col // (2 * m))
        lvl = in_pair & ((row % (2 * m)) >= m) & ((col % (2 * m)) < m)
        ls = [jnp.where(lvl, a, 0.0) for a in a_list]
        if m == 1:
            ts = [t - l for t, l in zip(ts, ls)]
        else:
            tl = [_mm(t, l) for t, l in zip(ts, ls)]
            ts = [t - _mm(x, t) for t, x in zip(ts, tl)]
        m *= 2
    return ts


EV_GA, EV_Z, EV_XBC, EV_R, EV_K, EV_V, EV_WA, EV_DT = 0, 1024, 2048, 4096, 5120, 6144, 7168, 7296
EV_N = 7680
RWKV_CHUNK = 64


RWKV_GROUP = 4
RWKV_BATCH = 8


def _block_rows(x, n, mask):
    return jnp.where(mask, jnp.concatenate([x] * n, axis=0), 0.0)


def _wkv_groups(alpha, beta, kappa, rho, v, kappa_e, beta_e, wc, st_ref, C, bb):
    gh, hd = RWKV_GROUP, HD_A
    gw, tw = gh * hd, gh * C
    iota = lambda shape, d: lax.broadcasted_iota(jnp.int32, shape, d)
    row, cin = iota((C, tw), 0), iota((C, tw), 1) % C
    strict, incl = cin < row, cin <= row
    m_tt = iota((tw, tw), 0) // C == iota((tw, tw), 1) // C
    m_tk = iota((tw, gw), 0) // C == iota((tw, gw), 1) // hd
    m_kk = iota((gw, gw), 0) // hd == iota((gw, gw), 1) // hd
    probs = [(b, slice(b * C, (b + 1) * C), slice(g * gw, (g + 1) * gw)) for b in range(bb) for g in range(H_A // gh)]
    s0s = [st_ref[b, :, sl] for b, _, sl in probs]
    bd_s = [_block_rows(s0, gh, m_kk) for s0 in s0s]
    bd_v = [_block_rows(v[rs, sl], gh, m_tk) for _, rs, sl in probs]
    grams = [_mm_nt(jnp.concatenate([alpha[rs, sl], rho[rs, sl]], axis=0),
                    jnp.concatenate([_block_rows(beta[rs, sl], gh, m_tk), _block_rows(kappa[rs, sl], gh, m_tk)],
                                    axis=0)) for _, rs, sl in probs]
    a_bs = [jnp.where(strict, g[:C, :tw], 0.0) for g in grams]
    ts = [jnp.where(cin == row, 1.0, 0.0).astype(F32) for _ in probs]
    m = 1
    while m < C:
        lvl = ((row // (2 * m)) == (cin // (2 * m))) & ((row % (2 * m)) >= m) & ((cin % (2 * m)) < m)
        ls = [jnp.where(lvl, a, 0.0) for a in a_bs]
        if m == 1:
            ts = [t - l for t, l in zip(ts, ls)]
        else:
            tl = [_mm(t, _block_rows(l, gh, m_tt)) for t, l in zip(ts, ls)]
            ts = [t - _mm(x, _block_rows(t, gh, m_tt)) for t, x in zip(ts, tl)]
        m *= 2
    rhss = [_mm_nt(alpha[rs, sl], bs) + _mm(jnp.where(strict, g[:C, tw:], 0.0), bv)
            for (_, rs, sl), bs, bv, g in zip(probs, bd_s, bd_v, grams)]
    us = [_mm(t, _block_rows(x, gh, m_tk)) for t, x in zip(ts, rhss)]
    outs = [_mm_nt(rho[rs, sl], bs) + _mm(jnp.where(incl, g[C:, tw:], 0.0), bv)
            - _mm(jnp.where(incl, g[C:, :tw], 0.0), _block_rows(u, gh, m_tk))
            for (_, rs, sl), bs, bv, g, u in zip(probs, bd_s, bd_v, grams, us)]
    for (b, rs, sl), s0, u in zip(probs, s0s, us):
        x = jnp.concatenate([v[rs, sl], -u], axis=0)
        y = jnp.concatenate([kappa_e[rs, sl], beta_e[rs, sl]], axis=0)
        cross = jnp.where(m_kk, _mm_tn(x, y), 0.0)
        st_ref[b, :, sl] = s0 * wc[rs, sl][:1] + sum(cross[h * hd:(h + 1) * hd] for h in range(gh))
    n_g = H_A // gh
    return jnp.concatenate([jnp.concatenate(outs[b * n_g:(b + 1) * n_g], axis=1) for b in range(bb)], axis=0)


def _wkv_heads(alpha, beta, kappa, rho, v, kappa_e, beta_e, wc, st_ref, C, bb):
    row = lax.broadcasted_iota(jnp.int32, (C, C), 0)
    col = lax.broadcasted_iota(jnp.int32, (C, C), 1)
    incl, strict = col <= row, col < row
    probs = [(b, slice(b * C, (b + 1) * C), slice(h * HD_A, (h + 1) * HD_A)) for b in range(bb) for h in range(H_A)]
    s0s = [st_ref[b, :, sl] for b, _, sl in probs]
    bks = [jnp.concatenate([beta[rs, sl], kappa[rs, sl]], axis=0) for _, rs, sl in probs]
    g_as = [_mm_nt(alpha[rs, sl], bk) for (_, rs, sl), bk in zip(probs, bks)]
    g_rs = [_mm_nt(rho[rs, sl], bk) for (_, rs, sl), bk in zip(probs, bks)]
    t_invs = _unit_lower_inverses([jnp.where(strict, g[:, :C], 0.0) for g in g_as], row, col, C)
    rhss = [_mm_nt(alpha[rs, sl], s0) + _mm(jnp.where(strict, g[:, C:], 0.0), v[rs, sl])
            for (_, rs, sl), s0, g in zip(probs, s0s, g_as)]
    us = [_mm(t, x) for t, x in zip(t_invs, rhss)]
    outs = [_mm_nt(rho[rs, sl], s0) + _mm(jnp.where(incl, g[:, C:], 0.0), v[rs, sl])
            - _mm(jnp.where(incl, g[:, :C], 0.0), u) for (_, rs, sl), s0, g, u in zip(probs, s0s, g_rs, us)]
    for (b, rs, sl), s0, u in zip(probs, s0s, us):
        x = jnp.concatenate([v[rs, sl], -u], axis=0)
        y = jnp.concatenate([kappa_e[rs, sl], beta_e[rs, sl]], axis=0)
        st_ref[b, :, sl] = s0 * wc[rs, sl][:1] + _mm_tn(x, y)
    return jnp.concatenate([jnp.concatenate(outs[b * H_A:(b + 1) * H_A], axis=1) for b in range(bb)], axis=0)


def _rwkv_kernel(r_ref, k_ref, v_ref, wa_ref, g_ref, sh0_ref, s0_ref, mu_ref, pv_ref, w2_ref, a2_ref, bones_ref,
                 o_ref, snew_ref, shnew_ref, st_ref, carry_ref, *, chunk, n_valid, bb):
    C = chunk
    R = bb * C
    c = pl.program_id(1)

    @pl.when(c == 0)
    def _():
        carry_ref[...] = sh0_ref[...]
        for b in range(bb):
            for h in range(H_A):
                st_ref[b, :, h * HD_A:(h + 1) * HD_A] = s0_ref[b, h]

    mu = mu_ref[...]
    pv = pv_ref[...]
    w0, a0, k_k, k_a, r_k, ln_g, ln_b = (pv[i:i + 1] for i in range(7))
    row8 = lax.broadcasted_iota(jnp.int32, (8, 1), 0)
    t_in_chunk = lax.broadcasted_iota(jnp.int32, (R, 1), 0) % C

    def tok_shift(x, lo, hi):
        rolled = pltpu.roll(x, 1, 0)
        pieces = []
        for b in range(bb):
            pieces.append(jnp.where(row8 == 0, carry_ref[b, :, lo:hi], rolled[b * C:b * C + 8]))
            if C > 8:
                pieces.append(rolled[b * C + 8:(b + 1) * C])
        return x + (jnp.concatenate(pieces, axis=0) - x) * mu[:, lo:hi]

    pr, pk, pvv, pwa = (ref[...].reshape(R, ref.shape[-1]) for ref in (r_ref, k_ref, v_ref, wa_ref))
    r = tok_shift(pr, 0, D_A)
    k = tok_shift(pk, D_A, 2 * D_A)
    v = tok_shift(pvv, 2 * D_A, 3 * D_A)
    wa = tok_shift(pwa, 3 * D_A, A_SHIFT)
    for b in range(bb):
        last = b * C + n_valid - 1
        new_carry = jnp.concatenate([x[last:last + 1] for x in (pr, pk, pvv, pwa)], axis=1)
        carry_ref[b] = new_carry
        shnew_ref[b] = new_carry

    wl, al = wa[:, :W_LORA], wa[:, W_LORA:]
    w_log = -_softplus(-(w0 + _mm(jnp.tanh(wl), w2_ref[...]))) - 0.5
    logw = -jnp.exp(w_log)
    a = _sigmoid(a0 + _mm(al, a2_ref[...]))
    bones = bones_ref[...]
    kk = k * k_k
    kk = kk / jnp.maximum(jnp.sqrt(_mm_exact_rhs(kk * kk, bones)), 1e-12)
    kp = k * (1.0 + (a - 1.0) * k_a)
    bonus = _mm_exact_rhs(r * kp * r_k, bones) * v
    if n_valid < C:
        ok = t_in_chunk < n_valid
        logw = jnp.where(ok, logw, 0.0)
        kk = jnp.where(ok, kk, 0.0)
        kp = jnp.where(ok, kp, 0.0)

    row = lax.broadcasted_iota(jnp.int32, (R, R), 0)
    col = lax.broadcasted_iota(jnp.int32, (R, R), 1)
    same = (row // C) == (col // C)
    cum = _mm_exact_lhs((same & (col <= row)).astype(BF16), logw)
    tot = _mm_exact_lhs(same.astype(BF16), logw)
    eneg = jnp.exp(-cum)
    alpha = kk * jnp.exp(cum - logw)
    ka = kk * a
    beta = ka * eneg
    kappa = kp * eneg
    rho = r * jnp.exp(cum)
    dec_end = jnp.exp(tot - cum)
    kappa_e = kp * dec_end
    beta_e = ka * dec_end
    wc = jnp.exp(tot)

    solve = _wkv_groups if RWKV_GROUP * C % LANES == 0 else _wkv_heads
    out = solve(alpha, beta, kappa, rho, v, kappa_e, beta_e, wc, st_ref, C, bb)

    mean = _mm_exact_rhs(out, bones) * (1.0 / HD_A)
    d = out - mean
    var = _mm_exact_rhs(d * d, bones) * (1.0 / HD_A)
    y = d * lax.rsqrt(var + RWKV_LN_EPS) * ln_g + ln_b
    gate = g_ref[...].reshape(R, D_A)
    o_ref[...] = ((y + bonus) * (gate * _sigmoid(gate))).reshape(bb, C, D_A)
    for b in range(bb):
        for h in range(H_A):
            snew_ref[b, h] = st_ref[b, :, h * HD_A:(h + 1) * HD_A]


def rwkv7_pallas(proj3d, shift0, wkv0, mu, w0, w2, a0, a2, k_k, k_a, r_k, ln_g, ln_b, n_valid_tokens):
    b, l, _ = proj3d.shape
    chunk = min(RWKV_CHUNK, l)
    n_chunks = l // chunk
    bb = math.gcd(b, RWKV_BATCH)
    assert l % chunk == 0 and chunk % 8 == 0
    n_valid = n_valid_tokens - (n_chunks - 1) * chunk
    assert 0 < n_valid <= chunk and (n_valid == chunk or n_chunks == 1)
    pvec = jnp.stack([w0, a0, k_k, k_a, r_k.reshape(D_A), ln_g, ln_b, jnp.zeros_like(w0)])
    hid = jnp.arange(D_A) // HD_A
    bones = (hid[:, None] == hid[None, :]).astype(BF16)
    blk = lambda w, j: pl.BlockSpec((bb, chunk, w), lambda bi, ci: (bi, ci, j))
    const = lambda shape: pl.BlockSpec(shape, lambda bi, ci: (0,) * len(shape))
    out, wkv_new, shift_new = pl.pallas_call(
        functools.partial(_rwkv_kernel, chunk=chunk, n_valid=n_valid, bb=bb),
        grid=(b // bb, n_chunks),
        in_specs=[
            blk(D_A, EV_R // D_A), blk(D_A, EV_K // D_A), blk(D_A, EV_V // D_A), blk(LANES, EV_WA // LANES),
            blk(D_A, EV_GA // D_A),
            pl.BlockSpec((bb, 1, A_SHIFT), lambda bi, ci: (bi, 0, 0)),
            pl.BlockSpec((bb, H_A, HD_A, HD_A), lambda bi, ci: (bi, 0, 0, 0)),
            const((1, A_SHIFT)), const((8, D_A)), const((W_LORA, D_A)), const((A_LORA, D_A)), const((D_A, D_A)),
        ],
        out_specs=[
            pl.BlockSpec((bb, chunk, D_A), lambda bi, ci: (bi, ci, 0)),
            pl.BlockSpec((bb, H_A, HD_A, HD_A), lambda bi, ci: (bi, 0, 0, 0)),
            pl.BlockSpec((bb, 1, A_SHIFT), lambda bi, ci: (bi, 0, 0)),
        ],
        out_shape=[
            jax.ShapeDtypeStruct((b, l, D_A), F32),
            jax.ShapeDtypeStruct((b, H_A, HD_A, HD_A), F32),
            jax.ShapeDtypeStruct((b, 1, A_SHIFT), F32),
        ],
        scratch_shapes=[pltpu.VMEM((bb, HD_A, D_A), F32), pltpu.VMEM((bb, 1, A_SHIFT), F32)],
        compiler_params=pltpu.CompilerParams(dimension_semantics=("parallel", "arbitrary")),
        name="rwkv7_chunked",
    )(proj3d, proj3d, proj3d, proj3d, proj3d, shift0.reshape(b, 1, A_SHIFT), wkv0,
      mu.reshape(1, A_SHIFT), pvec, w2.astype(BF16), a2.astype(BF16), bones)
    return out, wkv_new, shift_new.reshape(b, A_SHIFT)


OD_U, OD_GC, OD_Q, OD_GD, OD_K, OD_V, OD_QI, OD_KIWI = 0, 512, 1024, 2048, 3072, 3328, 3584, 4096
OD_N = 4608
KV_W = KVH_D * DH_D
QI_W = HI_D * DI_D
INT_MIN = -2 ** 31


def _rope_tables(pos, head_dim):
    rd = head_dim // ROPE_FRAC
    half = rd // 2
    inv_freq = ROPE_THETA ** (-jnp.arange(half, dtype=F32) / half)
    ang = pos.astype(F32)[:, None] * inv_freq[None, :]
    cos, sin = jnp.cos(ang), jnp.sin(ang)
    t = pos.shape[0]
    ones = jnp.ones((t, head_dim - rd), F32)
    cos_h = jnp.concatenate([cos, cos, ones], axis=1)
    sin_h = jnp.concatenate([-sin, sin, 0.0 * ones], axis=1)
    reps = LANES // head_dim
    return jnp.tile(cos_h, (1, reps)), jnp.tile(sin_h, (1, reps))


def _rotate(x, cos_t, sin_t, head_dim):
    w = x.shape[1]
    half = head_dim // ROPE_FRAC // 2
    lane = lax.broadcasted_iota(jnp.int32, x.shape, 1) % head_dim
    if w >= LANES:
        cos_f = jnp.tile(cos_t, (1, w // LANES))
        sin_f = jnp.tile(sin_t, (1, w // LANES))
        partner = jnp.where(lane < half, pltpu.roll(x, w - half, 1), pltpu.roll(x, half, 1))
    else:
        cos_f, sin_f = cos_t[:, :w], sin_t[:, :w]
        partner = jnp.where(lane < half, jnp.concatenate([x[:, half:], x[:, :half]], axis=1),
                            jnp.concatenate([x[:, w - half:], x[:, :w - half]], axis=1))
    return x * cos_f + partner * sin_f


def _rope_kernel(q_ref, k_ref, v_ref, qi_ref, kiwi_ref, c128_ref, s128_ref, c64_ref, s64_ref,
                 qo_ref, qio_ref, ko_ref, vo_ref, kio_ref, kb_ref, vb_ref, kib_ref, *, v_transposed):
    c128, s128, c64, s64 = c128_ref[...], s128_ref[...], c64_ref[...], s64_ref[...]
    qo_ref[0] = _rotate(q_ref[0], c128, s128, DH_D).astype(BF16)
    qio_ref[0] = _rotate(qi_ref[0], c64, s64, DI_D).astype(BF16)
    k_rot = _rotate(k_ref[0], c128, s128, DH_D)
    ko_ref[0] = k_rot
    kb_ref[0] = k_rot.astype(BF16)
    v = v_ref[0]
    vo_ref[0] = v
    vb_ref[0] = (v.T if v_transposed else v).astype(BF16)
    ki_rot = _rotate(kiwi_ref[0][:, :DI_D], c64, s64, DI_D)
    kio_ref[0] = ki_rot
    kib_ref[0] = ki_rot.astype(BF16)


def rope_pallas(proj3d, pos, v_transposed=False):
    b, l, _ = proj3d.shape
    tr = min(512, l)
    assert l % tr == 0
    c128, s128 = _rope_tables(pos, DH_D)
    c64, s64 = _rope_tables(pos, DI_D)
    blk = lambda w, j: pl.BlockSpec((1, tr, w), lambda bi, ti: (bi, ti, j))
    tab = pl.BlockSpec((tr, LANES), lambda bi, ti: (ti, 0))
    oblk = lambda w: pl.BlockSpec((1, tr, w), lambda bi, ti: (bi, ti, 0))
    shp = lambda w, dt: jax.ShapeDtypeStruct((b, l, w), dt)
    vb_spec, vb_shape = oblk(KV_W), shp(KV_W, BF16)
    if v_transposed:
        vb_spec = pl.BlockSpec((1, KV_W, tr), lambda bi, ti: (bi, 0, ti))
        vb_shape = jax.ShapeDtypeStruct((b, KV_W, l), BF16)
    return pl.pallas_call(
        functools.partial(_rope_kernel, v_transposed=v_transposed),
        grid=(b, l // tr),
        in_specs=[blk(D_D, OD_Q // D_D), blk(KV_W, OD_K // KV_W), blk(KV_W, OD_V // KV_W), blk(QI_W, OD_QI // QI_W),
                  blk(LANES, OD_KIWI // LANES), tab, tab, tab, tab],
        out_specs=[oblk(D_D), oblk(QI_W), oblk(KV_W), oblk(KV_W), oblk(DI_D), oblk(KV_W), vb_spec, oblk(DI_D)],
        out_shape=[shp(D_D, BF16), shp(QI_W, BF16), shp(KV_W, F32), shp(KV_W, F32), shp(DI_D, F32),
                   shp(KV_W, BF16), vb_shape, shp(DI_D, BF16)],
        compiler_params=pltpu.CompilerParams(dimension_semantics=("parallel", "parallel")),
        name="dsa_rope",
    )(proj3d, proj3d, proj3d, proj3d, proj3d, c128, s128, c64, s64)


DSA_TQ = 128
DSA_TK = 512
NEG_BIG = -1e30
DSA_ONES = 16


NEG_INF = float("-inf")
REFINE_STEPS = 2


def _pattern_to_float(u):
    key = u ^ INT_MIN
    return pltpu.bitcast(key ^ ((key >> 31) & 0x7FFFFFFF), F32)


def _threshold_from_pattern(u):
    return jnp.where(u == 0, NEG_INF, _pattern_to_float(u))


def _dsa_prompt_kernel(q_ref, qi_ref, kiwi_ref, gd_ref, k_ref, vt_ref, ki_ref, o_ref, score_ref, *, n_sel):
    tq, tk = DSA_TQ, DSA_TK
    t0 = pl.program_id(1) * tq
    n_kt = (t0 + tq + tk - 1) // tk
    qi = qi_ref[0]
    wi_t = kiwi_ref[0].T[DI_D:DI_D + HI_D, :]
    pos_q = t0 + lax.broadcasted_iota(jnp.int32, (1, tq), 1)
    row_k = lax.broadcasted_iota(jnp.int32, (tk, tq), 0)

    def key_slice(kt):
        return pl.ds(pl.multiple_of(kt * tk, tk), tk)

    def fold_rows(x):
        x = x.reshape(tk // 8, 8, tq)
        while x.shape[0] > 1:
            half = x.shape[0] // 2
            x = x[:half] + x[half:]
        return x[0]

    qi_rows = jnp.concatenate([qi[:, h * DI_D:(h + 1) * DI_D] for h in range(HI_D)], axis=0)

    def score_body(kt, carry):
        ks = key_slice(kt)
        d = lax.dot_general(ki_ref[0, ks, :], qi_rows, (((1,), (1,)), ((), ())), preferred_element_type=F32)
        acc = sum(wi_t[h:h + 1, :] * jnp.maximum(d[:, h * tq:(h + 1) * tq], 0.0) for h in range(HI_D))
        score_ref[ks, :] = jnp.where(kt * tk + row_k <= pos_q, acc * IDX_SCALE + 0.0, NEG_INF)
        return carry

    lax.fori_loop(0, n_kt, score_body, 0)

    def count(pred_fn):
        def body(kt, c):
            return c + fold_rows(jnp.where(pred_fn(score_ref[key_slice(kt), :]), 1.0, 0.0))
        c = lax.fori_loop(0, n_kt, body, jnp.zeros((8, tq), F32))
        return jnp.sum(c, axis=0, keepdims=True)

    def bit_body(i, ans):
        cand = ans | lax.shift_left(jnp.int32(1), 31 - i)
        cand_f = _pattern_to_float(cand)
        cnt = count(lambda sc: sc >= cand_f)
        return jnp.where(cnt >= n_sel, cand, ans)

    thr = _threshold_from_pattern(lax.fori_loop(0, 32, bit_body, jnp.zeros((1, tq), jnp.int32)))

    def refine(_, thr):
        def body(kt, m):
            sc = score_ref[key_slice(kt), :]
            x = jnp.where(sc > thr, sc, -NEG_INF).reshape(tk // SUBLANES, SUBLANES, tq)
            while x.shape[0] > 1:
                half = x.shape[0] // 2
                x = jnp.minimum(x[:half], x[half:])
            return jnp.minimum(m, x[0])

        nxt = jnp.min(lax.fori_loop(0, n_kt, body, jnp.full((SUBLANES, tq), -NEG_INF, F32)), axis=0, keepdims=True)
        return jnp.where(count(lambda sc: sc >= nxt) >= n_sel, nxt, thr)

    thr = lax.fori_loop(0, REFINE_STEPS, refine, thr)

    def rank_ties():
        need = n_sel - count(lambda sc: sc > thr)
        tri = (lax.broadcasted_iota(jnp.int32, (tk, tk), 1)
               <= lax.broadcasted_iota(jnp.int32, (tk, tk), 0)).astype(BF16)

        def sel_body(kt, tie_seen):
            ks = key_slice(kt)
            sc = score_ref[ks, :]
            tie = (sc == thr) & (sc > NEG_INF)
            tie_f = jnp.where(tie, 1.0, 0.0)
            rank = tie_seen + jnp.dot(tri, tie_f.astype(BF16), preferred_element_type=F32)
            score_ref[ks, :] = jnp.where((sc > thr) | (tie & (rank <= need)), 1.0, 0.0)
            return tie_seen + jnp.sum(tie_f, axis=0, keepdims=True)

        lax.fori_loop(0, n_kt, sel_body, jnp.zeros((1, tq), F32))

    def take_all():
        def sel_body(kt, carry):
            ks = key_slice(kt)
            sc = score_ref[ks, :]
            score_ref[ks, :] = jnp.where((sc >= thr) & (sc > NEG_INF), 1.0, 0.0)
            return carry

        lax.fori_loop(0, n_kt, sel_body, 0)

    lax.cond(jnp.max(count(lambda sc: sc >= thr)) > n_sel, rank_ties, take_all)

    c_exp = (DH_D ** -0.5) * math.log2(math.e)
    ones_rows = jnp.ones((DSA_ONES, tk), BF16)
    qs = [q_ref[0, :, h * DH_D:(h + 1) * DH_D] for h in range(H_D)]

    def att_body(kt, carry):
        m, accs = carry
        ks = key_slice(kt)
        sel = score_ref[ks, :] != 0.0
        m_rows, new_accs = [], []
        for j in range(KVH_D):
            hs = range(j * QPK_D, (j + 1) * QPK_D)
            kt_j = k_ref[0, ks, j * DH_D:(j + 1) * DH_D]
            vt_j = jnp.concatenate([vt_ref[0, j * DH_D:(j + 1) * DH_D, ks], ones_rows], axis=0)
            logits = [lax.dot_general(kt_j, qs[h], (((1,), (1,)), ((), ())), preferred_element_type=F32) for h in hs]
            logits = [jnp.where(sel, x, NEG_BIG) for x in logits]
            m_new = [jnp.maximum(m[h:h + 1, :], jnp.max(x, axis=0, keepdims=True)) for h, x in zip(hs, logits)]
            ps = [jnp.exp2((x - mn) * c_exp).astype(BF16) for x, mn in zip(logits, m_new)]
            scales = [jnp.exp2((m[h:h + 1, :] - mn) * c_exp) for h, mn in zip(hs, m_new)]
            pv = [jnp.dot(vt_j, p, preferred_element_type=F32) for p in ps]
            m_rows.extend(m_new)
            new_accs.extend(accs[h] * sc + x for h, sc, x in zip(hs, scales, pv))
        return jnp.concatenate(m_rows, axis=0), tuple(new_accs)

    init = (jnp.full((H_D, tq), NEG_BIG, F32), tuple(jnp.zeros((DH_D + DSA_ONES, tq), F32) for _ in range(H_D)))
    _, accs = lax.fori_loop(0, n_kt, att_body, init)
    out = jnp.concatenate([(a[:DH_D] / a[DH_D:DH_D + 1]).T for a in accs], axis=1)
    gd = gd_ref[0]
    o_ref[0] = out * (gd * _sigmoid(gd))


def dsa_prompt_pallas(proj3d, q_b, qi_b, k_b, vt_b, ki_b):
    b, l, _ = proj3d.shape
    n_sel = min(TOPK_MAX, l // 4)
    assert l % DSA_TQ == 0 and l % DSA_TK == 0
    qblk = lambda w, j: pl.BlockSpec((1, DSA_TQ, w), lambda bi, ti: (bi, ti, j))
    full = lambda w: pl.BlockSpec((1, l, w), lambda bi, ti: (bi, 0, 0))
    return pl.pallas_call(
        functools.partial(_dsa_prompt_kernel, n_sel=float(n_sel)),
        grid=(b, l // DSA_TQ),
        in_specs=[qblk(D_D, 0), qblk(QI_W, 0), qblk(LANES, OD_KIWI // LANES), qblk(D_D, OD_GD // D_D),
                  full(KV_W), pl.BlockSpec((1, KV_W, l), lambda bi, ti: (bi, 0, 0)), full(DI_D)],
        out_specs=pl.BlockSpec((1, DSA_TQ, D_D), lambda bi, ti: (bi, ti, 0)),
        out_shape=jax.ShapeDtypeStruct((b, l, D_D), F32),
        scratch_shapes=[pltpu.VMEM((l, DSA_TQ), F32)],
        compiler_params=pltpu.CompilerParams(dimension_semantics=("parallel", "arbitrary")),
        name="dsa_prompt",
    )(q_b, qi_b, proj3d, proj3d, k_b, vt_b, ki_b)


S5_N = G_C * P_C
S5_ROWS = 2 * S5_N // LANES
S5_TC = 256


def _s5_kernel(u_ref, g_ref, h0re_ref, h0im_ref, lam_t_ref, lam_r_ref, bd_ref, cd_ref, dsk_ref, gw_ref, gb_ref,
               o_ref, hre_ref, him_ref, scr_ref, h_ref, *, bb, tc, n_valid):
    c = pl.program_id(1)
    rows = bb * tc
    half = S5_ROWS // 2

    @pl.when(c == 0)
    def _():
        for i in range(bb):
            h_ref[i * S5_ROWS:i * S5_ROWS + half] = h0re_ref[i]
            h_ref[i * S5_ROWS + half:(i + 1) * S5_ROWS] = h0im_ref[i]

    def discretize(lam_ref):
        lr, li, dt = lam_ref[0], lam_ref[1], jnp.exp(lam_ref[2])
        mag = jnp.exp(lr * dt)
        br, bi = mag * jnp.cos(li * dt), mag * jnp.sin(li * dt)
        den = lr * lr + li * li
        fr = ((br - 1.0) * lr + bi * li) / den
        fi = (bi * lr - (br - 1.0) * li) / den
        return br, bi, fr, fi

    _, _, fr, fi = discretize(lam_r_ref)
    u = u_ref[...].reshape(rows, D_C)
    hu, hn = D_C // 2, S5_N // 2
    b_re, b_im = (jnp.concatenate([_mm(u[:, i * hu:(i + 1) * hu], bd_ref[i * hu:(i + 1) * hu, o + i * hn:o + (i + 1) * hn])
                                   for i in range(2)], axis=1) for o in (0, S5_N))
    bu = jnp.concatenate([fr * b_re - fi * b_im, fr * b_im + fi * b_re], axis=1)
    for r in range(S5_ROWS):
        scr_ref[pl.ds(r, rows, stride=S5_ROWS), :] = bu[:, r * LANES:(r + 1) * LANES]

    lbr, lbi, _, _ = discretize(lam_t_ref)
    for i in range(bb):
        def step(t, h):
            hr, hi = h
            idx = pl.multiple_of((i * tc + t) * S5_ROWS, S5_ROWS)
            blk = scr_ref[pl.ds(idx, S5_ROWS), :]
            nr = lbr * hr - lbi * hi + blk[:half]
            ni = lbr * hi + lbi * hr + blk[half:]
            scr_ref[pl.ds(idx, S5_ROWS), :] = jnp.concatenate([nr, ni], axis=0)
            return nr, ni

        h_in = (h_ref[i * S5_ROWS:i * S5_ROWS + half], h_ref[i * S5_ROWS + half:(i + 1) * S5_ROWS])
        hr, hi = lax.fori_loop(0, n_valid, step, h_in, unroll=4)
        h_ref[i * S5_ROWS:i * S5_ROWS + half] = hr
        h_ref[i * S5_ROWS + half:(i + 1) * S5_ROWS] = hi
        hre_ref[i] = hr
        him_ref[i] = hi

    h_tiles = [scr_ref[pl.ds(r, rows, stride=S5_ROWS), :] for r in range(S5_ROWS)]
    n_out = D_C // LANES
    per = half // n_out
    y_tiles = []
    for t in range(n_out):
        src = [p * half + t * per + i for p in range(2) for i in range(per)]
        y_tiles.append(_mm(jnp.concatenate([h_tiles[r] for r in src], axis=1),
                           jnp.concatenate([cd_ref[r * LANES:(r + 1) * LANES, t * LANES:(t + 1) * LANES] for r in src], axis=0)))
    y = dsk_ref[...] * u + jnp.concatenate(y_tiles, axis=1)
    y = 0.5 * y * (1.0 + jnp.tanh(math.sqrt(2.0 / math.pi) * (y + 0.044715 * (y * y * y))))
    y = y * _sigmoid(_mm(y, gw_ref[...]) + gb_ref[...])
    gate = g_ref[...].reshape(rows, D_C)
    o_ref[...] = (y * (gate * _sigmoid(gate))).reshape(bb, tc, D_C)


def s5_pallas(proj3d, n_valid_tokens, h0_re, h0_im, lam_re, lam_im, log_dt, b_re, b_im, c_re, c_im, d_skip,
              glu_w, glu_b):
    b, l, _ = proj3d.shape
    tc = min(S5_TC, l)
    n_chunks = l // tc
    bb = max(1, min(b, S5_TC // tc))
    assert l % tc == 0 and b % bb == 0 and tc % 8 == 0
    n_valid = n_valid_tokens - (n_chunks - 1) * tc
    assert 0 < n_valid <= tc and (n_valid == tc or n_chunks == 1)
    half = S5_ROWS // 2
    dt_full = jnp.repeat(log_dt, P_C)
    lam_t = jnp.stack([lam_re.reshape(half, LANES), lam_im.reshape(half, LANES), dt_full.reshape(half, LANES)])
    lam_r = jnp.stack([lam_re.reshape(1, S5_N), lam_im.reshape(1, S5_N), dt_full.reshape(1, S5_N)])
    eye = jnp.eye(G_C, dtype=F32)
    bd = jnp.concatenate([jnp.einsum('gpm,gh->gmhp', t, eye).reshape(D_C, S5_N) for t in (b_re, b_im)], axis=1)
    cd = jnp.concatenate([jnp.einsum('gmp,gh->gphm', t, eye).reshape(S5_N, D_C) for t in (c_re, -c_im)], axis=0)
    const = lambda shape: pl.BlockSpec(shape, lambda bi, ci: (0,) * len(shape))
    st_spec = pl.BlockSpec((bb, half, LANES), lambda bi, ci: (bi, 0, 0))
    out, h_re, h_im = pl.pallas_call(
        functools.partial(_s5_kernel, bb=bb, tc=tc, n_valid=n_valid),
        grid=(b // bb, n_chunks),
        in_specs=[
            pl.BlockSpec((bb, tc, D_C), lambda bi, ci: (bi, ci, OD_U // D_C)),
            pl.BlockSpec((bb, tc, D_C), lambda bi, ci: (bi, ci, OD_GC // D_C)),
            st_spec, st_spec,
            const((3, half, LANES)), const((3, 1, S5_N)), const((D_C, 2 * S5_N)), const((2 * S5_N, D_C)),
            const((1, D_C)), const((D_C, D_C)), const((1, D_C)),
        ],
        out_specs=[pl.BlockSpec((bb, tc, D_C), lambda bi, ci: (bi, ci, 0)), st_spec, st_spec],
        out_shape=[jax.ShapeDtypeStruct((b, l, D_C), F32), jax.ShapeDtypeStruct((b, half, LANES), F32),
                   jax.ShapeDtypeStruct((b, half, LANES), F32)],
        scratch_shapes=[pltpu.VMEM((bb * tc * S5_ROWS, LANES), F32), pltpu.VMEM((bb * S5_ROWS, LANES), F32)],
        compiler_params=pltpu.CompilerParams(dimension_semantics=("parallel", "arbitrary")),
        name="s5_scan",
    )(proj3d, proj3d, h0_re.reshape(b, half, LANES), h0_im.reshape(b, half, LANES), lam_t, lam_r,
      bd.astype(BF16), cd.astype(BF16), d_skip.reshape(1, D_C), glu_w.astype(BF16), glu_b.reshape(1, D_C))
    return out, h_re.reshape(b, G_C, P_C), h_im.reshape(b, G_C, P_C)


def _hi_mid_lo(x):
    hi = x.astype(BF16)
    r1 = x - hi.astype(F32)
    mid = r1.astype(BF16)
    lo = (r1 - mid.astype(F32)).astype(BF16)
    return hi, mid, lo


def _mm3_rhs01(x, b_bf16):
    return sum(jnp.dot(p, b_bf16, preferred_element_type=F32) for p in _hi_mid_lo(x))


def _ssd_kernel(z_ref, xbc_ref, dt_ref, conv0_ref, ssm0_ref, cw_ref, cb_ref, sm_ref, pv_ref, e16_ref, bones_ref,
                o_ref, ssm_ref, convn_ref, buf_ref, st_ref, *, cl, n_valid):
    c = pl.program_id(1)
    keep = CONV_W - 1

    @pl.when(c == 0)
    def _():
        buf_ref[8 - keep:8] = conv0_ref[0]
        st_ref[...] = ssm0_ref[0]

    buf_ref[8:8 + cl] = xbc_ref[0]
    cw = cw_ref[...]
    conv = cb_ref[...] + sum(buf_ref[8 - keep + i:8 - keep + i + cl] * cw[i:i + 1] for i in range(CONV_W))
    convn_ref[0] = buf_ref[8 + n_valid - keep:8 + n_valid]
    buf_ref[8 - keep:8] = buf_ref[8 + cl - keep:8 + cl]
    act = conv * _sigmoid(conv)
    xs, bm, cm = act[:, :D_B], act[:, D_B:D_B + G_B * N_B], act[:, D_B + G_B * N_B:]

    sm = sm_ref[...]
    dt = _softplus(dt_ref[0][:, :H_B] + sm[0:1, :H_B])
    row1 = lax.broadcasted_iota(jnp.int32, (cl, 1), 0)
    if n_valid < cl:
        dt = jnp.where(row1 < n_valid, dt, 0.0)
    a = dt * (-jnp.exp(sm[1:2, :H_B]))
    row = lax.broadcasted_iota(jnp.int32, (cl, cl), 0)
    col = lax.broadcasted_iota(jnp.int32, (cl, cl), 1)
    incl = col <= row
    a_parts = _hi_mid_lo(a)
    a_cum = sum(jnp.dot(incl.astype(BF16), p, preferred_element_type=F32) for p in a_parts)
    a_cum_t = sum(lax.dot_general(p, (row <= col).astype(BF16), (((0,), (0,)), ((), ())),
                                  preferred_element_type=F32) for p in a_parts)
    a_last = a_cum[cl - 1:cl]
    e16 = e16_ref[...]
    xdt = xs * _mm3_rhs01(dt, e16)
    ea_full = jnp.exp(_mm3_rhs01(a_cum, e16))
    xdec = xdt * jnp.exp(_mm3_rhs01(a_last - a_cum, e16))
    chunk_decay = jnp.exp(a_last)

    groups = range(G_B)
    gsl = [slice(g * N_B, (g + 1) * N_B) for g in groups]
    cbs = [_mm_nt(cm[:, s], bm[:, s]) for s in gsl]
    y_off = jnp.concatenate(
        [_mm_nt(cm[:, s], st_ref[g * HPG_B:(g + 1) * HPG_B].reshape(HPG_B * P_B, N_B)) for g, s in zip(groups, gsl)],
        axis=1)
    y_diag = []
    for h in range(H_B):
        hs = slice(h * P_B, (h + 1) * P_B)
        lmat = jnp.exp(jnp.where(incl, a_cum[:, h:h + 1] - a_cum_t[h:h + 1, :], NEG_BIG))
        y_diag.append(_mm(cbs[h // HPG_B] * lmat, xdt[:, hs]))
    for h in range(H_B):
        hs = slice(h * P_B, (h + 1) * P_B)
        st_ref[h] = st_ref[h] * chunk_decay[:, h:h + 1] + _mm_tn(xdec[:, hs], bm[:, gsl[h // HPG_B]])

    pv = pv_ref[...]
    y = jnp.concatenate(y_diag, axis=1) + y_off * ea_full + xs * pv[0:1]
    z = z_ref[0]
    y = y * (z * _sigmoid(z))
    ms = _mm_exact_rhs(y * y, bones_ref[...]) * (1.0 / (D_B // G_B))
    o_ref[0] = y * lax.rsqrt(ms + NORM_EPS) * pv[1:2]

    @pl.when(c == pl.num_programs(1) - 1)
    def _():
        ssm_ref[0] = st_ref[...]


def ssd_pallas(proj3d, n_valid_tokens, conv0, ssm0, conv_w, conv_b, dt_bias, a_log, d_skip, gnorm_g):
    b, l, _ = proj3d.shape
    cl = min(SSD_CHUNK, l)
    n_chunks = l // cl
    assert l % cl == 0 and cl % 8 == 0
    n_valid = n_valid_tokens - (n_chunks - 1) * cl
    assert 0 < n_valid <= cl and (n_valid == cl or n_chunks == 1)
    sm = jnp.zeros((SUBLANES, LANES), F32).at[0, :H_B].set(dt_bias).at[1, :H_B].set(a_log)
    pv = jnp.zeros((8, D_B), F32).at[0].set(jnp.repeat(d_skip, P_B)).at[1].set(gnorm_g)
    e16 = (jnp.arange(D_B)[None, :] // P_B == jnp.arange(H_B)[:, None]).astype(BF16)
    gid = jnp.arange(D_B) // (D_B // G_B)
    bones = (gid[:, None] == gid[None, :]).astype(BF16)
    const = lambda shape: pl.BlockSpec(shape, lambda bi, ci: (0,) * len(shape))
    return pl.pallas_call(
        functools.partial(_ssd_kernel, cl=cl, n_valid=n_valid),
        grid=(b, n_chunks),
        in_specs=[
            pl.BlockSpec((1, cl, D_B), lambda bi, ci: (bi, ci, EV_Z // D_B)),
            pl.BlockSpec((1, cl, CONV_DIM), lambda bi, ci: (bi, ci, EV_XBC // CONV_DIM)),
            pl.BlockSpec((1, cl, LANES), lambda bi, ci: (bi, ci, EV_DT // LANES)),
            pl.BlockSpec((1, CONV_W - 1, CONV_DIM), lambda bi, ci: (bi, 0, 0)),
            pl.BlockSpec((1, H_B, P_B, N_B), lambda bi, ci: (bi, 0, 0, 0)),
            const((CONV_W, CONV_DIM)), const((1, CONV_DIM)), const((SUBLANES, LANES)), const((8, D_B)),
            const((H_B, D_B)), const((D_B, D_B)),
        ],
        out_specs=[
            pl.BlockSpec((1, cl, D_B), lambda bi, ci: (bi, ci, 0)),
            pl.BlockSpec((1, H_B, P_B, N_B), lambda bi, ci: (bi, 0, 0, 0)),
            pl.BlockSpec((1, CONV_W - 1, CONV_DIM), lambda bi, ci: (bi, 0, 0)),
        ],
        out_shape=[jax.ShapeDtypeStruct((b, l, D_B), F32), jax.ShapeDtypeStruct((b, H_B, P_B, N_B), F32),
                   jax.ShapeDtypeStruct((b, CONV_W - 1, CONV_DIM), F32)],
        scratch_shapes=[pltpu.VMEM((8 + cl, CONV_DIM), F32), pltpu.VMEM((H_B, P_B, N_B), F32)],
        compiler_params=pltpu.CompilerParams(dimension_semantics=("parallel", "arbitrary")),
        name="ssd_chunked",
    )(proj3d, proj3d, proj3d, conv0, ssm0, conv_w, conv_b.reshape(1, CONV_DIM), sm, pv, e16, bones)


RADIX_BITS = 4
DSA_SAMPLE_BATCH = 2


def _dsa_sample_kernel(pt_ref, q_ref, qi_ref, kiwi_ref, gd_ref, kn_ref, vn_ref, kin_ref, *rest, n_pages, n_sel, tq, bb):
    per = bb * n_pages
    kp_refs = [rest[i * n_pages:(i + 1) * n_pages] for i in range(bb)]
    vp_refs = [rest[per + i * n_pages:per + (i + 1) * n_pages] for i in range(bb)]
    kip_refs = [rest[2 * per + i * n_pages:2 * per + (i + 1) * n_pages] for i in range(bb)]
    o_ref = rest[3 * per]
    ps = PAGE_SIZE
    n_tiles = n_pages + 1
    bs = range(bb)
    pad_rows = lambda x: jnp.concatenate([x, jnp.zeros((ps - tq, x.shape[1]), x.dtype)], axis=0)

    qs_i = [jnp.concatenate([qi_ref[i][:, h * DI_D:(h + 1) * DI_D] for h in range(HI_D)], axis=0) for i in bs]
    wcol = [jnp.concatenate([kiwi_ref[i][:, DI_D + h:DI_D + h + 1] for h in range(HI_D)], axis=0) for i in bs]
    dots = [[_mm(qs_i[i], r[0]) for r in kip_refs[i]] + [_mm_nt(qs_i[i], pad_rows(kin_ref[i]))] for i in bs]
    n_keys = n_tiles * ps
    qrow = lax.broadcasted_iota(jnp.int32, (tq, n_keys), 0)
    kcol = lax.broadcasted_iota(jnp.int32, (tq, n_keys), 1)
    admissible = kcol - n_pages * ps <= qrow
    keys = []
    for i in bs:
        scores = []
        for d in dots[i]:
            d = wcol[i] * jnp.maximum(d, 0.0)
            scores.append(sum(d[h * tq:(h + 1) * tq] for h in range(HI_D)))
        keys.append(jnp.where(admissible, jnp.concatenate(scores, axis=1) * IDX_SCALE + 0.0, NEG_INF))

    def count(pred):
        return jnp.sum(jnp.where(pred, 1.0, 0.0), axis=1, keepdims=True)

    ans = [jnp.zeros((tq, 1), jnp.int32) for _ in bs]
    for shift in range(32 - RADIX_BITS, -1, -RADIX_BITS):
        digits = [jnp.zeros((tq, 1), jnp.int32) for _ in bs]
        for d in range(1, 2 ** RADIX_BITS):
            step = d << shift if d << shift < 2 ** 31 else (d << shift) - 2 ** 32
            digits = [dg + jnp.where(count(k >= _pattern_to_float(a | step)) >= n_sel, 1, 0)
                      for dg, k, a in zip(digits, keys, ans)]
        ans = [a | lax.shift_left(dg, shift) for a, dg in zip(ans, digits)]
    thrs = [_threshold_from_pattern(a) for a in ans]
    for _ in range(REFINE_STEPS):
        nxts = [jnp.min(jnp.where(k > t, k, -NEG_INF), axis=1, keepdims=True) for k, t in zip(keys, thrs)]
        thrs = [jnp.where(count(k >= nx) >= n_sel, nx, t) for k, nx, t in zip(keys, nxts, thrs)]
    gts = [k > t for k, t in zip(keys, thrs)]
    needs = [n_sel - count(g) for g in gts]
    ties = [(k == t) & (k > NEG_INF) for k, t in zip(keys, thrs)]
    tie_fs = [jnp.where(t, 1.0, 0.0) for t in ties]
    tri = (lax.broadcasted_iota(jnp.int32, (ps, ps), 0) <= lax.broadcasted_iota(jnp.int32, (ps, ps), 1)).astype(BF16)
    sels = []
    for i in bs:
        seen = jnp.zeros((tq, 1), F32)
        ranks = []
        for t in range(n_tiles):
            tf = tie_fs[i][:, t * ps:(t + 1) * ps]
            ranks.append(seen + jnp.dot(tf.astype(BF16), tri, preferred_element_type=F32))
            seen = seen + jnp.sum(tf, axis=1, keepdims=True)
        sels.append(jnp.where(gts[i] | (ties[i] & (jnp.concatenate(ranks, axis=1) <= needs[i])), 1.0, 0.0).astype(BF16))

    krow = lax.broadcasted_iota(jnp.int32, (ps, KVH_D * ps), 0)
    ccol = lax.broadcasted_iota(jnp.int32, (ps, KVH_D * ps), 1)
    spread = [(ccol == KVH_D * krow + j).astype(BF16) for j in range(KVH_D)]
    masks = []
    for i in bs:
        parts = []
        for t in range(n_pages):
            st = sels[i][:, t * ps:(t + 1) * ps]
            per_kv = [jnp.dot(st, spread[j], preferred_element_type=F32) for j in range(KVH_D)]
            parts.append(jnp.concatenate([per_kv[h // QPK_D] for h in range(H_D)], axis=0))
        parts.append(jnp.concatenate([sels[i][:, n_pages * ps:].astype(F32)] * H_D, axis=0))
        masks.append(jnp.concatenate(parts, axis=1) > 0.5)

    half = QPK_D * tq
    w2 = KVH_D * ps
    qs = [jnp.concatenate([q_ref[i][:, h * DH_D:(h + 1) * DH_D] for h in range(H_D)], axis=0) for i in bs]
    kns = [pad_rows(kn_ref[i]) for i in bs]
    vns = [pad_rows(vn_ref[i]) for i in bs]
    logits = []
    for i in bs:
        new_logits = jnp.concatenate([_mm_nt(qs[i][j * half:(j + 1) * half], kns[i][:, j * DH_D:(j + 1) * DH_D])
                                      for j in range(KVH_D)], axis=0)
        lg = jnp.concatenate([_mm_nt(qs[i], r[0].astype(BF16)) for r in kp_refs[i]] + [new_logits], axis=1)
        logits.append(jnp.where(masks[i], lg * (DH_D ** -0.5), NEG_BIG))
    ms = [jnp.max(x, axis=1, keepdims=True) for x in logits]
    ps_ = [jnp.where(mk, jnp.exp(x - m), 0.0) for mk, x, m in zip(masks, logits, ms)]
    ls = [jnp.sum(p, axis=1, keepdims=True) for p in ps_]
    for i in bs:
        pb = ps_[i].astype(BF16)
        acc = sum(jnp.dot(pb[:, t * w2:(t + 1) * w2], r[0].astype(BF16), preferred_element_type=F32)
                  for t, r in enumerate(vp_refs[i]))
        p_new = pb[:, n_pages * w2:]
        acc = acc + jnp.concatenate([jnp.dot(p_new[j * half:(j + 1) * half], vns[i][:, j * DH_D:(j + 1) * DH_D],
                                             preferred_element_type=F32) for j in range(KVH_D)], axis=0)
        o = acc / ls[i]
        gd = gd_ref[i]
        o_ref[i] = jnp.concatenate([o[h * tq:(h + 1) * tq] for h in range(H_D)], axis=1) * (gd * _sigmoid(gd))


def dsa_sample_pallas(proj3d, q_b, qi_b, k_b, v_b, ki_b, cache_k, cache_v, cache_ki, page_table, n_valid_tokens):
    b, tq, _ = proj3d.shape
    n_pages = page_table.shape[1]
    bb = math.gcd(b, DSA_SAMPLE_BATCH)
    n_sel = min(TOPK_MAX, (n_pages * PAGE_SIZE + n_valid_tokens) // 4)
    row = lambda w, j=0: pl.BlockSpec((bb, tq, w), lambda bi, pt, j=j: (bi, 0, j))
    pages = lambda shape: [pl.BlockSpec((1,) + shape, lambda bi, pt, i=i, p=p: (pt[bi * bb + i, p], 0, 0))
                           for i in range(bb) for p in range(n_pages)]
    grid_spec = pltpu.PrefetchScalarGridSpec(
        num_scalar_prefetch=1,
        grid=(b // bb,),
        in_specs=[row(D_D), row(QI_W), row(LANES, OD_KIWI // LANES), row(D_D, OD_GD // D_D), row(KV_W), row(KV_W), row(DI_D)]
        + pages((PAGE_SIZE * KVH_D, DH_D)) + pages((PAGE_SIZE * KVH_D, DH_D)) + pages((DI_D, PAGE_SIZE)),
        out_specs=pl.BlockSpec((bb, tq, D_D), lambda bi, pt: (bi, 0, 0)),
    )
    n_ops = bb * n_pages
    return pl.pallas_call(
        functools.partial(_dsa_sample_kernel, n_pages=n_pages, n_sel=float(n_sel), tq=tq, bb=bb),
        grid_spec=grid_spec,
        out_shape=jax.ShapeDtypeStruct((b, tq, D_D), F32),
        compiler_params=pltpu.CompilerParams(dimension_semantics=("parallel",)),
        name="dsa_sample",
    )(page_table, q_b, qi_b, proj3d, proj3d, k_b, v_b, ki_b, *([cache_k] * n_ops), *([cache_v] * n_ops),
      *([cache_ki] * n_ops))


def _split(x, sizes):
    return jnp.split(x, np.cumsum(sizes)[:-1].tolist(), axis=-1)


def even_mixer(x, g, lv, shift0, wkv0, conv0, ssm0, w_in, w_out, mu, w0, w2, a0, a2, k_k, k_a, r_k,
               ln_g, ln_b, conv_w, conv_b, dt_bias, a_log, d_skip, gnorm_g):
    b, l, d = x.shape
    pa_w, ga_w, z_w, xbc_w, dt_w = _split(w_in, [A_SHIFT, D_A, D_B, CONV_DIM, H_B])
    w_ev = jnp.concatenate([ga_w, z_w, xbc_w, pa_w, dt_w, jnp.zeros((d, EV_N - IN_E), w_in.dtype)], axis=1)
    proj = norm_matmul(x.reshape(b * l, d), g, w_ev.astype(BF16)).reshape(b, l, EV_N)
    out_a, wkv_new, shift_new = rwkv7_pallas(proj, shift0, wkv0, mu, w0, w2, a0, a2, k_k, k_a, r_k, ln_g, ln_b, lv)
    out_b, ssm_new, conv_new = ssd_pallas(proj, lv, conv0, ssm0, conv_w, conv_b, dt_bias, a_log, d_skip, gnorm_g)
    x_new = matmul_residual(out_a.reshape(b * l, D_A), out_b.reshape(b * l, D_B), w_out.astype(BF16),
                            x.reshape(b * l, d)).reshape(b, l, d)
    return x_new, (wkv_new, shift_new, ssm_new, conv_new)


def odd_mixer(x, g, lv, pos, c_re0, c_im0, attend, w_in, w_out, lam_re, lam_im, log_dt, b_re, b_im,
              c_re, c_im, d_skip, glu_w, glu_b):
    b, l, d = x.shape
    u_w, gc_w, q_w, k_w, v_w, qi_w, ki_w, wi_w, gd_w = _split(
        w_in, [D_C, D_C, D_D, KV_W, KV_W, QI_W, DI_D, HI_D, D_D])
    w_od = jnp.concatenate([u_w, gc_w, q_w, gd_w, k_w, v_w, qi_w, ki_w, wi_w,
                            jnp.zeros((d, OD_N - IN_O), w_in.dtype)], axis=1)
    proj = norm_matmul(x.reshape(b * l, d), g, w_od.astype(BF16)).reshape(b, l, OD_N)
    out_c, re_last, im_last = s5_pallas(proj, lv, c_re0, c_im0, lam_re, lam_im, log_dt, b_re, b_im,
                                        c_re, c_im, d_skip, glu_w, glu_b)
    q_b, qi_b, k_f, v_f, ki, k_b, v_b, ki_b = rope_pallas(proj, pos, v_transposed=attend is None)
    if attend is None:
        out_d = dsa_prompt_pallas(proj, q_b, qi_b, k_b, v_b, ki_b)
    else:
        out_d = dsa_sample_pallas(proj, q_b, qi_b, k_b, v_b, ki_b, *attend, lv)
    k = k_f[:, :lv].reshape(b, lv, KVH_D, DH_D)
    v = v_f[:, :lv].reshape(b, lv, KVH_D, DH_D)
    ki = ki[:, :lv]
    x_new = matmul_residual(out_c.reshape(b * l, D_C), out_d.reshape(b * l, D_D), w_out.astype(BF16),
                            x.reshape(b * l, d)).reshape(b, l, d)
    return x_new, (re_last, im_last, k, v, ki)


def kernel(x_prompt, x_sample, state_a_wkv, state_a_shift, state_b_ssm, state_b_conv, state_c_re, state_c_im, cache_d_k, cache_d_v, cache_d_kidx, page_table, norm_g, final_norm_g, w_in_e, w_out_e, rwkv_mu, rwkv_w0, rwkv_w2, rwkv_a0, rwkv_a2, rwkv_kk, rwkv_ka, rwkv_rk, rwkv_ln_g, rwkv_ln_b, ssd_conv_w, ssd_conv_b, ssd_dt_bias, ssd_a_log, ssd_d, ssd_norm_g, w_in_o, w_out_o, s5_lam_re, s5_lam_im, s5_log_dt, s5_b_re, s5_b_im, s5_c_re, s5_c_im, s5_d, s5_glu_w, s5_glu_b):
    f32 = jnp.float32
    bp, lp = x_prompt.shape[:2]
    bs, ls = x_sample.shape[:2]
    pos_p = jnp.arange(lp)
    ls_pad = _round_up(ls, 8)
    pos_s = PAST_LEN + jnp.arange(ls_pad)
    xp, xs = x_prompt, jnp.pad(x_sample, ((0, 0), (0, ls_pad - ls), (0, 0)))
    even_p, even_s, odd_p, odd_s = [], [], [], []
    for i in range(DEPTH):
        j = i // 2
        if i % 2 == 0:
            pe = (w_in_e[j], w_out_e[j], rwkv_mu[j], rwkv_w0[j], rwkv_w2[j], rwkv_a0[j], rwkv_a2[j],
                  rwkv_kk[j], rwkv_ka[j], rwkv_rk[j], rwkv_ln_g[j], rwkv_ln_b[j], ssd_conv_w[j],
                  ssd_conv_b[j], ssd_dt_bias[j], ssd_a_log[j], ssd_d[j], ssd_norm_g[j])
            xp, st_p = even_mixer(xp, norm_g[i], lp, jnp.zeros((bp, A_SHIFT), f32), jnp.zeros((bp, H_A, HD_A, HD_A), f32),
                                  jnp.zeros((bp, CONV_W - 1, CONV_DIM), f32),
                                  jnp.zeros((bp, H_B, P_B, N_B), f32), *pe)
            xs, st_s = even_mixer(xs, norm_g[i], ls, state_a_shift[j], state_a_wkv[j], state_b_conv[j], state_b_ssm[j], *pe)
            even_p.append(st_p)
            even_s.append(st_s)
        else:
            po = (w_in_o[j], w_out_o[j], s5_lam_re[j], s5_lam_im[j], s5_log_dt[j], s5_b_re[j], s5_b_im[j],
                  s5_c_re[j], s5_c_im[j], s5_d[j], s5_glu_w[j], s5_glu_b[j])
            zc = jnp.zeros((bp, G_C, P_C), f32)
            xp, st_p = odd_mixer(xp, norm_g[i], lp, pos_p, zc, zc, None, *po)
            n_pool = cache_d_k.shape[1]
            attend_s = (cache_d_k.reshape(-1, PAGE_SIZE * KVH_D, DH_D), cache_d_v.reshape(-1, PAGE_SIZE * KVH_D, DH_D),
                        jnp.swapaxes(cache_d_kidx, 2, 3).reshape(-1, DI_D, PAGE_SIZE), page_table + j * n_pool)
            xs, st_s = odd_mixer(xs, norm_g[i], ls, pos_s, state_c_re[j], state_c_im[j], attend_s, *po)
            odd_p.append(st_p)
            odd_s.append(st_s)
    y_prompt = rmsnorm_rows(xp.reshape(bp * lp, D_MODEL), final_norm_g).reshape(bp, lp, D_MODEL)
    y_sample = rmsnorm_rows(xs.reshape(bs * ls_pad, D_MODEL), final_norm_g).reshape(bs, ls_pad, D_MODEL)[:, :ls]
    new_a_wkv_p, new_a_shift_p, new_b_ssm_p, new_b_conv_p = [jnp.stack(t) for t in zip(*even_p)]
    new_a_wkv_s, new_a_shift_s, new_b_ssm_s, new_b_conv_s = [jnp.stack(t) for t in zip(*even_s)]
    new_c_re_p, new_c_im_p, new_d_k_p, new_d_v_p, new_d_kidx_p = [jnp.stack(t) for t in zip(*odd_p)]
    new_c_re_s, new_c_im_s, new_d_k_s, new_d_v_s, new_d_kidx_s = [jnp.stack(t) for t in zip(*odd_s)]
    return (y_prompt, y_sample,
            new_a_wkv_p, new_a_shift_p, new_b_ssm_p, new_b_conv_p,
            new_c_re_p, new_c_im_p, new_d_k_p, new_d_v_p, new_d_kidx_p,
            new_a_wkv_s, new_a_shift_s, new_b_ssm_s, new_b_conv_s,
            new_c_re_s, new_c_im_s, new_d_k_s, new_d_v_s, new_d_kidx_s)
```
